```python
import math
import jax, jax.numpy as jnp
from jax import lax
import numpy as np

D_MODEL = 1024
BATCH = 8
SEQ = 4096
DEPTH = 2

GRID_W = 64
CTX_LEN = 256
D_MIX = 2 * D_MODEL
LRU_WIDTH = D_MIX // 4
LRU_HEADS = 8
LRU_HEAD_DIM = LRU_WIDTH // LRU_HEADS
LRU_CONV = 4
LRU_C = 8.0
HY_WIDTH = D_MIX // 4
HY_ORDER = 2
HY_CONV = 3
HY_BANDS = 16
HY_EMB = 1 + 2 * HY_BANDS
HY_HIDDEN = 64
HY_FAST_DECAY = 0.3
HY_SLOW_DECAY = 1.5
HY_TARGET = 1e-2
HG_WIDTH = D_MIX // 2
HG_HEADS = 8
HG_KEY = HG_WIDTH // HG_HEADS
HG_VAL = HG_WIDTH // HG_HEADS
HG_CHUNK = 64
EPS = 1e-6
TINY = 1e-12
IN_SPLITS = (LRU_WIDTH, LRU_WIDTH,
             HY_WIDTH * (HY_ORDER + 1), HY_WIDTH,
             HG_WIDTH, HG_WIDTH, HG_WIDTH, HG_WIDTH, HG_WIDTH)
D_IN = sum(IN_SPLITS)

kernel_name = "hybrid_rglru_hyena_hgrn2_prefix_block"


def rms_norm(x, gain):
    xf = x.astype(jnp.float32)
    y = xf * lax.rsqrt(jnp.mean(xf * xf, axis=-1, keepdims=True) + EPS)
    return (y * gain.astype(jnp.float32)).astype(x.dtype)


def depthwise_conv(x, w, b, pad_left, pad_right):
    y = lax.conv_general_dilated(x, w[:, None, :].astype(x.dtype), window_strides=(1,),
                                 padding=[(pad_left, pad_right)],
                                 dimension_numbers=("NWC", "WIO", "NWC"),
                                 feature_group_count=x.shape[-1])
    return y + b.astype(x.dtype)


def to_column_major(x, rows):
    b, t, ch = x.shape
    return x.reshape(b, rows, GRID_W, ch).transpose(0, 2, 1, 3).reshape(b, t, ch)


def to_row_major(x, rows):
    b, t, ch = x.shape
    return x.reshape(b, GRID_W, rows, ch).transpose(0, 2, 1, 3).reshape(b, t, ch)


def linear_scan(a, b, h0):
    b = b.at[:, 0].add(a[:, 0] * h0)

    def combine(left, right):
        a_l, b_l = left
        a_r, b_r = right
        return a_l * a_r, a_r * b_l + b_r

    _, h = lax.associative_scan(combine, (a, b), axis=1)
    return h


def rglru_coeffs(u, w_a, b_a, w_x, b_x, lam):
    bsz, t, _ = u.shape
    uh = u.reshape(bsz, t, LRU_HEADS, LRU_HEAD_DIM)
    r = jax.nn.sigmoid(jnp.einsum("bthi,hij->bthj", uh, w_a).reshape(bsz, t, LRU_WIDTH) + b_a)
    gi = jax.nn.sigmoid(jnp.einsum("bthi,hij->bthj", uh, w_x).reshape(bsz, t, LRU_WIDTH) + b_x)
    log_a = -LRU_C * r * jax.nn.softplus(-lam)
    a = jnp.exp(log_a)
    b = jnp.sqrt(jnp.maximum(-jnp.expm1(2.0 * log_a), TINY)) * (gi * u)
    return a, b


def rglru_bidir(u, h0_f, h0_b, w_a, b_a, w_x, b_x, lam):
    a_f, b_f = rglru_coeffs(u, w_a[0], b_a[0], w_x[0], b_x[0], lam[0])
    a_b, b_b = rglru_coeffs(u[:, ::-1], w_a[1], b_a[1], w_x[1], b_x[1], lam[1])
    h_f = linear_scan(a_f, b_f, h0_f)
    h_b = linear_scan(a_b, b_b, h0_b)
    return h_f + h_b[:, ::-1], h_f[:, -1], h_b[:, -1]


def hgrn2_gates(z, lb):
    f = lb + (1.0 - lb) * jax.nn.sigmoid(z)
    return jnp.log(jnp.maximum(f, TINY)), (1.0 - lb) * jax.nn.sigmoid(-z)


def hgrn2_chunk_scan(q, log_f, k, v, s0):
    bsz, t, nh, _ = q.shape
    dv = v.shape[-1]
    n_chunks = t // HG_CHUNK

    def chunks(z):
        return z.reshape(bsz, n_chunks, HG_CHUNK, nh, z.shape[-1]).swapaxes(0, 1)

    causal = jnp.tril(jnp.ones((HG_CHUNK, HG_CHUNK), dtype=bool))[None, :, :, None, None]

    def step(s, inp):
        qc, gc, kc, vc = inp
        g_cum = jnp.cumsum(gc, axis=1)
        o_inter = jnp.einsum("bchk,bhkv->bchv", qc * jnp.exp(g_cum), s)
        rel = jnp.where(causal, g_cum[:, :, None] - g_cum[:, None, :], 0.0)
        decay = jnp.where(causal, jnp.exp(rel), 0.0)
        scores = jnp.einsum("bjhk,bjshk,bshk->bhjs", qc, decay, kc)
        o_intra = jnp.einsum("bhjs,bshv->bjhv", scores, vc)
        g_last = g_cum[:, -1]
        s_new = jnp.exp(g_last)[..., None] * s + jnp.einsum(
            "bshk,bshv->bhkv", kc * jnp.exp(g_last[:, None] - g_cum), vc)
        return s_new, o_inter + o_intra

    s_fin, o = lax.scan(step, s0, (chunks(q), chunks(log_f), chunks(k), chunks(v)))
    return o.swapaxes(0, 1).reshape(bsz, t, nh, dv), s_fin


def hyena_filter_spectra(length, w1, b1, w2, b2, w3, freq):
    f32 = jnp.float32
    pos = jnp.arange(length, dtype=f32)
    t = pos / length
    bands = jnp.linspace(1e-4, HY_BANDS - 1, HY_BANDS, dtype=f32)
    ang = (2.0 * math.pi / length) * pos[:, None] * bands[None, :]
    z = jnp.concatenate([t[:, None], jnp.cos(ang), jnp.sin(ang)], axis=-1)
    fr = freq.astype(f32)
    hdn = jnp.sin(fr * (z @ w1.astype(f32) + b1.astype(f32)))
    hdn = jnp.sin(fr * (hdn @ w2.astype(f32) + b2.astype(f32)))
    h = (hdn @ w3.astype(f32)).reshape(length, HY_ORDER, 2, HY_WIDTH)
    deltas = jnp.abs(jnp.linspace(math.log(HY_TARGET) / HY_SLOW_DECAY,
                                  math.log(HY_TARGET) / HY_FAST_DECAY, HY_WIDTH, dtype=f32))
    h = h * jnp.exp(-t[:, None] * deltas[None, :])[:, None, None, :]
    full = jnp.concatenate([h[:, :, 0], jnp.zeros((1, HY_ORDER, HY_WIDTH), f32), h[:0:-1, :, 1]], axis=0)
    full = full * lax.rsqrt(jnp.sum(full * full, axis=0, keepdims=True) + TINY)
    return jnp.fft.rfft(full, n=2 * length, axis=0)


def long_conv(z, spec, skip):
    length = z.shape[1]
    y = jnp.fft.irfft(jnp.fft.rfft(z, n=2 * length, axis=1) * spec[None], n=2 * length, axis=1)[:, :length]
    return y + skip * z


def hyena_order2(u, conv_w, conv_b, spec, skip):
    u = depthwise_conv(u, conv_w, conv_b, 1, 1).astype(jnp.float32)
    v, x1, x2 = jnp.split(u, 3, axis=-1)
    z = x1 * long_conv(v, spec[:, 0], skip[0])
    return x2 * long_conv(z, spec[:, 1], skip[1])


def token_mixers(h, init, grid_rows, with_output, w_in, lru_conv_w, lru_conv_b, lru_wa, lru_ba,
                 lru_wx, lru_bx, lru_lam, hy_conv_w, hy_conv_b, hy_w1, hy_b1, hy_w2, hy_b2, hy_w3,
                 hy_freq, hy_skip, hg_lb, hg_norm):
    f32 = jnp.float32
    bsz, t, _ = h.shape
    proj = jnp.einsum("btd,de->bte", h, w_in)
    xa, ga, ub, gb, q, zf, zb, vi, gc = jnp.split(proj, np.cumsum(IN_SPLITS)[:-1].tolist(), axis=-1)
    if grid_rows is not None:
        xa = to_column_major(xa, grid_rows)
    ua = depthwise_conv(xa, lru_conv_w, lru_conv_b, 2, 1).astype(f32)
    ya, ha_f, ha_b = rglru_bidir(ua, init[0], init[1], lru_wa, lru_ba, lru_wx, lru_bx, lru_lam)
    shp = (bsz, t, HG_HEADS, HG_KEY)
    qh = q.astype(f32).reshape(shp)
    vh = vi.astype(f32).reshape(bsz, t, HG_HEADS, HG_VAL)
    log_f_f, k_f = hgrn2_gates(zf.astype(f32).reshape(shp), hg_lb[0])
    log_f_b, k_b = hgrn2_gates(zb.astype(f32).reshape(shp), hg_lb[1])
    o_f, s_f = hgrn2_chunk_scan(qh, log_f_f, k_f, vh, init[2])
    o_b, s_b = hgrn2_chunk_scan(qh[:, ::-1], log_f_b[:, ::-1], k_b[:, ::-1], vh[:, ::-1], init[3])
    states = (ha_f, ha_b, s_f, s_b)
    if not with_output:
        return None, states
    if grid_rows is not None:
        ya = to_row_major(ya, grid_rows)
    ya = ya * jax.nn.silu(ga.astype(f32))
    spec = hyena_filter_spectra(t, hy_w1, hy_b1, hy_w2, hy_b2, hy_w3, hy_freq)
    yb = hyena_order2(ub, hy_conv_w, hy_conv_b, spec, hy_skip) * jax.nn.silu(gb.astype(f32))
    oc = o_f + o_b[:, ::-1]
    oc = oc * lax.rsqrt(jnp.mean(oc * oc, axis=-1, keepdims=True) + EPS) * hg_norm.reshape(HG_HEADS, HG_VAL)
    yc = oc.reshape(bsz, t, HG_WIDTH) * jax.nn.silu(gc.astype(f32))
    return jnp.concatenate([ya, yb, yc], axis=-1).astype(h.dtype), states


def setup_inputs(seed: int = 0) -> dict:
    key = jax.random.key(seed)
    ks = jax.random.split(key, 32)
    nrm = jax.random.normal
    f32 = jnp.float32
    a_c = jax.random.uniform(ks[15], (DEPTH, 2, LRU_WIDTH), f32, minval=0.9, maxval=0.999)
    a_base = a_c ** (1.0 / LRU_C)
    return {
        "x": nrm(ks[0], (BATCH, SEQ, D_MODEL), f32),
        "c": nrm(ks[1], (BATCH, D_MODEL), f32),
        "ctx": nrm(ks[2], (BATCH, CTX_LEN, D_MODEL), f32),
        "c_ctx": nrm(ks[3], (D_MODEL,), f32),
        "norm_g": 1.0 + 0.02 * nrm(ks[4], (DEPTH, D_MODEL), f32),
        "w_mod": 0.3 * D_MODEL ** -0.5 * nrm(ks[5], (DEPTH, D_MODEL, 3 * D_MODEL), f32),
        "b_mod": 0.02 * nrm(ks[6], (DEPTH, 3 * D_MODEL), f32),
        "w_in": D_MODEL ** -0.5 * nrm(ks[7], (DEPTH, D_MODEL, D_IN), f32),
        "w_out": D_MIX ** -0.5 * nrm(ks[8], (DEPTH, D_MIX, D_MODEL), f32),
        "lru_conv_w": LRU_CONV ** -0.5 * nrm(ks[9], (DEPTH, LRU_CONV, LRU_WIDTH), f32),
        "lru_conv_b": 0.02 * nrm(ks[10], (DEPTH, LRU_WIDTH), f32),
        "lru_wa": LRU_HEAD_DIM ** -0.5 * nrm(ks[11], (DEPTH, 2, LRU_HEADS, LRU_HEAD_DIM, LRU_HEAD_DIM), f32),
        "lru_ba": 0.02 * nrm(ks[12], (DEPTH, 2, LRU_WIDTH), f32),
        "lru_wx": LRU_HEAD_DIM ** -0.5 * nrm(ks[13], (DEPTH, 2, LRU_HEADS, LRU_HEAD_DIM, LRU_HEAD_DIM), f32),
        "lru_bx": 0.02 * nrm(ks[14], (DEPTH, 2, LRU_WIDTH), f32),
        "lru_lam": jnp.log(a_base) - jnp.log1p(-a_base),
        "hy_conv_w": HY_CONV ** -0.5 * nrm(ks[16], (DEPTH, HY_CONV, HY_WIDTH * (HY_ORDER + 1)), f32),
        "hy_conv_b": 0.02 * nrm(ks[17], (DEPTH, HY_WIDTH * (HY_ORDER + 1)), f32),
        "hy_w1": HY_EMB ** -0.5 * nrm(ks[18], (DEPTH, HY_EMB, HY_HIDDEN), f32),
        "hy_b1": 0.02 * nrm(ks[19], (DEPTH, HY_HIDDEN), f32),
        "hy_w2": HY_HIDDEN ** -0.5 * nrm(ks[20], (DEPTH, HY_HIDDEN, HY_HIDDEN), f32),
        "hy_b2": 0.02 * nrm(ks[21], (DEPTH, HY_HIDDEN), f32),
        "hy_w3": HY_HIDDEN ** -0.5 * nrm(ks[22], (DEPTH, HY_HIDDEN, HY_ORDER * 2 * HY_WIDTH), f32),
        "hy_freq": 1.0 + 0.05 * nrm(ks[23], (DEPTH, HY_HIDDEN), f32),
        "hy_skip": nrm(ks[24], (DEPTH, HY_ORDER, HY_WIDTH), f32),
        "hg_lb_logits": 0.1 * nrm(ks[25], (DEPTH, 2, HG_WIDTH), f32),
        "hg_norm": 1.0 + 0.02 * nrm(ks[26], (DEPTH, HG_WIDTH), f32),
        "final_g": 1.0 + 0.02 * nrm(ks[27], (D_MODEL,), f32),
    }


def reference(x, c, ctx, c_ctx, norm_g, w_mod, b_mod, w_in, w_out, lru_conv_w, lru_conv_b, lru_wa,
              lru_ba, lru_wx, lru_bx, lru_lam, hy_conv_w, hy_conv_b, hy_w1, hy_b1, hy_w2, hy_b2, hy_w3,
              hy_freq, hy_skip, hg_lb_logits, hg_norm, final_g):
    bsz, seq, _ = x.shape
    rows = seq // GRID_W
    p = jax.nn.softmax(hg_lb_logits.astype(jnp.float32), axis=0)
    lower_bounds = jnp.cumsum(p, axis=0) - p[0]
    silu_c = jax.nn.silu(c)
    silu_cc = jax.nn.silu(c_ctx)
    zero_states = (jnp.zeros((bsz, LRU_WIDTH), jnp.float32), jnp.zeros((bsz, LRU_WIDTH), jnp.float32),
                   jnp.zeros((bsz, HG_HEADS, HG_KEY, HG_VAL), jnp.float32),
                   jnp.zeros((bsz, HG_HEADS, HG_KEY, HG_VAL), jnp.float32))
    xc = ctx
    for layer in range(DEPTH):
        last = layer == DEPTH - 1
        mod = silu_c @ w_mod[layer] + b_mod[layer]
        mod_c = silu_cc @ w_mod[layer] + b_mod[layer]
        shift, scale, gate = jnp.split(mod[:, None, :], 3, axis=-1)
        shift_c, scale_c, gate_c = jnp.split(mod_c[None, None, :], 3, axis=-1)
        lp = (w_in[layer], lru_conv_w[layer], lru_conv_b[layer], lru_wa[layer], lru_ba[layer],
              lru_wx[layer], lru_bx[layer], lru_lam[layer], hy_conv_w[layer], hy_conv_b[layer],
              hy_w1[layer], hy_b1[layer], hy_w2[layer], hy_b2[layer], hy_w3[layer], hy_freq[layer],
              hy_skip[layer], lower_bounds[layer].reshape(2, HG_HEADS, HG_KEY), hg_norm[layer])
        hc = rms_norm(xc, norm_g[layer]) * (1.0 + scale_c) + shift_c
        yc, ctx_states = token_mixers(hc, zero_states, None, not last, *lp)
        h = rms_norm(x, norm_g[layer]) * (1.0 + scale) + shift
        y, _ = token_mixers(h, ctx_states, rows, True, *lp)
        x = x + gate * jnp.einsum("btm,md->btd", y, w_out[layer])
        if not last:
            xc = xc + gate_c * jnp.einsum("btm,md->btd", yc, w_out[layer])
    return rms_norm(x, final_g)
```

```python
import functools
import math

import numpy as np
import jax
import jax.numpy as jnp
from jax import lax
from jax.experimental import pallas as pl
from jax.experimental.pallas import tpu as pltpu

GRID_W = 64
LRU_HEADS = 8
LRU_CONV = 4
LRU_C = 8.0
HY_ORDER = 2
HY_CONV = 3
HY_BANDS = 16
HY_FAST_DECAY = 0.3
HY_SLOW_DECAY = 1.5
HY_TARGET = 1e-2
HG_HEADS = 8
HG_CHUNK = 64
EPS = 1e-6
TINY = 1e-12

LANES = 128
SUBLANES = 8
VMEM_LIMIT_BYTES = 56 * 1024 * 1024

F32 = jnp.float32
BF16 = jnp.bfloat16


def _cparams(semantics):
    return pltpu.CompilerParams(dimension_semantics=semantics, vmem_limit_bytes=VMEM_LIMIT_BYTES)


def _sigmoid(x):
    return 1.0 / (1.0 + jnp.exp(-x))


def _silu(x):
    return x * _sigmoid(x)


def _mod_kernel(c_ref, w_ref, b_ref, o_ref):
    s = _silu(c_ref[...])
    o_ref[...] = jnp.dot(s, w_ref[...], preferred_element_type=F32,
                         precision=lax.Precision.HIGHEST) + b_ref[...]


def _modulation(cvec, w_mod, b_mod):
    depth, d, d3 = w_mod.shape
    r = cvec.shape[0]
    return pl.pallas_call(
        _mod_kernel,
        out_shape=jax.ShapeDtypeStruct((depth, r, d3), F32),
        grid=(depth, d3 // d),
        in_specs=[
            pl.BlockSpec((r, d), lambda l, j: (0, 0)),
            pl.BlockSpec((None, d, d), lambda l, j: (l, 0, j)),
            pl.BlockSpec((None, 1, d), lambda l, j: (l, 0, j)),
        ],
        out_specs=pl.BlockSpec((None, r, d), lambda l, j: (l, 0, j)),
        compiler_params=_cparams(("arbitrary", "arbitrary")),
        name="adaln_modulation",
    )(cvec, w_mod, b_mod.reshape(depth, 1, d3))


def _inproj_kernel(x_ref, mod_ref, g_ref, w_ref, o_ref, h_s):
    @pl.when(pl.program_id(2) == 0)
    def _():
        x = x_ref[...]
        ms = jnp.mean(x * x, axis=-1, keepdims=True)
        y = x * lax.rsqrt(ms + EPS) * g_ref[...]
        h = y * (1.0 + mod_ref[1:2, :]) + mod_ref[0:1, :]
        h_s[...] = h.astype(BF16)

    o_ref[...] = jnp.dot(h_s[...], w_ref[...], preferred_element_type=F32)


def _in_projection(x, mod, gain, w_bf16, tm, tn):
    bsz, t, d = x.shape
    d_in = w_bf16.shape[1]
    tm = min(tm, t)
    return pl.pallas_call(
        _inproj_kernel,
        out_shape=jax.ShapeDtypeStruct((bsz, t, d_in), F32),
        grid=(bsz, t // tm, d_in // tn),
        in_specs=[
            pl.BlockSpec((None, tm, d), lambda b, i, j: (b, i, 0)),
            pl.BlockSpec((None, 3, d), lambda b, i, j: (b, 0, 0)),
            pl.BlockSpec((1, d), lambda b, i, j: (0, 0)),
            pl.BlockSpec((d, tn), lambda b, i, j: (0, j)),
        ],
        out_specs=pl.BlockSpec((None, tm, tn), lambda b, i, j: (b, i, j)),
        scratch_shapes=[pltpu.VMEM((tm, d), BF16)],
        compiler_params=_cparams(("arbitrary", "arbitrary", "arbitrary")),
        name="rmsnorm_adaln_inproj",
    )(x, mod, gain.reshape(1, d), w_bf16)


def _lru_kernel(xprev_ref, x_ref, xnext_ref, h0_ref, cw_ref, cb_ref, wa_ref, ba_ref, wx_ref,
                bx_ref, lam_ref, h_ref, hfin_ref, xpad, a_s, b_s, hstate):
    d = pl.program_id(0)
    i = pl.program_id(1)
    n = pl.num_programs(1)
    idx = jnp.where(d == 0, i, n - 1 - i)
    tp, bsz, ch = x_ref.shape

    @pl.when(i == 0)
    def _():
        hstate[...] = h0_ref[...]

    xpad[0:2] = jnp.where(idx > 0, xprev_ref[...], 0.0)
    xpad[2:2 + tp] = x_ref[...]
    xpad[2 + tp:3 + tp] = jnp.where(idx < n - 1, xnext_ref[...], 0.0)
    u = cb_ref[...] + cw_ref[0:1, :] * xpad[0:tp]
    for k in range(1, LRU_CONV):
        u = u + cw_ref[k:k + 1, :] * xpad[k:k + tp]
    u2 = u.reshape(tp * bsz, ch)
    ub = u2.astype(BF16)
    r = _sigmoid(jnp.dot(ub, wa_ref[...], preferred_element_type=F32) + ba_ref[...])
    gi = _sigmoid(jnp.dot(ub, wx_ref[...], preferred_element_type=F32) + bx_ref[...])
    nlam = -lam_ref[...]
    softplus = jnp.maximum(nlam, 0.0) + jnp.log(1.0 + jnp.exp(-jnp.abs(nlam)))
    log_a = (-LRU_C) * r * softplus
    a = jnp.exp(log_a)
    bb = jnp.sqrt(jnp.maximum(1.0 - a * a, TINY)) * (gi * u2)
    a_s[...] = a.reshape(tp, bsz, ch)
    b_s[...] = bb.reshape(tp, bsz, ch)

    def body(t, h):
        tt = jnp.where(d == 0, t, tp - 1 - t)
        h = a_s[tt] * h + b_s[tt]
        h_ref[tt] = h
        return h

    h = lax.fori_loop(0, tp, body, hstate[...], unroll=8)
    hstate[...] = h
    hfin_ref[...] = h


def _rglru(x_tm, h0, conv_w, conv_b, wa_dense, ba, wx_dense, bx, lam, tp):
    p, bsz, ch = x_tm.shape
    tp = min(tp, p)
    n = p // tp

    def tile(d, i):
        return jnp.where(d == 0, i, n - 1 - i)

    vec = lambda a: a.reshape(2, 1, ch)
    return pl.pallas_call(
        _lru_kernel,
        out_shape=(jax.ShapeDtypeStruct((2, p, bsz, ch), F32),
                   jax.ShapeDtypeStruct((2, bsz, ch), F32)),
        grid=(2, n),
        in_specs=[
            pl.BlockSpec((2, bsz, ch), lambda d, i: (jnp.maximum(tile(d, i) * (tp // 2) - 1, 0), 0, 0)),
            pl.BlockSpec((tp, bsz, ch), lambda d, i: (tile(d, i), 0, 0)),
            pl.BlockSpec((1, bsz, ch), lambda d, i: (jnp.minimum((tile(d, i) + 1) * tp, p - 1), 0, 0)),
            pl.BlockSpec((None, bsz, ch), lambda d, i: (d, 0, 0)),
            pl.BlockSpec((LRU_CONV, ch), lambda d, i: (0, 0)),
            pl.BlockSpec((1, ch), lambda d, i: (0, 0)),
            pl.BlockSpec((None, ch, ch), lambda d, i: (d, 0, 0)),
            pl.BlockSpec((None, 1, ch), lambda d, i: (d, 0, 0)),
            pl.BlockSpec((None, ch, ch), lambda d, i: (d, 0, 0)),
            pl.BlockSpec((None, 1, ch), lambda d, i: (d, 0, 0)),
            pl.BlockSpec((None, 1, ch), lambda d, i: (d, 0, 0)),
        ],
        out_specs=(
            pl.BlockSpec((None, tp, bsz, ch), lambda d, i: (d, tile(d, i), 0, 0)),
            pl.BlockSpec((None, bsz, ch), lambda d, i: (d, 0, 0)),
        ),
        scratch_shapes=[
            pltpu.VMEM((tp + LRU_CONV - 1, bsz, ch), F32),
            pltpu.VMEM((tp, bsz, ch), F32),
            pltpu.VMEM((tp, bsz, ch), F32),
            pltpu.VMEM((bsz, ch), F32),
        ],
        compiler_params=_cparams(("arbitrary", "arbitrary")),
        name="rglru_bidir_scan",
    )(x_tm, x_tm, x_tm, h0, conv_w, conv_b.reshape(1, ch), wa_dense, vec(ba), wx_dense, vec(bx),
      vec(lam))


def _hgrn2_kernel(q_ref, z_ref, v_ref, lb_ref, s0_ref, o_ref, sfin_ref, st):
    d = pl.program_id(0)
    i = pl.program_id(2)
    tT = q_ref.shape[0]
    nh, dv, dk = st.shape
    c = HG_CHUNK
    nch = tT // c

    @pl.when(i == 0)
    def _():
        st[...] = s0_ref[...]

    row = lax.broadcasted_iota(jnp.int32, (c, c), 0)
    col = lax.broadcasted_iota(jnp.int32, (c, c), 1)
    fwd = d == 0
    keep = (col - row) * jnp.where(fwd, 1, -1) <= 0
    tri = jnp.where(keep, 1.0, 0.0)
    lb = lb_ref[...]

    def chunk(j, carry):
        cj = jnp.where(fwd, j, nch - 1 - j)
        rows = pl.ds(pl.multiple_of(cj * c, c), c)
        z = z_ref[rows, :]
        e = jnp.exp(-jnp.abs(z))
        s_big = 1.0 / (1.0 + e)
        s_small = e * s_big
        sig = jnp.where(z >= 0, s_big, s_small)
        sig_neg = jnp.where(z >= 0, s_small, s_big)
        f = lb + (1.0 - lb) * sig
        g = jnp.log(jnp.maximum(f, TINY))
        kk = (1.0 - lb) * sig_neg
        gc = jnp.dot(tri, g, preferred_element_type=F32, precision=lax.Precision.HIGHEST)
        ref_row = gc[c // 2:c // 2 + 1, :]
        tot = jnp.where(fwd, gc[c - 1:c, :], gc[0:1, :])
        e_q = jnp.exp(gc - ref_row)
        e_k = jnp.exp(ref_row - gc)
        q = q_ref[rows, :]
        qp = q * e_q
        kp = kk * e_k
        qpp = (qp * jnp.exp(ref_row)).astype(BF16)
        kpp = (kp * jnp.exp(tot - ref_row)).astype(BF16)
        qp = qp.astype(BF16)
        kp = kp.astype(BF16)
        vb = v_ref[rows, :].astype(BF16)
        dec = jnp.exp(tot)
        for h in range(nh):
            sl = slice(h * dk, (h + 1) * dk)
            sv = slice(h * dv, (h + 1) * dv)
            sc = lax.dot_general(qp[:, sl], kp[:, sl], (((1,), (1,)), ((), ())),
                                 preferred_element_type=F32)
            sc = jnp.where(keep, sc, 0.0).astype(BF16)
            s_t = st[h]
            o = jnp.dot(sc, vb[:, sv], preferred_element_type=F32)
            o = o + lax.dot_general(qpp[:, sl], s_t.astype(BF16), (((1,), (1,)), ((), ())),
                                    preferred_element_type=F32)
            o_ref[rows, sv] = o
            upd = lax.dot_general(vb[:, sv], kpp[:, sl], (((0,), (0,)), ((), ())),
                                  preferred_element_type=F32)
            st[h] = s_t * dec[:, sl] + upd
        return carry

    lax.fori_loop(0, nch, chunk, 0)
    sfin_ref[...] = st[...]


def _hgrn2(proj, q_blk, z_blk, v_blk, lb, s0, tT):
    bsz, t, _ = proj.shape
    _, _, nh, dv, dk = s0.shape
    w = nh * dk
    tT = min(tT, t)
    n = t // tT

    def tile(d, i):
        return jnp.where(d == 0, i, n - 1 - i)

    return pl.pallas_call(
        _hgrn2_kernel,
        out_shape=(jax.ShapeDtypeStruct((2, bsz, t, nh * dv), F32),
                   jax.ShapeDtypeStruct((2, bsz, nh, dv, dk), F32)),
        grid=(2, bsz, n),
        in_specs=[
            pl.BlockSpec((None, tT, w), lambda d, b, i: (b, tile(d, i), q_blk)),
            pl.BlockSpec((None, tT, w), lambda d, b, i: (b, tile(d, i), z_blk + d)),
            pl.BlockSpec((None, tT, w), lambda d, b, i: (b, tile(d, i), v_blk)),
            pl.BlockSpec((None, 1, w), lambda d, b, i: (d, 0, 0)),
            pl.BlockSpec((None, None, nh, dv, dk), lambda d, b, i: (d, b, 0, 0, 0)),
        ],
        out_specs=(
            pl.BlockSpec((None, None, tT, nh * dv), lambda d, b, i: (d, b, tile(d, i), 0)),
            pl.BlockSpec((None, None, nh, dv, dk), lambda d, b, i: (d, b, 0, 0, 0)),
        ),
        scratch_shapes=[pltpu.VMEM((nh, dv, dk), F32)],
        compiler_params=_cparams(("arbitrary", "arbitrary", "arbitrary")),
        name="hgrn2_bidir_chunk_scan",
    )(proj, proj, proj, lb.reshape(2, 1, w), s0)


DFT_SPLIT = 64


def _odd_dft_tables(length):
    period = 4 * length
    unit = 2.0 * math.pi / period
    k2 = 2 * jnp.arange(length, dtype=jnp.int32) + 1
    t_hi = DFT_SPLIT * jnp.arange(length // DFT_SPLIT, dtype=jnp.int32)
    t_lo = jnp.arange(DFT_SPLIT, dtype=jnp.int32)

    def cs(ph):
        ang = (ph % period).astype(F32) * unit
        return jnp.cos(ang), jnp.sin(ang)

    c1, s1 = cs(k2[:, None] * t_hi[None, :])
    c0, s0 = cs(k2[:, None] * t_lo[None, :])
    fc = (c1[:, :, None] * c0[:, None, :] - s1[:, :, None] * s0[:, None, :]).reshape(length, length)
    fs = (s1[:, :, None] * c0[:, None, :] + c1[:, :, None] * s0[:, None, :]).reshape(length, length)
    t_all = jnp.arange(length, dtype=jnp.int32)
    ka = 2 * DFT_SPLIT * jnp.arange(length // DFT_SPLIT, dtype=jnp.int32)
    kb = 2 * jnp.arange(DFT_SPLIT, dtype=jnp.int32) + 1
    c1, s1 = cs((t_all[:, None] % period) * ka[None, :] % period)
    c0, s0 = cs(t_all[:, None] * kb[None, :])
    fct = (c1[:, :, None] * c0[:, None, :] - s1[:, :, None] * s0[:, None, :]).reshape(length, length)
    fst = (s1[:, :, None] * c0[:, None, :] + c1[:, :, None] * s0[:, None, :]).reshape(length, length)
    return tuple(a.astype(BF16) for a in (fc, fs, fct, fst))


def _filter_kernel(emb_ref, w1_ref, b1_ref, w2_ref, b2_ref, w3_ref, fr_ref, dl_ref,
                   a_ref, b_ref, ss_ref, *, length):
    i = pl.program_id(0)
    tl = emb_ref.shape[0]
    cw = dl_ref.shape[1]
    fr = fr_ref[...]
    hi = lax.Precision.HIGHEST
    hdn = jnp.sin(fr * (jnp.dot(emb_ref[...], w1_ref[...], preferred_element_type=F32,
                                precision=hi) + b1_ref[...]))
    hdn = jnp.sin(fr * (jnp.dot(hdn, w2_ref[...], preferred_element_type=F32,
                                precision=hi) + b2_ref[...]))
    h = jnp.dot(hdn, w3_ref[...], preferred_element_type=F32, precision=hi)
    pos = (lax.broadcasted_iota(jnp.int32, (tl, 1), 0) + i * tl)
    t = pos.astype(F32) / float(length)
    win = jnp.exp(-t * dl_ref[...])
    not_first = pos > 0

    @pl.when(i == 0)
    def _():
        ss_ref[...] = jnp.zeros_like(ss_ref)

    for o in range(HY_ORDER):
        hpos = h[:, (2 * o) * cw:(2 * o + 1) * cw] * win
        hneg = jnp.where(not_first, h[:, (2 * o + 1) * cw:(2 * o + 2) * cw] * win, 0.0)
        a_ref[:, o * cw:(o + 1) * cw] = (hpos + hneg).astype(BF16)
        b_ref[:, o * cw:(o + 1) * cw] = (hneg - hpos).astype(BF16)
        ss_ref[:, o * cw:(o + 1) * cw] += jnp.sum(hpos * hpos + hneg * hneg, axis=0, keepdims=True)


def _spectrum_kernel(fc_ref, fs_ref, a_ref, b_ref, ss_ref, hr_ref, hi_ref, *, scale):
    s = lax.rsqrt(ss_ref[...] + TINY) * scale
    hr_ref[...] = jnp.dot(fc_ref[...], a_ref[...], preferred_element_type=F32) * s
    hi_ref[...] = jnp.dot(fs_ref[...], b_ref[...], preferred_element_type=F32) * s


def _hyena_spectra(length, fc, fs, w1, b1, w2, b2, w3, freq, width, tl, tk):
    f32 = F32
    pos = jnp.arange(length, dtype=f32)
    t = pos / length
    bands = jnp.linspace(1e-4, HY_BANDS - 1, HY_BANDS, dtype=f32)
    ang = (2.0 * math.pi / length) * pos[:, None] * bands[None, :]
    emb = jnp.concatenate([t[:, None], jnp.cos(ang), jnp.sin(ang)], axis=-1)
    deltas = jnp.abs(jnp.linspace(math.log(HY_TARGET) / HY_SLOW_DECAY,
                                  math.log(HY_TARGET) / HY_FAST_DECAY, width, dtype=f32))
    n_emb = -(-emb.shape[1] // LANES) * LANES
    w1 = jnp.pad(w1.astype(f32), ((0, n_emb - emb.shape[1]), (0, 0)))
    emb = jnp.pad(emb, ((0, 0), (0, n_emb - emb.shape[1])))
    hid = w1.shape[1]
    cols = HY_ORDER * width
    tl = min(tl, length)
    tk = min(tk, length)
    full = lambda *s: pl.BlockSpec(s, lambda i: (0,) * len(s))
    a_un, b_un, ss = pl.pallas_call(
        functools.partial(_filter_kernel, length=length),
        out_shape=(jax.ShapeDtypeStruct((length, cols), BF16),
                   jax.ShapeDtypeStruct((length, cols), BF16),
                   jax.ShapeDtypeStruct((1, cols), f32)),
        grid=(length // tl,),
        in_specs=[
            pl.BlockSpec((tl, n_emb), lambda i: (i, 0)),
            full(n_emb, hid), full(1, hid), full(hid, hid), full(1, hid),
            full(hid, 2 * cols), full(1, hid), full(1, width),
        ],
        out_specs=(pl.BlockSpec((tl, cols), lambda i: (i, 0)),
                   pl.BlockSpec((tl, cols), lambda i: (i, 0)),
                   pl.BlockSpec((1, cols), lambda i: (0, 0))),
        compiler_params=_cparams(("arbitrary",)),
        name="hyena_filter_mlp",
    )(emb, w1, b1.reshape(1, hid), w2, b2.reshape(1, hid), w3, freq.reshape(1, hid),
      deltas.reshape(1, width))
    tn = min(cols, 512)
    return pl.pallas_call(
        functools.partial(_spectrum_kernel, scale=1.0 / length),
        out_shape=(jax.ShapeDtypeStruct((length, cols), f32),
                   jax.ShapeDtypeStruct((length, cols), f32)),
        grid=(cols // tn, length // tk),
        in_specs=[
            pl.BlockSpec((tk, length), lambda j, k: (k, 0)),
            pl.BlockSpec((tk, length), lambda j, k: (k, 0)),
            pl.BlockSpec((length, tn), lambda j, k: (0, j)),
            pl.BlockSpec((length, tn), lambda j, k: (0, j)),
            pl.BlockSpec((1, tn), lambda j, k: (0, j)),
        ],
        out_specs=(pl.BlockSpec((tk, tn), lambda j, k: (k, j)),
                   pl.BlockSpec((tk, tn), lambda j, k: (k, j))),
        compiler_params=_cparams(("arbitrary", "arbitrary")),
        name="hyena_filter_spectrum",
    )(fc, fs, a_un, b_un, ss)


def _short_conv(x, w_ref, b_ref):
    length = x.shape[0]
    rows = lax.broadcasted_iota(jnp.int32, (length, 1), 0)
    prev = jnp.where(rows > 0, pltpu.roll(x, 1, axis=0), 0.0)
    nxt = jnp.where(rows < length - 1, pltpu.roll(x, length - 1, axis=0), 0.0)
    return w_ref[0:1, :] * prev + w_ref[1:2, :] * x + w_ref[2:3, :] * nxt + b_ref[...]


def _hyena_conv_kernel(*refs, conv_a, has_gate):
    it = iter(refs)
    a_ref = next(it)
    aw_ref = ab_ref = None
    if conv_a:
        aw_ref, ab_ref = next(it), next(it)
    m_ref, mw_ref, mb_ref = next(it), next(it), next(it)
    g_ref = next(it) if has_gate else None
    skip_ref, fc_ref, fs_ref, fct_ref, fst_ref, hr_ref, hi_ref = (next(it) for _ in range(7))
    o_ref = next(it)
    a_bf = next(it)
    k = pl.program_id(2)
    nk = pl.num_programs(2)

    def a_value():
        a = a_ref[...]
        return _short_conv(a, aw_ref, ab_ref) if conv_a else a

    @pl.when(k == 0)
    def _():
        a_bf[...] = a_value().astype(BF16)
        o_ref[...] = jnp.zeros_like(o_ref)

    ab = a_bf[...]
    zr = jnp.dot(fc_ref[...], ab, preferred_element_type=F32)
    zs = jnp.dot(fs_ref[...], ab, preferred_element_type=F32)
    hr = hr_ref[...]
    hi = hi_ref[...]
    yr = (zr * hr + zs * hi).astype(BF16)
    yi = (zr * hi - zs * hr).astype(BF16)
    o_ref[...] += (jnp.dot(fct_ref[...], yr, preferred_element_type=F32)
                   - jnp.dot(fst_ref[...], yi, preferred_element_type=F32))

    @pl.when(k == nk - 1)
    def _():
        a = a_value()
        m = _short_conv(m_ref[...], mw_ref, mb_ref)
        y = m * (o_ref[...] + skip_ref[...] * a)
        if has_gate:
            y = y * _silu(g_ref[...])
        o_ref[...] = y


def _hyena_conv(a_src, a_blk, a_conv, m_src, m_blk, m_conv, g_src, g_blk, skip, tables, spectra,
                order, width, cb, tk):
    bsz, length, _ = m_src.shape
    fc, fs, fct, fst = tables
    hr, hi = spectra
    tk = min(tk, length)
    nc = width // cb
    conv_a = a_conv is not None
    has_gate = g_src is not None
    once = dict(pipeline_mode=pl.Buffered(1))
    inputs, specs = [], []

    def add(arr, spec):
        inputs.append(arr)
        specs.append(spec)

    add(a_src, pl.BlockSpec((None, length, cb), lambda b, c, k: (b, 0, a_blk + c), **once))
    if conv_a:
        add(a_conv[0], pl.BlockSpec((HY_CONV, cb), lambda b, c, k: (0, a_conv[2] + c)))
        add(a_conv[1], pl.BlockSpec((1, cb), lambda b, c, k: (0, a_conv[2] + c)))
    add(m_src, pl.BlockSpec((None, length, cb), lambda b, c, k: (b, 0, m_blk + c), **once))
    add(m_conv[0], pl.BlockSpec((HY_CONV, cb), lambda b, c, k: (0, m_conv[2] + c)))
    add(m_conv[1], pl.BlockSpec((1, cb), lambda b, c, k: (0, m_conv[2] + c)))
    if has_gate:
        add(g_src, pl.BlockSpec((None, length, cb), lambda b, c, k: (b, 0, g_blk + c), **once))
    add(skip, pl.BlockSpec((None, 1, cb), lambda b, c, k: (order, 0, c)))
    add(fc, pl.BlockSpec((tk, length), lambda b, c, k: (k, 0)))
    add(fs, pl.BlockSpec((tk, length), lambda b, c, k: (k, 0)))
    add(fct, pl.BlockSpec((length, tk), lambda b, c, k: (0, k)))
    add(fst, pl.BlockSpec((length, tk), lambda b, c, k: (0, k)))
    add(hr, pl.BlockSpec((tk, cb), lambda b, c, k: (k, order * nc + c)))
    add(hi, pl.BlockSpec((tk, cb), lambda b, c, k: (k, order * nc + c)))
    return pl.pallas_call(
        functools.partial(_hyena_conv_kernel, conv_a=conv_a, has_gate=has_gate),
        out_shape=jax.ShapeDtypeStruct((bsz, length, width), F32),
        grid=(bsz, nc, length // tk),
        in_specs=specs,
        out_specs=pl.BlockSpec((None, length, cb), lambda b, c, k: (b, 0, c)),
        scratch_shapes=[pltpu.VMEM((length, cb), BF16)],
        compiler_params=_cparams(("arbitrary", "arbitrary", "arbitrary")),
        name="hyena_long_conv_o%d" % order,
    )(*inputs)


def _outproj_kernel(x_ref, ya_ref, ga_ref, yb_ref, of_ref, ob_ref, gc_ref, hgn_ref, w_ref, mod_ref,
                    fg_ref, o_ref, *, final, n_heads):
    wa = ya_ref.shape[1]
    wb = yb_ref.shape[1]
    wc = of_ref.shape[1]
    dv = wc // n_heads
    ya = (ya_ref[...] * _silu(ga_ref[...])).astype(BF16)
    acc = jnp.dot(ya, w_ref[0:wa, :], preferred_element_type=F32)
    acc = acc + jnp.dot(yb_ref[...].astype(BF16), w_ref[wa:wa + wb, :], preferred_element_type=F32)
    oc = of_ref[...] + ob_ref[...]
    gate_c = _silu(gc_ref[...]) * hgn_ref[...]
    for h in range(n_heads):
        sl = slice(h * dv, (h + 1) * dv)
        och = oc[:, sl]
        ms = jnp.mean(och * och, axis=-1, keepdims=True)
        ych = (och * lax.rsqrt(ms + EPS) * gate_c[:, sl]).astype(BF16)
        acc = acc + jnp.dot(ych, w_ref[wa + wb + h * dv:wa + wb + (h + 1) * dv, :],
                            preferred_element_type=F32)
    xn = x_ref[...] + mod_ref[2:3, :] * acc
    if final:
        ms = jnp.mean(xn * xn, axis=-1, keepdims=True)
        xn = xn * lax.rsqrt(ms + EPS) * fg_ref[...]
    o_ref[...] = xn


def _out_projection(x, ya, proj, ga_blk, yb, o_dirs, gc_blk, hg_norm, w_out_bf16, mod, final_g, final,
                    tm):
    bsz, t, d = x.shape
    wa = ya.shape[2]
    wb = yb.shape[2]
    wc = o_dirs.shape[3]
    tm = min(tm, t)
    return pl.pallas_call(
        functools.partial(_outproj_kernel, final=final, n_heads=HG_HEADS),
        out_shape=jax.ShapeDtypeStruct((bsz, t, d), F32),
        grid=(bsz, t // tm),
        in_specs=[
            pl.BlockSpec((None, tm, d), lambda b, i: (b, i, 0)),
            pl.BlockSpec((None, tm, wa), lambda b, i: (b, i, 0)),
            pl.BlockSpec((None, tm, wa), lambda b, i: (b, i, ga_blk)),
            pl.BlockSpec((None, tm, wb), lambda b, i: (b, i, 0)),
            pl.BlockSpec((None, None, tm, wc), lambda b, i: (0, b, i, 0)),
            pl.BlockSpec((None, None, tm, wc), lambda b, i: (1, b, i, 0)),
            pl.BlockSpec((None, tm, wc), lambda b, i: (b, i, gc_blk)),
            pl.BlockSpec((1, wc), lambda b, i: (0, 0)),
            pl.BlockSpec((wa + wb + wc, d), lambda b, i: (0, 0)),
            pl.BlockSpec((None, 3, d), lambda b, i: (b, 0, 0)),
            pl.BlockSpec((1, d), lambda b, i: (0, 0)),
        ],
        out_specs=pl.BlockSpec((None, tm, d), lambda b, i: (b, i, 0)),
        compiler_params=_cparams(("arbitrary", "arbitrary")),
        name="gate_outproj_residual",
    )(x, ya, proj, yb, o_dirs, o_dirs, proj, hg_norm.reshape(1, wc), w_out_bf16, mod,
      final_g.reshape(1, d))


def _block_diag(w):
    two, nh, n, _ = w.shape
    eye = jnp.eye(nh, dtype=w.dtype)
    dense = jnp.einsum("dhij,hg->dhigj", w, eye).reshape(two, nh * n, nh * n)
    return dense.astype(BF16)


def _mixers(x, mod, norm_g, lp, init, grid_rows, with_output):
    bsz, t, d = x.shape
    wl = lp["lru_w"]
    wh = lp["hy_w"]
    wg = lp["hg_w"]
    proj = _in_projection(x, mod, norm_g, lp["w_in"], tm=1024, tn=2048)
    xa = proj[:, :, :wl]
    if grid_rows is not None:
        x_tm = xa.reshape(bsz, grid_rows, GRID_W, wl).transpose(2, 1, 0, 3).reshape(t, bsz, wl)
    else:
        x_tm = xa.transpose(1, 0, 2)
    h_dirs, h_fin = _rglru(x_tm, init[0], lp["lru_conv_w"], lp["lru_conv_b"], lp["wa_dense"],
                           lp["lru_ba"], lp["wx_dense"], lp["lru_bx"], lp["lru_lam"], tp=128)
    q_off = 2 * wl + 4 * wh
    o_dirs, s_fin = _hgrn2(proj, q_off // wg, q_off // wg + 1, q_off // wg + 3, lp["hg_lb"], init[1],
                           tT=512)
    states = (h_fin, s_fin)
    if not with_output:
        return None, states
    ya_tm = h_dirs[0] + h_dirs[1]
    if grid_rows is not None:
        ya = ya_tm.reshape(GRID_W, grid_rows, bsz, wl).transpose(2, 1, 0, 3).reshape(bsz, t, wl)
    else:
        ya = ya_tm.transpose(1, 0, 2)
    tables = lp["dft"][t]
    spectra = _hyena_spectra(t, tables[0], tables[1], lp["hy_w1"], lp["hy_b1"], lp["hy_w2"],
                             lp["hy_b2"], lp["hy_w3"], lp["hy_freq"], wh, tl=512, tk=256)
    cb = min(wh, 256)
    ub_blk = (2 * wl) // cb
    nb = wh // cb
    cw, cbias = lp["hy_conv_w"], lp["hy_conv_b"].reshape(1, -1)
    z = _hyena_conv(proj, ub_blk, (cw, cbias, 0), proj, ub_blk + nb, (cw, cbias, nb), None, 0,
                    lp["hy_skip"], tables, spectra, 0, wh, cb, tk=128)
    yb = _hyena_conv(z, 0, None, proj, ub_blk + 2 * nb, (cw, cbias, 2 * nb), proj, ub_blk + 3 * nb,
                     lp["hy_skip"], tables, spectra, 1, wh, cb, tk=128)
    return (ya, yb, o_dirs, proj), states


def kernel(x, c, ctx, c_ctx, norm_g, w_mod, b_mod, w_in, w_out, lru_conv_w, lru_conv_b, lru_wa, lru_ba,
           lru_wx, lru_bx, lru_lam, hy_conv_w, hy_conv_b, hy_w1, hy_b1, hy_w2, hy_b2, hy_w3, hy_freq,
           hy_skip, hg_lb_logits, hg_norm, final_g):
    bsz, seq, d = x.shape
    depth = w_mod.shape[0]
    ctx_len = ctx.shape[1]
    rows = seq // GRID_W
    wl = lru_lam.shape[-1]
    wh = hy_skip.shape[-1]
    wg = hg_norm.shape[-1]
    dk = wg // HG_HEADS

    p = jax.nn.softmax(hg_lb_logits.astype(F32), axis=0)
    lower_bounds = jnp.cumsum(p, axis=0) - p[0]

    n_rows = -(-(bsz + 1) // SUBLANES) * SUBLANES
    cvec = jnp.zeros((n_rows, d), F32).at[:bsz].set(c).at[bsz].set(c_ctx)
    mod_all = _modulation(cvec, w_mod, b_mod)

    dft = {length: _odd_dft_tables(length) for length in (seq, ctx_len)}

    zero_states = (jnp.zeros((2, bsz, wl), F32), jnp.zeros((2, bsz, HG_HEADS, dk, dk), F32))
    xc = ctx
    for layer in range(depth):
        last = layer == depth - 1
        mod = mod_all[layer, :bsz].reshape(bsz, 3, d)
        mod_c = jnp.broadcast_to(mod_all[layer, bsz].reshape(1, 3, d), (bsz, 3, d))
        lp = dict(
            w_in=w_in[layer].astype(BF16), lru_w=wl, hy_w=wh, hg_w=wg,
            lru_conv_w=lru_conv_w[layer], lru_conv_b=lru_conv_b[layer],
            wa_dense=_block_diag(lru_wa[layer]), wx_dense=_block_diag(lru_wx[layer]),
            lru_ba=lru_ba[layer], lru_bx=lru_bx[layer], lru_lam=lru_lam[layer],
            hy_conv_w=hy_conv_w[layer], hy_conv_b=hy_conv_b[layer], hy_w1=hy_w1[layer],
            hy_b1=hy_b1[layer], hy_w2=hy_w2[layer], hy_b2=hy_b2[layer], hy_w3=hy_w3[layer],
            hy_freq=hy_freq[layer], hy_skip=hy_skip[layer].reshape(HY_ORDER, 1, wh),
            hg_lb=lower_bounds[layer], dft=dft)
        w_out_l = w_out[layer].astype(BF16)
        ga_blk = 1
        gc_blk = (2 * wl + 4 * wh + 4 * wg) // wg
        pieces_c, ctx_states = _mixers(xc, mod_c, norm_g[layer], lp, zero_states, None, not last)
        pieces, _ = _mixers(x, mod, norm_g[layer], lp, ctx_states, rows, True)
        ya, yb, o_dirs, proj = pieces
        x = _out_projection(x, ya, proj, ga_blk, yb, o_dirs, gc_blk, hg_norm[layer], w_out_l, mod,
                            final_g, last, tm=512)
        if not last:
            ya, yb, o_dirs, proj = pieces_c
            xc = _out_projection(xc, ya, proj, ga_blk, yb, o_dirs, gc_blk, hg_norm[layer], w_out_l,
                                 mod_c, final_g, False, tm=512)
    return x
```

```python
import functools
import math

import numpy as np
import jax
import jax.numpy as jnp
from jax import lax
from jax.experimental import pallas as pl
from jax.experimental.pallas import tpu as pltpu

GRID_W = 64
LRU_HEADS = 8
LRU_CONV = 4
LRU_C = 8.0
HY_ORDER = 2
HY_CONV = 3
HY_BANDS = 16
HY_FAST_DECAY = 0.3
HY_SLOW_DECAY = 1.5
HY_TARGET = 1e-2
HG_HEADS = 8
HG_CHUNK = 64
HG_SAFE_SPAN = 60.0
EPS = 1e-6
TINY = 1e-12

LANES = 128
SUBLANES = 8
VMEM_LIMIT_BYTES = 56 * 1024 * 1024

F32 = jnp.float32
BF16 = jnp.bfloat16


def _cparams(semantics):
    return pltpu.CompilerParams(dimension_semantics=semantics, vmem_limit_bytes=VMEM_LIMIT_BYTES)


def _sigmoid(x):
    return 1.0 / (1.0 + jnp.exp(-x))


def _silu(x):
    return x * _sigmoid(x)


def _mod_kernel(c_ref, w_ref, b_ref, o_ref):
    s = _silu(c_ref[...])
    o_ref[...] = jnp.dot(s, w_ref[...], preferred_element_type=F32,
                         precision=lax.Precision.HIGHEST) + b_ref[...]


def _modulation(cvec, w_mod, b_mod):
    depth, d, d3 = w_mod.shape
    r = cvec.shape[0]
    return pl.pallas_call(
        _mod_kernel,
        out_shape=jax.ShapeDtypeStruct((depth, r, d3), F32),
        grid=(depth, d3 // d),
        in_specs=[
            pl.BlockSpec((r, d), lambda l, j: (0, 0)),
            pl.BlockSpec((None, d, d), lambda l, j: (l, 0, j)),
            pl.BlockSpec((None, 1, d), lambda l, j: (l, 0, j)),
        ],
        out_specs=pl.BlockSpec((None, r, d), lambda l, j: (l, 0, j)),
        compiler_params=_cparams(("arbitrary", "arbitrary")),
        name="adaln_modulation",
    )(cvec, w_mod, b_mod.reshape(depth, 1, d3))


def _inproj_kernel(x_ref, mod_ref, g_ref, w_ref, o_ref, h_s):
    @pl.when(pl.program_id(2) == 0)
    def _():
        x = x_ref[...]
        ms = jnp.mean(x * x, axis=-1, keepdims=True)
        y = x * lax.rsqrt(ms + EPS) * g_ref[...]
        h = y * (1.0 + mod_ref[1:2, :]) + mod_ref[0:1, :]
        h_s[...] = h.astype(BF16)

    o_ref[...] = jnp.dot(h_s[...], w_ref[...], preferred_element_type=F32)


def _in_projection(x, mod, gain, w_bf16, tm, tn):
    bsz, t, d = x.shape
    d_in = w_bf16.shape[1]
    tm = min(tm, t)
    return pl.pallas_call(
        _inproj_kernel,
        out_shape=jax.ShapeDtypeStruct((bsz, t, d_in), F32),
        grid=(bsz, t // tm, d_in // tn),
        in_specs=[
            pl.BlockSpec((None, tm, d), lambda b, i, j: (b, i, 0)),
            pl.BlockSpec((None, 3, d), lambda b, i, j: (b, 0, 0)),
            pl.BlockSpec((1, d), lambda b, i, j: (0, 0)),
            pl.BlockSpec((d, tn), lambda b, i, j: (0, j)),
        ],
        out_specs=pl.BlockSpec((None, tm, tn), lambda b, i, j: (b, i, j)),
        scratch_shapes=[pltpu.VMEM((tm, d), BF16)],
        compiler_params=_cparams(("arbitrary", "arbitrary", "arbitrary")),
        name="rmsnorm_adaln_inproj",
    )(x, mod, gain.reshape(1, d), w_bf16)


def _lru_kernel(xprev_ref, x_ref, xnext_ref, h0_ref, cw_ref, cb_ref, wa_ref, ba_ref, wx_ref,
                bx_ref, lam_ref, h_ref, hfin_ref, xpad, a_s, b_s, hstate):
    d = pl.program_id(0)
    i = pl.program_id(1)
    n = pl.num_programs(1)
    idx = jnp.where(d == 0, i, n - 1 - i)
    tp, bsz, ch = x_ref.shape

    @pl.when(i == 0)
    def _():
        hstate[...] = h0_ref[...]

    xpad[0:2] = jnp.where(idx > 0, xprev_ref[...], 0.0)
    xpad[2:2 + tp] = x_ref[...]
    xpad[2 + tp:3 + tp] = jnp.where(idx < n - 1, xnext_ref[...], 0.0)
    u = cb_ref[...] + cw_ref[0:1, :] * xpad[0:tp]
    for k in range(1, LRU_CONV):
        u = u + cw_ref[k:k + 1, :] * xpad[k:k + tp]
    u2 = u.reshape(tp * bsz, ch)
    ub = u2.astype(BF16)
    r = _sigmoid(jnp.dot(ub, wa_ref[...], preferred_element_type=F32) + ba_ref[...])
    gi = _sigmoid(jnp.dot(ub, wx_ref[...], preferred_element_type=F32) + bx_ref[...])
    nlam = -lam_ref[...]
    softplus = jnp.maximum(nlam, 0.0) + jnp.log(1.0 + jnp.exp(-jnp.abs(nlam)))
    log_a = (-LRU_C) * r * softplus
    a = jnp.exp(log_a)
    bb = jnp.sqrt(jnp.maximum(1.0 - a * a, TINY)) * (gi * u2)
    a_s[...] = a.reshape(tp, bsz, ch)
    b_s[...] = bb.reshape(tp, bsz, ch)

    def body(t, h):
        tt = jnp.where(d == 0, t, tp - 1 - t)
        h = a_s[tt] * h + b_s[tt]
        h_ref[tt] = h
        return h

    h = lax.fori_loop(0, tp, body, hstate[...], unroll=8)
    hstate[...] = h
    hfin_ref[...] = h


def _rglru(x_tm, h0, conv_w, conv_b, wa_dense, ba, wx_dense, bx, lam, tp):
    p, bsz, ch = x_tm.shape
    tp = min(tp, p)
    n = p // tp

    def tile(d, i):
        return jnp.where(d == 0, i, n - 1 - i)

    vec = lambda a: a.reshape(2, 1, ch)
    return pl.pallas_call(
        _lru_kernel,
        out_shape=(jax.ShapeDtypeStruct((2, p, bsz, ch), F32),
                   jax.ShapeDtypeStruct((2, bsz, ch), F32)),
        grid=(2, n),
        in_specs=[
            pl.BlockSpec((2, bsz, ch), lambda d, i: (jnp.maximum(tile(d, i) * (tp // 2) - 1, 0), 0, 0)),
            pl.BlockSpec((tp, bsz, ch), lambda d, i: (tile(d, i), 0, 0)),
            pl.BlockSpec((1, bsz, ch), lambda d, i: (jnp.minimum((tile(d, i) + 1) * tp, p - 1), 0, 0)),
            pl.BlockSpec((None, bsz, ch), lambda d, i: (d, 0, 0)),
            pl.BlockSpec((LRU_CONV, ch), lambda d, i: (0, 0)),
            pl.BlockSpec((1, ch), lambda d, i: (0, 0)),
            pl.BlockSpec((None, ch, ch), lambda d, i: (d, 0, 0)),
            pl.BlockSpec((None, 1, ch), lambda d, i: (d, 0, 0)),
            pl.BlockSpec((None, ch, ch), lambda d, i: (d, 0, 0)),
            pl.BlockSpec((None, 1, ch), lambda d, i: (d, 0, 0)),
            pl.BlockSpec((None, 1, ch), lambda d, i: (d, 0, 0)),
        ],
        out_specs=(
            pl.BlockSpec((None, tp, bsz, ch), lambda d, i: (d, tile(d, i), 0, 0)),
            pl.BlockSpec((None, bsz, ch), lambda d, i: (d, 0, 0)),
        ),
        scratch_shapes=[
            pltpu.VMEM((tp + LRU_CONV - 1, bsz, ch), F32),
            pltpu.VMEM((tp, bsz, ch), F32),
            pltpu.VMEM((tp, bsz, ch), F32),
            pltpu.VMEM((bsz, ch), F32),
        ],
        compiler_params=_cparams(("arbitrary", "arbitrary")),
        name="rglru_bidir_scan",
    )(x_tm, x_tm, x_tm, h0, conv_w, conv_b.reshape(1, ch), wa_dense, vec(ba), wx_dense, vec(bx),
      vec(lam))


def _hgrn2_kernel(q_ref, z_ref, v_ref, lb_ref, s0_ref, o_ref, sfin_ref, st_a, st_b):
    d = pl.program_id(0)
    i = pl.program_id(2)
    tT = q_ref.shape[0]
    nh, dv, dk = st_a.shape
    c = HG_CHUNK
    nch = tT // c

    @pl.when(i == 0)
    def _():
        st_a[...] = s0_ref[...]

    row = lax.broadcasted_iota(jnp.int32, (c, c), 0)
    col = lax.broadcasted_iota(jnp.int32, (c, c), 1)
    fwd = d == 0
    keep = (col - row) * jnp.where(fwd, 1, -1) <= 0
    tri = jnp.where(keep, 1.0, 0.0).astype(BF16)
    tri3 = jnp.concatenate([tri, tri, tri], axis=1)
    nt_dims = (((1,), (1,)), ((), ()))
    tn_dims = (((0,), (0,)), ((), ()))

    def gates(z, lb):
        e = jnp.exp(-jnp.abs(z))
        s_big = 1.0 / (1.0 + e)
        s_small = e * s_big
        sig = jnp.where(z >= 0, s_big, s_small)
        sig_neg = jnp.where(z >= 0, s_small, s_big)
        f = jnp.maximum(lb + (1.0 - lb) * sig, TINY)
        return f, (1.0 - lb) * sig_neg

    def chunk(j, src, dst):
        cj = jnp.where(fwd, j, nch - 1 - j)
        base = pl.multiple_of(cj * c, c)
        rows = pl.ds(base, c)
        span = jnp.zeros((1, dk), F32)
        heads = [(slice(h * dk, (h + 1) * dk), slice(h * dv, (h + 1) * dv)) for h in range(nh)]
        stage_a = []
        for sl, _ in heads:
            f, kk = gates(z_ref[rows, sl], lb_ref[:, sl])
            g = jnp.log(f)
            g1 = g.astype(BF16)
            r1 = g - g1.astype(F32)
            g2 = r1.astype(BF16)
            g3 = (r1 - g2.astype(F32)).astype(BF16)
            gc = jnp.dot(tri3, jnp.concatenate([g1, g2, g3], axis=0), preferred_element_type=F32)
            stage_a.append((kk, gc))
        stage_b = []
        for (sl, _), (kk, gc) in zip(heads, stage_a):
            ref_row = gc[c // 2:c // 2 + 1, :]
            tot = jnp.where(fwd, gc[c - 1:c, :], gc[0:1, :])
            span = jnp.maximum(span, jnp.maximum(jnp.abs(gc[0:1, :] - ref_row),
                                                 jnp.abs(gc[c - 1:c, :] - ref_row)))
            qp = q_ref[rows, sl] * jnp.exp(gc - ref_row)
            kp = kk * jnp.exp(ref_row - gc)
            qpp = (qp * jnp.exp(ref_row)).astype(BF16)
            kpp = (kp * jnp.exp(tot - ref_row)).astype(BF16)
            sc = lax.dot_general(qp.astype(BF16), kp.astype(BF16), nt_dims, preferred_element_type=F32)
            stage_b.append((sc, qpp, kpp, jnp.exp(tot)))
        for h, ((_, sv), (sc, qpp, kpp, dec)) in enumerate(zip(heads, stage_b)):
            vb = v_ref[rows, sv].astype(BF16)
            s_t = src[h]
            sc = jnp.where(keep, sc, 0.0).astype(BF16)
            o = jnp.dot(sc, vb, preferred_element_type=F32)
            o = o + lax.dot_general(qpp, s_t.astype(BF16), nt_dims, preferred_element_type=F32)
            o_ref[rows, sv] = o
            upd = lax.dot_general(vb, kpp, tn_dims, preferred_element_type=F32)
            dst[h] = s_t * dec + upd

        @pl.when(jnp.max(span) > HG_SAFE_SPAN)
        def _():
            sub = lax.broadcasted_iota(jnp.int32, (SUBLANES, 1), 0)
            for h in range(nh):
                dst[h] = src[h]

            def step(u, carry2):
                t = base + jnp.where(fwd, u, c - 1 - u)
                grp = pl.ds(pl.multiple_of((t // SUBLANES) * SUBLANES, SUBLANES), SUBLANES)
                live = sub == t % SUBLANES
                for h in range(nh):
                    sl = slice(h * dk, (h + 1) * dk)
                    sv = slice(h * dv, (h + 1) * dv)
                    f8, k8 = gates(z_ref[grp, sl], lb_ref[:, sl])
                    f = jnp.sum(jnp.where(live, f8, 0.0), axis=0, keepdims=True)
                    v8 = jnp.where(live, v_ref[grp, sv], 0.0)
                    s_new = dst[h] * f + lax.dot_general(v8, k8, tn_dims, preferred_element_type=F32)
                    dst[h] = s_new
                    o8 = lax.dot_general(q_ref[grp, sl], s_new, nt_dims, preferred_element_type=F32)
                    o_ref[grp, sv] = jnp.where(live, o8, o_ref[grp, sv])
                return carry2

            lax.fori_loop(0, c, step, 0)

    def chunk_pair(jj, carry):
        chunk(2 * jj, st_a, st_b)
        chunk(2 * jj + 1, st_b, st_a)
        return carry

    lax.fori_loop(0, nch // 2, chunk_pair, 0)
    sfin_ref[...] = st_a[...]


def _hgrn2(proj, q_blk, z_blk, v_blk, lb, s0, tT):
    bsz, t, _ = proj.shape
    _, _, nh, dv, dk = s0.shape
    w = nh * dk
    tT = min(tT, t)
    n = t // tT
    assert (tT // HG_CHUNK) % 2 == 0, "the ping-pong state buffers need an even chunk count per tile"

    def tile(d, i):
        return jnp.where(d == 0, i, n - 1 - i)

    return pl.pallas_call(
        _hgrn2_kernel,
        out_shape=(jax.ShapeDtypeStruct((2, bsz, t, nh * dv), F32),
                   jax.ShapeDtypeStruct((2, bsz, nh, dv, dk), F32)),
        grid=(2, bsz, n),
        in_specs=[
            pl.BlockSpec((None, tT, w), lambda d, b, i: (b, tile(d, i), q_blk)),
            pl.BlockSpec((None, tT, w), lambda d, b, i: (b, tile(d, i), z_blk + d)),
            pl.BlockSpec((None, tT, w), lambda d, b, i: (b, tile(d, i), v_blk)),
            pl.BlockSpec((None, 1, w), lambda d, b, i: (d, 0, 0)),
            pl.BlockSpec((None, None, nh, dv, dk), lambda d, b, i: (d, b, 0, 0, 0)),
        ],
        out_specs=(
            pl.BlockSpec((None, None, tT, nh * dv), lambda d, b, i: (d, b, tile(d, i), 0)),
            pl.BlockSpec((None, None, nh, dv, dk), lambda d, b, i: (d, b, 0, 0, 0)),
        ),
        scratch_shapes=[pltpu.VMEM((nh, dv, dk), F32), pltpu.VMEM((nh, dv, dk), F32)],
        compiler_params=_cparams(("arbitrary", "arbitrary", "arbitrary")),
        name="hgrn2_bidir_chunk_scan",
    )(proj, proj, proj, lb.reshape(2, 1, w), s0)


HY_MAX_RADIX = 8
HY_MIN_SUBLEN = 128
HY_CHUNK = 8


def _hyena_radix(length):
    return max(1, min(HY_MAX_RADIX, length // HY_MIN_SUBLEN))


def _hyena_tables(length):
    radix = _hyena_radix(length)
    n = length // radix
    k2 = 2 * jnp.arange(n, dtype=jnp.int32) + 1
    ph = (k2[:, None] * jnp.arange(n, dtype=jnp.int32)[None, :]) % (4 * n)
    psi = ph.astype(F32) * (2.0 * math.pi / (4 * n))
    fc, fs = jnp.cos(psi), jnp.sin(psi)
    fwd = jnp.concatenate([fc, fs], axis=0).astype(BF16)
    inv = jnp.concatenate([fc.T, -fs.T], axis=1).astype(BF16)
    phr = (jnp.arange(radix, dtype=jnp.int32)[:, None] * k2[None, :]) % (4 * length)
    phi = phr.astype(F32) * (2.0 * math.pi / (4 * length))
    rep = lambda a: jnp.broadcast_to(a[:, :, None], (radix, n, LANES))
    return fwd, inv, rep(jnp.cos(phi)), rep(jnp.sin(phi))


def _cmul_const(x, c, s):
    re, im = x
    if abs(s) < 1e-12:
        return (re, im) if c > 0 else (-re, -im)
    if abs(c) < 1e-12:
        return (-im, re) if s > 0 else (im, -re)
    return (re * c - im * s, re * s + im * c)


def _fft_list(xs, sign):
    r = len(xs)
    if r == 1:
        return xs
    ev = _fft_list(xs[0::2], sign)
    od = _fft_list(xs[1::2], sign)
    out = [None] * r
    for j in range(r // 2):
        ang = sign * 2.0 * math.pi * j / r
        t = _cmul_const(od[j], math.cos(ang), math.sin(ang))
        out[j] = (ev[j][0] + t[0], ev[j][1] + t[1])
        out[j + r // 2] = (ev[j][0] - t[0], ev[j][1] - t[1])
    return out


def _hy_sub_transforms(slab, fwd_ref, z_ref, radix, n):
    n_slabs = slab.shape[0]
    for r in range(radix):
        parts = [slab[s, pl.ds(r, n, stride=radix), :] if radix > 1 else slab[s]
                 for s in range(n_slabs)]
        xr = parts[0] if n_slabs == 1 else jnp.concatenate(parts, axis=1)
        z_ref[r] = jnp.dot(fwd_ref[...], xr.astype(BF16), preferred_element_type=F32)


def _hy_load_twiddles(twr_ref, twi_ref, rows_c, radix):
    return [(twr_ref[r, rows_c, :], twi_ref[r, rows_c, :]) for r in range(1, radix)]


def _hy_twiddled_inputs(z_ref, tw, rows_c, rows_s, lanes, radix):
    xs = []
    for r in range(radix):
        c = z_ref[r, rows_c, lanes]
        s = z_ref[r, rows_s, lanes]
        if r == 0:
            xs.append((c, -s))
        else:
            wr, wi = tw[r - 1]
            xs.append((c * wr - s * wi, -(c * wi + s * wr)))
    return xs


def _filter_kernel(emb_ref, w1_ref, b1_ref, w2_ref, b2_ref, w3_ref, fr_ref, dl_ref,
                   a_ref, b_ref, ss_ref, *, length):
    i = pl.program_id(0)
    tl = emb_ref.shape[0]
    cw = dl_ref.shape[1]
    fr = fr_ref[...]
    hi = lax.Precision.HIGHEST
    hdn = jnp.sin(fr * (jnp.dot(emb_ref[...], w1_ref[...], preferred_element_type=F32,
                                precision=hi) + b1_ref[...]))
    hdn = jnp.sin(fr * (jnp.dot(hdn, w2_ref[...], preferred_element_type=F32,
                                precision=hi) + b2_ref[...]))
    h = jnp.dot(hdn, w3_ref[...], preferred_element_type=F32, precision=hi)
    pos = (lax.broadcasted_iota(jnp.int32, (tl, 1), 0) + i * tl)
    t = pos.astype(F32) / float(length)
    win = jnp.exp(-t * dl_ref[...])
    not_first = pos > 0

    @pl.when(i == 0)
    def _():
        ss_ref[...] = jnp.zeros_like(ss_ref)

    for o in range(HY_ORDER):
        hpos = h[:, (2 * o) * cw:(2 * o + 1) * cw] * win
        hneg = jnp.where(not_first, h[:, (2 * o + 1) * cw:(2 * o + 2) * cw] * win, 0.0)
        a_ref[:, o * cw:(o + 1) * cw] = hpos + hneg
        b_ref[:, o * cw:(o + 1) * cw] = hneg - hpos
        ss_ref[:, o * cw:(o + 1) * cw] += jnp.sum(hpos * hpos + hneg * hneg, axis=0, keepdims=True)


def _spectrum_kernel(a_ref, b_ref, ss_ref, fwd_ref, twr_ref, twi_ref, hr_ref, hi_ref, slab, za, zb,
                     *, scale):
    radix, n, cb = hr_ref.shape
    for s in range(cb // LANES):
        slab[s] = a_ref[:, s * LANES:(s + 1) * LANES]
    _hy_sub_transforms(slab, fwd_ref, za, radix, n)
    for s in range(cb // LANES):
        slab[s] = b_ref[:, s * LANES:(s + 1) * LANES]
    _hy_sub_transforms(slab, fwd_ref, zb, radix, n)
    norm = lax.rsqrt(ss_ref[...] + TINY) * scale

    def step(i, carry):
        base = pl.multiple_of(i * HY_CHUNK, HY_CHUNK)
        rows_c = pl.ds(base, HY_CHUNK)
        rows_s = pl.ds(base + n, HY_CHUNK)
        tw = _hy_load_twiddles(twr_ref, twi_ref, rows_c, radix)
        for t in range(cb // LANES):
            lanes = slice(t * LANES, (t + 1) * LANES)
            fa = _fft_list(_hy_twiddled_inputs(za, tw, rows_c, rows_s, lanes, radix), -1)
            fb = _fft_list(_hy_twiddled_inputs(zb, tw, rows_c, rows_s, lanes, radix), -1)
            for j in range(radix):
                hr_ref[j, rows_c, lanes] = fa[j][0] * norm[:, lanes]
                hi_ref[j, rows_c, lanes] = -fb[j][1] * norm[:, lanes]
        return carry

    lax.fori_loop(0, n // HY_CHUNK, step, 0)


def _hyena_spectra(length, tables, w1, b1, w2, b2, w3, freq, width, tl):
    f32 = F32
    pos = jnp.arange(length, dtype=f32)
    t = pos / length
    bands = jnp.linspace(1e-4, HY_BANDS - 1, HY_BANDS, dtype=f32)
    ang = (2.0 * math.pi / length) * pos[:, None] * bands[None, :]
    emb = jnp.concatenate([t[:, None], jnp.cos(ang), jnp.sin(ang)], axis=-1)
    deltas = jnp.abs(jnp.linspace(math.log(HY_TARGET) / HY_SLOW_DECAY,
                                  math.log(HY_TARGET) / HY_FAST_DECAY, width, dtype=f32))
    n_emb = -(-emb.shape[1] // LANES) * LANES
    w1 = jnp.pad(w1.astype(f32), ((0, n_emb - emb.shape[1]), (0, 0)))
    emb = jnp.pad(emb, ((0, 0), (0, n_emb - emb.shape[1])))
    hid = w1.shape[1]
    cols = HY_ORDER * width
    tl = min(tl, length)
    full = lambda *s: pl.BlockSpec(s, lambda i: (0,) * len(s))
    a_un, b_un, ss = pl.pallas_call(
        functools.partial(_filter_kernel, length=length),
        out_shape=(jax.ShapeDtypeStruct((length, cols), f32),
                   jax.ShapeDtypeStruct((length, cols), f32),
                   jax.ShapeDtypeStruct((1, cols), f32)),
        grid=(length // tl,),
        in_specs=[
            pl.BlockSpec((tl, n_emb), lambda i: (i, 0)),
            full(n_emb, hid), full(1, hid), full(hid, hid), full(1, hid),
            full(hid, 2 * cols), full(1, hid), full(1, width),
        ],
        out_specs=(pl.BlockSpec((tl, cols), lambda i: (i, 0)),
                   pl.BlockSpec((tl, cols), lambda i: (i, 0)),
                   pl.BlockSpec((1, cols), lambda i: (0, 0))),
        compiler_params=_cparams(("arbitrary",)),
        name="hyena_filter_mlp",
    )(emb, w1, b1.reshape(1, hid), w2, b2.reshape(1, hid), w3, freq.reshape(1, hid),
      deltas.reshape(1, width))
    fwd, _, twr, twi = tables
    radix, n, _ = twr.shape
    cb = LANES
    return pl.pallas_call(
        functools.partial(_spectrum_kernel, scale=1.0 / length),
        out_shape=(jax.ShapeDtypeStruct((radix, n, cols), f32),
                   jax.ShapeDtypeStruct((radix, n, cols), f32)),
        grid=(cols // cb,),
        in_specs=[
            pl.BlockSpec((length, cb), lambda j: (0, j)),
            pl.BlockSpec((length, cb), lambda j: (0, j)),
            pl.BlockSpec((1, cb), lambda j: (0, j)),
            pl.BlockSpec((2 * n, n), lambda j: (0, 0)),
            pl.BlockSpec((radix, n, LANES), lambda j: (0, 0, 0)),
            pl.BlockSpec((radix, n, LANES), lambda j: (0, 0, 0)),
        ],
        out_specs=(pl.BlockSpec((radix, n, cb), lambda j: (0, 0, j)),
                   pl.BlockSpec((radix, n, cb), lambda j: (0, 0, j))),
        scratch_shapes=[pltpu.VMEM((cb // LANES, length, LANES), f32),
                        pltpu.VMEM((radix, 2 * n, cb), f32),
                        pltpu.VMEM((radix, 2 * n, cb), f32)],
        compiler_params=_cparams(("arbitrary",)),
        name="hyena_filter_spectrum",
    )(a_un, b_un, ss, fwd, twr, twi)


def _short_conv(x, w_ref, b_ref):
    length = x.shape[0]
    rows = lax.broadcasted_iota(jnp.int32, (length, 1), 0)
    prev = jnp.where(rows > 0, pltpu.roll(x, 1, axis=0), 0.0)
    nxt = jnp.where(rows < length - 1, pltpu.roll(x, length - 1, axis=0), 0.0)
    return w_ref[0:1, :] * prev + w_ref[1:2, :] * x + w_ref[2:3, :] * nxt + b_ref[...]


def _hyena_conv_kernel(*refs, conv_a, has_gate):
    it = iter(refs)
    a_ref = next(it)
    aw_ref = ab_ref = None
    if conv_a:
        aw_ref, ab_ref = next(it), next(it)
    m_ref, mw_ref, mb_ref = next(it), next(it), next(it)
    g_ref = next(it) if has_gate else None
    skip_ref, fwd_ref, inv_ref, twr_ref, twi_ref, hr_ref, hi_ref = (next(it) for _ in range(7))
    o_ref = next(it)
    slab, z_s = next(it), next(it)
    radix, n, cb = hr_ref.shape
    tiles = [slice(t * LANES, (t + 1) * LANES) for t in range(cb // LANES)]

    def a_value():
        a = a_ref[...]
        return _short_conv(a, aw_ref, ab_ref) if conv_a else a

    a = a_value()
    for s, lanes in enumerate(tiles):
        slab[s] = a[:, lanes]
    _hy_sub_transforms(slab, fwd_ref, z_s, radix, n)

    def step(i, carry):
        base = pl.multiple_of(i * HY_CHUNK, HY_CHUNK)
        rows_c = pl.ds(base, HY_CHUNK)
        rows_s = pl.ds(base + n, HY_CHUNK)
        tw = _hy_load_twiddles(twr_ref, twi_ref, rows_c, radix)
        for lanes in tiles:
            zf = _fft_list(_hy_twiddled_inputs(z_s, tw, rows_c, rows_s, lanes, radix), -1)
            ys = []
            for j in range(radix):
                hr = hr_ref[j, rows_c, lanes]
                hi = hi_ref[j, rows_c, lanes]
                zr, zi = zf[j]
                ys.append((zr * hr - zi * hi, zr * hi + zi * hr))
            vs = _fft_list(ys, +1)
            for r in range(radix):
                ur, ui = vs[r]
                if r > 0:
                    wr, wi = tw[r - 1]
                    ur, ui = wr * ur - wi * ui, wr * ui + wi * ur
                z_s[r, rows_c, lanes] = ur
                z_s[r, rows_s, lanes] = ui
        return carry

    lax.fori_loop(0, n // HY_CHUNK, step, 0)

    for r in range(radix):
        yr = jnp.dot(inv_ref[...], z_s[r].astype(BF16), preferred_element_type=F32)
        for s, lanes in enumerate(tiles):
            if radix > 1:
                slab[s, pl.ds(r, n, stride=radix), :] = yr[:, lanes]
            else:
                slab[s] = yr[:, lanes]

    a = a_value()
    m = _short_conv(m_ref[...], mw_ref, mb_ref)
    for s, lanes in enumerate(tiles):
        y = m[:, lanes] * (slab[s] + skip_ref[:, lanes] * a[:, lanes])
        if has_gate:
            y = y * _silu(g_ref[:, lanes])
        o_ref[:, lanes] = y


def _hyena_conv(a_src, a_blk, a_conv, m_src, m_blk, m_conv, g_src, g_blk, skip, tables, spectra,
                order, width, cb):
    bsz, length, _ = m_src.shape
    fwd, inv, twr, twi = tables
    hr, hi = spectra
    radix, n, _ = twr.shape
    nc = width // cb
    conv_a = a_conv is not None
    has_gate = g_src is not None
    once = dict(pipeline_mode=pl.Buffered(1))
    inputs, specs = [], []

    def add(arr, spec):
        inputs.append(arr)
        specs.append(spec)

    add(a_src, pl.BlockSpec((None, length, cb), lambda c, b: (b, 0, a_blk + c), **once))
    if conv_a:
        add(a_conv[0], pl.BlockSpec((HY_CONV, cb), lambda c, b: (0, a_conv[2] + c)))
        add(a_conv[1], pl.BlockSpec((1, cb), lambda c, b: (0, a_conv[2] + c)))
    add(m_src, pl.BlockSpec((None, length, cb), lambda c, b: (b, 0, m_blk + c), **once))
    add(m_conv[0], pl.BlockSpec((HY_CONV, cb), lambda c, b: (0, m_conv[2] + c)))
    add(m_conv[1], pl.BlockSpec((1, cb), lambda c, b: (0, m_conv[2] + c)))
    if has_gate:
        add(g_src, pl.BlockSpec((None, length, cb), lambda c, b: (b, 0, g_blk + c), **once))
    add(skip, pl.BlockSpec((None, 1, cb), lambda c, b: (order, 0, c)))
    add(fwd, pl.BlockSpec((2 * n, n), lambda c, b: (0, 0), **once))
    add(inv, pl.BlockSpec((n, 2 * n), lambda c, b: (0, 0), **once))
    add(twr, pl.BlockSpec((radix, n, LANES), lambda c, b: (0, 0, 0), **once))
    add(twi, pl.BlockSpec((radix, n, LANES), lambda c, b: (0, 0, 0), **once))
    add(hr, pl.BlockSpec((radix, n, cb), lambda c, b: (0, 0, order * nc + c), **once))
    add(hi, pl.BlockSpec((radix, n, cb), lambda c, b: (0, 0, order * nc + c), **once))
    return pl.pallas_call(
        functools.partial(_hyena_conv_kernel, conv_a=conv_a, has_gate=has_gate),
        out_shape=jax.ShapeDtypeStruct((bsz, length, width), F32),
        grid=(nc, bsz),
        in_specs=specs,
        out_specs=pl.BlockSpec((None, length, cb), lambda c, b: (b, 0, c)),
        scratch_shapes=[pltpu.VMEM((cb // LANES, length, LANES), F32),
                        pltpu.VMEM((radix, 2 * n, cb), F32)],
        compiler_params=_cparams(("arbitrary", "arbitrary")),
        name="hyena_long_conv_o%d" % order,
    )(*inputs)


def _outproj_kernel(x_ref, ya_ref, ga_ref, yb_ref, of_ref, ob_ref, gc_ref, hgn_ref, w_ref, mod_ref,
                    fg_ref, o_ref, *, final, n_heads):
    wa = ya_ref.shape[1]
    wb = yb_ref.shape[1]
    wc = of_ref.shape[1]
    dv = wc // n_heads
    ya = (ya_ref[...] * _silu(ga_ref[...])).astype(BF16)
    acc = jnp.dot(ya, w_ref[0:wa, :], preferred_element_type=F32)
    acc = acc + jnp.dot(yb_ref[...].astype(BF16), w_ref[wa:wa + wb, :], preferred_element_type=F32)
    oc = of_ref[...] + ob_ref[...]
    gate_c = _silu(gc_ref[...]) * hgn_ref[...]
    for h in range(n_heads):
        sl = slice(h * dv, (h + 1) * dv)
        och = oc[:, sl]
        ms = jnp.mean(och * och, axis=-1, keepdims=True)
        ych = (och * lax.rsqrt(ms + EPS) * gate_c[:, sl]).astype(BF16)
        acc = acc + jnp.dot(ych, w_ref[wa + wb + h * dv:wa + wb + (h + 1) * dv, :],
                            preferred_element_type=F32)
    xn = x_ref[...] + mod_ref[2:3, :] * acc
    if final:
        ms = jnp.mean(xn * xn, axis=-1, keepdims=True)
        xn = xn * lax.rsqrt(ms + EPS) * fg_ref[...]
    o_ref[...] = xn


def _out_projection(x, ya, proj, ga_blk, yb, o_dirs, gc_blk, hg_norm, w_out_bf16, mod, final_g, final,
                    tm):
    bsz, t, d = x.shape
    wa = ya.shape[2]
    wb = yb.shape[2]
    wc = o_dirs.shape[3]
    tm = min(tm, t)
    return pl.pallas_call(
        functools.partial(_outproj_kernel, final=final, n_heads=HG_HEADS),
        out_shape=jax.ShapeDtypeStruct((bsz, t, d), F32),
        grid=(bsz, t // tm),
        in_specs=[
            pl.BlockSpec((None, tm, d), lambda b, i: (b, i, 0)),
            pl.BlockSpec((None, tm, wa), lambda b, i: (b, i, 0)),
            pl.BlockSpec((None, tm, wa), lambda b, i: (b, i, ga_blk)),
            pl.BlockSpec((None, tm, wb), lambda b, i: (b, i, 0)),
            pl.BlockSpec((None, None, tm, wc), lambda b, i: (0, b, i, 0)),
            pl.BlockSpec((None, None, tm, wc), lambda b, i: (1, b, i, 0)),
            pl.BlockSpec((None, tm, wc), lambda b, i: (b, i, gc_blk)),
            pl.BlockSpec((1, wc), lambda b, i: (0, 0)),
            pl.BlockSpec((wa + wb + wc, d), lambda b, i: (0, 0)),
            pl.BlockSpec((None, 3, d), lambda b, i: (b, 0, 0)),
            pl.BlockSpec((1, d), lambda b, i: (0, 0)),
        ],
        out_specs=pl.BlockSpec((None, tm, d), lambda b, i: (b, i, 0)),
        compiler_params=_cparams(("arbitrary", "arbitrary")),
        name="gate_outproj_residual",
    )(x, ya, proj, yb, o_dirs, o_dirs, proj, hg_norm.reshape(1, wc), w_out_bf16, mod,
      final_g.reshape(1, d))


def _block_diag(w):
    two, nh, n, _ = w.shape
    eye = jnp.eye(nh, dtype=w.dtype)
    dense = jnp.einsum("dhij,hg->dhigj", w, eye).reshape(two, nh * n, nh * n)
    return dense.astype(BF16)


def _mixers(x, mod, norm_g, lp, init, grid_rows, with_output):
    bsz, t, d = x.shape
    wl = lp["lru_w"]
    wh = lp["hy_w"]
    wg = lp["hg_w"]
    proj = _in_projection(x, mod, norm_g, lp["w_in"], tm=1024, tn=2048)
    xa = proj[:, :, :wl]
    if grid_rows is not None:
        x_tm = xa.reshape(bsz, grid_rows, GRID_W, wl).transpose(2, 1, 0, 3).reshape(t, bsz, wl)
    else:
        x_tm = xa.transpose(1, 0, 2)
    h_dirs, h_fin = _rglru(x_tm, init[0], lp["lru_conv_w"], lp["lru_conv_b"], lp["wa_dense"],
                           lp["lru_ba"], lp["wx_dense"], lp["lru_bx"], lp["lru_lam"], tp=128)
    q_off = 2 * wl + 4 * wh
    o_dirs, s_fin = _hgrn2(proj, q_off // wg, q_off // wg + 1, q_off // wg + 3, lp["hg_lb"], init[1],
                           tT=512)
    states = (h_fin, s_fin)
    if not with_output:
        return None, states
    ya_tm = h_dirs[0] + h_dirs[1]
    if grid_rows is not None:
        ya = ya_tm.reshape(GRID_W, grid_rows, bsz, wl).transpose(2, 1, 0, 3).reshape(bsz, t, wl)
    else:
        ya = ya_tm.transpose(1, 0, 2)
    tables = lp["dft"][t]
    spectra = _hyena_spectra(t, tables, lp["hy_w1"], lp["hy_b1"], lp["hy_w2"],
                             lp["hy_b2"], lp["hy_w3"], lp["hy_freq"], wh, tl=512)
    cb = min(wh, 256)
    ub_blk = (2 * wl) // cb
    nb = wh // cb
    cw, cbias = lp["hy_conv_w"], lp["hy_conv_b"].reshape(1, -1)
    z = _hyena_conv(proj, ub_blk, (cw, cbias, 0), proj, ub_blk + nb, (cw, cbias, nb), None, 0,
                    lp["hy_skip"], tables, spectra, 0, wh, cb)
    yb = _hyena_conv(z, 0, None, proj, ub_blk + 2 * nb, (cw, cbias, 2 * nb), proj, ub_blk + 3 * nb,
                     lp["hy_skip"], tables, spectra, 1, wh, cb)
    return (ya, yb, o_dirs, proj), states


def kernel(x, c, ctx, c_ctx, norm_g, w_mod, b_mod, w_in, w_out, lru_conv_w, lru_conv_b, lru_wa, lru_ba,
           lru_wx, lru_bx, lru_lam, hy_conv_w, hy_conv_b, hy_w1, hy_b1, hy_w2, hy_b2, hy_w3, hy_freq,
           hy_skip, hg_lb_logits, hg_norm, final_g):
    bsz, seq, d = x.shape
    depth = w_mod.shape[0]
    ctx_len = ctx.shape[1]
    rows = seq // GRID_W
    wl = lru_lam.shape[-1]
    wh = hy_skip.shape[-1]
    wg = hg_norm.shape[-1]
    dk = wg // HG_HEADS

    p = jax.nn.softmax(hg_lb_logits.astype(F32), axis=0)
    lower_bounds = jnp.cumsum(p, axis=0) - p[0]

    n_rows = -(-(bsz + 1) // SUBLANES) * SUBLANES
    cvec = jnp.zeros((n_rows, d), F32).at[:bsz].set(c).at[bsz].set(c_ctx)
    mod_all = _modulation(cvec, w_mod, b_mod)

    dft = {length: _hyena_tables(length) for length in (seq, ctx_len)}

    zero_states = (jnp.zeros((2, bsz, wl), F32), jnp.zeros((2, bsz, HG_HEADS, dk, dk), F32))
    xc = ctx
    for layer in range(depth):
        last = layer == depth - 1
        mod = mod_all[layer, :bsz].reshape(bsz, 3, d)
        mod_c = jnp.broadcast_to(mod_all[layer, bsz].reshape(1, 3, d), (bsz, 3, d))
        lp = dict(
            w_in=w_in[layer].astype(BF16), lru_w=wl, hy_w=wh, hg_w=wg,
            lru_conv_w=lru_conv_w[layer], lru_conv_b=lru_conv_b[layer],
            wa_dense=_block_diag(lru_wa[layer]), wx_dense=_block_diag(lru_wx[layer]),
            lru_ba=lru_ba[layer], lru_bx=lru_bx[layer], lru_lam=lru_lam[layer],
            hy_conv_w=hy_conv_w[layer], hy_conv_b=hy_conv_b[layer], hy_w1=hy_w1[layer],
            hy_b1=hy_b1[layer], hy_w2=hy_w2[layer], hy_b2=hy_b2[layer], hy_w3=hy_w3[layer],
            hy_freq=hy_freq[layer], hy_skip=hy_skip[layer].reshape(HY_ORDER, 1, wh),
            hg_lb=lower_bounds[layer], dft=dft)
        w_out_l = w_out[layer].astype(BF16)
        ga_blk = 1
        gc_blk = (2 * wl + 4 * wh + 4 * wg) // wg
        pieces_c, ctx_states = _mixers(xc, mod_c, norm_g[layer], lp, zero_states, None, not last)
        pieces, _ = _mixers(x, mod, norm_g[layer], lp, ctx_states, rows, True)
        ya, yb, o_dirs, proj = pieces
        x = _out_projection(x, ya, proj, ga_blk, yb, o_dirs, gc_blk, hg_norm[layer], w_out_l, mod,
                            final_g, last, tm=512)
        if not last:
            ya, yb, o_dirs, proj = pieces_c
            xc = _out_projection(xc, ya, proj, ga_blk, yb, o_dirs, gc_blk, hg_norm[layer], w_out_l,
                                 mod_c, final_g, False, tm=512)
    return x
```

```python
import functools
import math

import numpy as np
import jax
import jax.numpy as jnp
from jax import lax
from jax.experimental import pallas as pl
from jax.experimental.pallas import tpu as pltpu

GRID_W = 64
LRU_HEADS = 8
LRU_CONV = 4
LRU_C = 8.0
HY_ORDER = 2
HY_CONV = 3
HY_BANDS = 16
HY_FAST_DECAY = 0.3
HY_SLOW_DECAY = 1.5
HY_TARGET = 1e-2
HG_HEADS = 8
HG_CHUNK = 64
HG_SAFE_SPAN = 60.0
EPS = 1e-6
TINY = 1e-12

LANES = 128
SUBLANES = 8
VMEM_LIMIT_BYTES = 56 * 1024 * 1024

F32 = jnp.float32
BF16 = jnp.bfloat16


def _cparams(semantics):
    return pltpu.CompilerParams(dimension_semantics=semantics, vmem_limit_bytes=VMEM_LIMIT_BYTES)


def _sigmoid(x):
    return 1.0 / (1.0 + jnp.exp(-x))


def _silu(x):
    return x * _sigmoid(x)


def _mod_kernel(c_ref, w_ref, b_ref, o_ref):
    s = _silu(c_ref[...])
    o_ref[...] = jnp.dot(s, w_ref[...], preferred_element_type=F32,
                         precision=lax.Precision.HIGHEST) + b_ref[...]


def _modulation(cvec, w_mod, b_mod):
    depth, d, d3 = w_mod.shape
    r = cvec.shape[0]
    return pl.pallas_call(
        _mod_kernel,
        out_shape=jax.ShapeDtypeStruct((depth, r, d3), F32),
        grid=(depth, d3 // d),
        in_specs=[
            pl.BlockSpec((r, d), lambda l, j: (0, 0)),
            pl.BlockSpec((None, d, d), lambda l, j: (l, 0, j)),
            pl.BlockSpec((None, 1, d), lambda l, j: (l, 0, j)),
        ],
        out_specs=pl.BlockSpec((None, r, d), lambda l, j: (l, 0, j)),
        compiler_params=_cparams(("arbitrary", "arbitrary")),
        name="adaln_modulation",
    )(cvec, w_mod, b_mod.reshape(depth, 1, d3))


def _inproj_kernel(x_ref, mod_ref, g_ref, w_ref, o_ref, h_s):
    @pl.when(pl.program_id(2) == 0)
    def _():
        x = x_ref[...]
        ms = jnp.mean(x * x, axis=-1, keepdims=True)
        y = x * lax.rsqrt(ms + EPS) * g_ref[...]
        h = y * (1.0 + mod_ref[1:2, :]) + mod_ref[0:1, :]
        h_s[...] = h.astype(BF16)

    o_ref[...] = jnp.dot(h_s[...], w_ref[...], preferred_element_type=F32)


def _in_projection(x, mod, gain, w_bf16, tm, tn):
    bsz, t, d = x.shape
    d_in = w_bf16.shape[1]
    tm = min(tm, t)
    return pl.pallas_call(
        _inproj_kernel,
        out_shape=jax.ShapeDtypeStruct((bsz, t, d_in), F32),
        grid=(bsz, t // tm, d_in // tn),
        in_specs=[
            pl.BlockSpec((None, tm, d), lambda b, i, j: (b, i, 0)),
            pl.BlockSpec((None, 3, d), lambda b, i, j: (b, 0, 0)),
            pl.BlockSpec((1, d), lambda b, i, j: (0, 0)),
            pl.BlockSpec((d, tn), lambda b, i, j: (0, j)),
        ],
        out_specs=pl.BlockSpec((None, tm, tn), lambda b, i, j: (b, i, j)),
        scratch_shapes=[pltpu.VMEM((tm, d), BF16)],
        compiler_params=_cparams(("arbitrary", "arbitrary", "arbitrary")),
        name="rmsnorm_adaln_inproj",
    )(x, mod, gain.reshape(1, d), w_bf16)


def _lru_kernel(xprev_ref, x_ref, xnext_ref, h0_ref, cw_ref, cb_ref, wa_ref, ba_ref, wx_ref,
                bx_ref, lam_ref, h_ref, hfin_ref, xpad, a_s, b_s, hstate):
    d = pl.program_id(0)
    i = pl.program_id(1)
    n = pl.num_programs(1)
    idx = jnp.where(d == 0, i, n - 1 - i)
    tp, bsz, ch = x_ref.shape

    @pl.when(i == 0)
    def _():
        hstate[...] = h0_ref[...]

    xpad[0:2] = jnp.where(idx > 0, xprev_ref[...], 0.0)
    xpad[2:2 + tp] = x_ref[...]
    xpad[2 + tp:3 + tp] = jnp.where(idx < n - 1, xnext_ref[...], 0.0)
    u = cb_ref[...] + cw_ref[0:1, :] * xpad[0:tp]
    for k in range(1, LRU_CONV):
        u = u + cw_ref[k:k + 1, :] * xpad[k:k + tp]
    u2 = u.reshape(tp * bsz, ch)
    ub = u2.astype(BF16)
    r = _sigmoid(jnp.dot(ub, wa_ref[...], preferred_element_type=F32) + ba_ref[...])
    gi = _sigmoid(jnp.dot(ub, wx_ref[...], preferred_element_type=F32) + bx_ref[...])
    nlam = -lam_ref[...]
    softplus = jnp.maximum(nlam, 0.0) + jnp.log(1.0 + jnp.exp(-jnp.abs(nlam)))
    log_a = (-LRU_C) * r * softplus
    a = jnp.exp(log_a)
    bb = jnp.sqrt(jnp.maximum(1.0 - a * a, TINY)) * (gi * u2)
    a_s[...] = a.reshape(tp, bsz, ch)
    b_s[...] = bb.reshape(tp, bsz, ch)

    def body(t, h):
        tt = jnp.where(d == 0, t, tp - 1 - t)
        h = a_s[tt] * h + b_s[tt]
        h_ref[tt] = h
        return h

    h = lax.fori_loop(0, tp, body, hstate[...], unroll=8)
    hstate[...] = h
    hfin_ref[...] = h


def _rglru(x_tm, h0, conv_w, conv_b, wa_dense, ba, wx_dense, bx, lam, tp):
    p, bsz, ch = x_tm.shape
    tp = min(tp, p)
    n = p // tp

    def tile(d, i):
        return jnp.where(d == 0, i, n - 1 - i)

    vec = lambda a: a.reshape(2, 1, ch)
    return pl.pallas_call(
        _lru_kernel,
        out_shape=(jax.ShapeDtypeStruct((2, p, bsz, ch), F32),
                   jax.ShapeDtypeStruct((2, bsz, ch), F32)),
        grid=(2, n),
        in_specs=[
            pl.BlockSpec((2, bsz, ch), lambda d, i: (jnp.maximum(tile(d, i) * (tp // 2) - 1, 0), 0, 0)),
            pl.BlockSpec((tp, bsz, ch), lambda d, i: (tile(d, i), 0, 0)),
            pl.BlockSpec((1, bsz, ch), lambda d, i: (jnp.minimum((tile(d, i) + 1) * tp, p - 1), 0, 0)),
            pl.BlockSpec((None, bsz, ch), lambda d, i: (d, 0, 0)),
            pl.BlockSpec((LRU_CONV, ch), lambda d, i: (0, 0)),
            pl.BlockSpec((1, ch), lambda d, i: (0, 0)),
            pl.BlockSpec((None, ch, ch), lambda d, i: (d, 0, 0)),
            pl.BlockSpec((None, 1, ch), lambda d, i: (d, 0, 0)),
            pl.BlockSpec((None, ch, ch), lambda d, i: (d, 0, 0)),
            pl.BlockSpec((None, 1, ch), lambda d, i: (d, 0, 0)),
            pl.BlockSpec((None, 1, ch), lambda d, i: (d, 0, 0)),
        ],
        out_specs=(
            pl.BlockSpec((None, tp, bsz, ch), lambda d, i: (d, tile(d, i), 0, 0)),
            pl.BlockSpec((None, bsz, ch), lambda d, i: (d, 0, 0)),
        ),
        scratch_shapes=[
            pltpu.VMEM((tp + LRU_CONV - 1, bsz, ch), F32),
            pltpu.VMEM((tp, bsz, ch), F32),
            pltpu.VMEM((tp, bsz, ch), F32),
            pltpu.VMEM((bsz, ch), F32),
        ],
        compiler_params=_cparams(("arbitrary", "arbitrary")),
        name="rglru_bidir_scan",
    )(x_tm, x_tm, x_tm, h0, conv_w, conv_b.reshape(1, ch), wa_dense, vec(ba), wx_dense, vec(bx),
      vec(lam))


def _hgrn2_kernel(q_ref, z_ref, v_ref, lb_ref, s0_ref, o_ref, sfin_ref, st_a, st_b, st_c):
    d = pl.program_id(0)
    i = pl.program_id(2)
    tT = q_ref.shape[0]
    nh, dv, dk = st_a.shape
    c = HG_CHUNK
    nch = tT // c

    @pl.when(i == 0)
    def _():
        st_a[...] = s0_ref[...]

    row = lax.broadcasted_iota(jnp.int32, (c, c), 0)
    col = lax.broadcasted_iota(jnp.int32, (c, c), 1)
    fwd = d == 0
    keep = (col - row) * jnp.where(fwd, 1, -1) <= 0
    tri = jnp.where(keep, 1.0, 0.0).astype(BF16)
    tri3 = jnp.concatenate([tri, tri, tri], axis=1)
    nt_dims = (((1,), (1,)), ((), ()))
    tn_dims = (((0,), (0,)), ((), ()))

    def gates(z, lb):
        e = jnp.exp(-jnp.abs(z))
        s_big = 1.0 / (1.0 + e)
        sig = jnp.where(z >= 0, s_big, e * s_big)
        f = lb + (1.0 - lb) * sig
        return jnp.maximum(f, TINY), 1.0 - f

    heads = [(slice(h * dk, (h + 1) * dk), slice(h * dv, (h + 1) * dv)) for h in range(nh)]

    def chunk_base(j):
        return pl.multiple_of(jnp.where(fwd, j, nch - 1 - j) * c, c)

    def stage_a(rows):
        out = []
        for sl, _ in heads:
            f, kk = gates(z_ref[rows, sl], lb_ref[:, sl])
            g = jnp.log(f)
            g1 = g.astype(BF16)
            r1 = g - g1.astype(F32)
            g2 = r1.astype(BF16)
            g3 = (r1 - g2.astype(F32)).astype(BF16)
            gc = jnp.dot(tri3, jnp.concatenate([g1, g2, g3], axis=0), preferred_element_type=F32)
            out.append((kk, gc))
        return out

    def stage_b(rows, from_a):
        out = []
        span = jnp.zeros((1, dk), F32)
        for (sl, _), (kk, gc) in zip(heads, from_a):
            ref_row = gc[c // 2:c // 2 + 1, :]
            tot = jnp.where(fwd, gc[c - 1:c, :], gc[0:1, :])
            span = jnp.maximum(span, jnp.maximum(jnp.abs(gc[0:1, :] - ref_row),
                                                 jnp.abs(gc[c - 1:c, :] - ref_row)))
            e_q = jnp.exp(gc - ref_row)
            qp = q_ref[rows, sl] * e_q
            kp = kk * (1.0 / e_q)
            qpp = (qp * jnp.exp(ref_row)).astype(BF16)
            kpp = (kp * jnp.exp(tot - ref_row)).astype(BF16)
            sc = lax.dot_general(qp.astype(BF16), kp.astype(BF16), nt_dims, preferred_element_type=F32)
            out.append((sc, qpp, kpp, jnp.exp(tot)))
        return out, span

    def stage_c(rows, from_b, src, dst):
        for h, ((_, sv), (sc, qpp, kpp, dec)) in enumerate(zip(heads, from_b)):
            vb = v_ref[rows, sv].astype(BF16)
            s_t = src[h]
            sc = jnp.where(keep, sc, 0.0).astype(BF16)
            o = jnp.dot(sc, vb, preferred_element_type=F32)
            o = o + lax.dot_general(qpp, s_t.astype(BF16), nt_dims, preferred_element_type=F32)
            o_ref[rows, sv] = o
            upd = lax.dot_general(vb, kpp, tn_dims, preferred_element_type=F32)
            dst[h] = s_t * dec + upd

    def exact_chunk(base, src, dst):
        sub = lax.broadcasted_iota(jnp.int32, (SUBLANES, 1), 0)
        dst[...] = src[...]

        def step(u, carry):
            t = base + jnp.where(fwd, u, c - 1 - u)
            grp = pl.ds(pl.multiple_of((t // SUBLANES) * SUBLANES, SUBLANES), SUBLANES)
            live = sub == t % SUBLANES
            for h, (sl, sv) in enumerate(heads):
                f8, k8 = gates(z_ref[grp, sl], lb_ref[:, sl])
                f = jnp.sum(jnp.where(live, f8, 0.0), axis=0, keepdims=True)
                v8 = jnp.where(live, v_ref[grp, sv], 0.0)
                s_new = dst[h] * f + lax.dot_general(v8, k8, tn_dims, preferred_element_type=F32)
                dst[h] = s_new
                o8 = lax.dot_general(q_ref[grp, sl], s_new, nt_dims, preferred_element_type=F32)
                o_ref[grp, sv] = jnp.where(live, o8, o_ref[grp, sv])
            return carry

        lax.fori_loop(0, c, step, 0)

    def chunk_pair(jj, carry):
        base0, base1 = chunk_base(2 * jj), chunk_base(2 * jj + 1)
        rows0, rows1 = pl.ds(base0, c), pl.ds(base1, c)
        b0, span0 = stage_b(rows0, stage_a(rows0))
        a1 = stage_a(rows1)
        stage_c(rows0, b0, st_a, st_b)
        b1, span1 = stage_b(rows1, a1)
        stage_c(rows1, b1, st_b, st_c)

        @pl.when(jnp.max(jnp.maximum(span0, span1)) > HG_SAFE_SPAN)
        def _():
            exact_chunk(base0, st_a, st_b)
            exact_chunk(base1, st_b, st_c)

        st_a[...] = st_c[...]
        return carry

    lax.fori_loop(0, nch // 2, chunk_pair, 0)
    sfin_ref[...] = st_a[...]


def _hgrn2(proj, q_blk, z_blk, v_blk, lb, s0, tT):
    bsz, t, _ = proj.shape
    _, _, nh, dv, dk = s0.shape
    w = nh * dk
    tT = min(tT, t)
    n = t // tT
    assert (tT // HG_CHUNK) % 2 == 0, "chunks are processed in pairs"

    def tile(d, i):
        return jnp.where(d == 0, i, n - 1 - i)

    return pl.pallas_call(
        _hgrn2_kernel,
        out_shape=(jax.ShapeDtypeStruct((2, bsz, t, nh * dv), F32),
                   jax.ShapeDtypeStruct((2, bsz, nh, dv, dk), F32)),
        grid=(2, bsz, n),
        in_specs=[
            pl.BlockSpec((None, tT, w), lambda d, b, i: (b, tile(d, i), q_blk)),
            pl.BlockSpec((None, tT, w), lambda d, b, i: (b, tile(d, i), z_blk + d)),
            pl.BlockSpec((None, tT, w), lambda d, b, i: (b, tile(d, i), v_blk)),
            pl.BlockSpec((None, 1, w), lambda d, b, i: (d, 0, 0)),
            pl.BlockSpec((None, None, nh, dv, dk), lambda d, b, i: (d, b, 0, 0, 0)),
        ],
        out_specs=(
            pl.BlockSpec((None, None, tT, nh * dv), lambda d, b, i: (d, b, tile(d, i), 0)),
            pl.BlockSpec((None, None, nh, dv, dk), lambda d, b, i: (d, b, 0, 0, 0)),
        ),
        scratch_shapes=[pltpu.VMEM((nh, dv, dk), F32)] * 3,
        compiler_params=_cparams(("arbitrary", "arbitrary", "arbitrary")),
        name="hgrn2_bidir_chunk_scan",
    )(proj, proj, proj, lb.reshape(2, 1, w), s0)


HY_MAX_RADIX = 8
HY_MIN_SUBLEN = 128
HY_CHUNK = 8


def _hyena_radix(length):
    return max(1, min(HY_MAX_RADIX, length // HY_MIN_SUBLEN))


def _hyena_tables(length):
    radix = _hyena_radix(length)
    n = length // radix
    k2 = 2 * jnp.arange(n, dtype=jnp.int32) + 1
    ph = (k2[:, None] * jnp.arange(n, dtype=jnp.int32)[None, :]) % (4 * n)
    psi = ph.astype(F32) * (2.0 * math.pi / (4 * n))
    fc, fs = jnp.cos(psi), jnp.sin(psi)
    fwd = jnp.concatenate([fc, fs], axis=0).astype(BF16)
    inv = jnp.concatenate([fc.T, -fs.T], axis=1).astype(BF16)
    phr = (jnp.arange(radix, dtype=jnp.int32)[:, None] * k2[None, :]) % (4 * length)
    phi = phr.astype(F32) * (2.0 * math.pi / (4 * length))
    rep = lambda a: jnp.broadcast_to(a[:, :, None], (radix, n, LANES))
    return fwd, inv, rep(jnp.cos(phi)), rep(jnp.sin(phi))


def _cmul_const(x, c, s):
    re, im = x
    if abs(s) < 1e-12:
        return (re, im) if c > 0 else (-re, -im)
    if abs(c) < 1e-12:
        return (-im, re) if s > 0 else (im, -re)
    return (re * c - im * s, re * s + im * c)


def _fft_list(xs, sign):
    r = len(xs)
    if r == 1:
        return xs
    ev = _fft_list(xs[0::2], sign)
    od = _fft_list(xs[1::2], sign)
    out = [None] * r
    for j in range(r // 2):
        ang = sign * 2.0 * math.pi * j / r
        t = _cmul_const(od[j], math.cos(ang), math.sin(ang))
        out[j] = (ev[j][0] + t[0], ev[j][1] + t[1])
        out[j + r // 2] = (ev[j][0] - t[0], ev[j][1] - t[1])
    return out


def _hy_sub_transforms(slab, fwd_ref, z_ref, radix, n):
    n_slabs = slab.shape[0]
    for r in range(radix):
        parts = [slab[s, pl.ds(r, n, stride=radix), :] if radix > 1 else slab[s]
                 for s in range(n_slabs)]
        xr = parts[0] if n_slabs == 1 else jnp.concatenate(parts, axis=1)
        z_ref[r] = jnp.dot(fwd_ref[...], xr.astype(BF16), preferred_element_type=F32)


def _hy_load_twiddles(twr_ref, twi_ref, rows_c, radix):
    return [(twr_ref[r, rows_c, :], twi_ref[r, rows_c, :]) for r in range(1, radix)]


def _hy_twiddled_inputs(z_ref, tw, rows_c, rows_s, lanes, radix):
    xs = []
    for r in range(radix):
        c = z_ref[r, rows_c, lanes]
        s = z_ref[r, rows_s, lanes]
        if r == 0:
            xs.append((c, -s))
        else:
            wr, wi = tw[r - 1]
            xs.append((c * wr - s * wi, -(c * wi + s * wr)))
    return xs


def _filter_kernel(emb_ref, w1_ref, b1_ref, w2_ref, b2_ref, w3_ref, fr_ref, dl_ref,
                   a_ref, b_ref, ss_ref, *, length):
    i = pl.program_id(0)
    tl = emb_ref.shape[0]
    cw = dl_ref.shape[1]
    fr = fr_ref[...]
    hi = lax.Precision.HIGHEST
    hdn = jnp.sin(fr * (jnp.dot(emb_ref[...], w1_ref[...], preferred_element_type=F32,
                                precision=hi) + b1_ref[...]))
    hdn = jnp.sin(fr * (jnp.dot(hdn, w2_ref[...], preferred_element_type=F32,
                                precision=hi) + b2_ref[...]))
    h = jnp.dot(hdn.astype(BF16), w3_ref[...].astype(BF16), preferred_element_type=F32)
    pos = (lax.broadcasted_iota(jnp.int32, (tl, 1), 0) + i * tl)
    t = pos.astype(F32) / float(length)
    win = jnp.exp(-t * dl_ref[...])
    not_first = pos > 0

    @pl.when(i == 0)
    def _():
        ss_ref[...] = jnp.zeros_like(ss_ref)

    for o in range(HY_ORDER):
        hpos = h[:, (2 * o) * cw:(2 * o + 1) * cw] * win
        hneg = jnp.where(not_first, h[:, (2 * o + 1) * cw:(2 * o + 2) * cw] * win, 0.0)
        a_ref[:, o * cw:(o + 1) * cw] = hpos + hneg
        b_ref[:, o * cw:(o + 1) * cw] = hneg - hpos
        ss_ref[:, o * cw:(o + 1) * cw] += jnp.sum(hpos * hpos + hneg * hneg, axis=0, keepdims=True)


def _spectrum_kernel(a_ref, b_ref, ss_ref, fwd_ref, twr_ref, twi_ref, hr_ref, hi_ref, slab, za, zb,
                     *, scale):
    radix, n, cb = hr_ref.shape
    for s in range(cb // LANES):
        slab[s] = a_ref[:, s * LANES:(s + 1) * LANES]
    _hy_sub_transforms(slab, fwd_ref, za, radix, n)
    for s in range(cb // LANES):
        slab[s] = b_ref[:, s * LANES:(s + 1) * LANES]
    _hy_sub_transforms(slab, fwd_ref, zb, radix, n)
    norm = lax.rsqrt(ss_ref[...] + TINY) * scale

    def step(i, carry):
        base = pl.multiple_of(i * HY_CHUNK, HY_CHUNK)
        rows_c = pl.ds(base, HY_CHUNK)
        rows_s = pl.ds(base + n, HY_CHUNK)
        tw = _hy_load_twiddles(twr_ref, twi_ref, rows_c, radix)
        for t in range(cb // LANES):
            lanes = slice(t * LANES, (t + 1) * LANES)
            fa = _fft_list(_hy_twiddled_inputs(za, tw, rows_c, rows_s, lanes, radix), -1)
            fb = _fft_list(_hy_twiddled_inputs(zb, tw, rows_c, rows_s, lanes, radix), -1)
            for j in range(radix):
                hr_ref[j, rows_c, lanes] = fa[j][0] * norm[:, lanes]
                hi_ref[j, rows_c, lanes] = -fb[j][1] * norm[:, lanes]
        return carry

    lax.fori_loop(0, n // HY_CHUNK, step, 0)


def _hyena_spectra(length, tables, w1, b1, w2, b2, w3, freq, width, tl):
    f32 = F32
    pos = jnp.arange(length, dtype=f32)
    t = pos / length
    bands = jnp.linspace(1e-4, HY_BANDS - 1, HY_BANDS, dtype=f32)
    ang = (2.0 * math.pi / length) * pos[:, None] * bands[None, :]
    emb = jnp.concatenate([t[:, None], jnp.cos(ang), jnp.sin(ang)], axis=-1)
    deltas = jnp.abs(jnp.linspace(math.log(HY_TARGET) / HY_SLOW_DECAY,
                                  math.log(HY_TARGET) / HY_FAST_DECAY, width, dtype=f32))
    n_emb = -(-emb.shape[1] // LANES) * LANES
    w1 = jnp.pad(w1.astype(f32), ((0, n_emb - emb.shape[1]), (0, 0)))
    emb = jnp.pad(emb, ((0, 0), (0, n_emb - emb.shape[1])))
    hid = w1.shape[1]
    cols = HY_ORDER * width
    tl = min(tl, length)
    full = lambda *s: pl.BlockSpec(s, lambda i: (0,) * len(s))
    a_un, b_un, ss = pl.pallas_call(
        functools.partial(_filter_kernel, length=length),
        out_shape=(jax.ShapeDtypeStruct((length, cols), f32),
                   jax.ShapeDtypeStruct((length, cols), f32),
                   jax.ShapeDtypeStruct((1, cols), f32)),
        grid=(length // tl,),
        in_specs=[
            pl.BlockSpec((tl, n_emb), lambda i: (i, 0)),
            full(n_emb, hid), full(1, hid), full(hid, hid), full(1, hid),
            full(hid, 2 * cols), full(1, hid), full(1, width),
        ],
        out_specs=(pl.BlockSpec((tl, cols), lambda i: (i, 0)),
                   pl.BlockSpec((tl, cols), lambda i: (i, 0)),
                   pl.BlockSpec((1, cols), lambda i: (0, 0))),
        compiler_params=_cparams(("arbitrary",)),
        name="hyena_filter_mlp",
    )(emb, w1, b1.reshape(1, hid), w2, b2.reshape(1, hid), w3, freq.reshape(1, hid),
      deltas.reshape(1, width))
    fwd, _, twr, twi = tables
    radix, n, _ = twr.shape
    cb = LANES
    return pl.pallas_call(
        functools.partial(_spectrum_kernel, scale=1.0 / length),
        out_shape=(jax.ShapeDtypeStruct((radix, n, cols), f32),
                   jax.ShapeDtypeStruct((radix, n, cols), f32)),
        grid=(cols // cb,),
        in_specs=[
            pl.BlockSpec((length, cb), lambda j: (0, j)),
            pl.BlockSpec((length, cb), lambda j: (0, j)),
            pl.BlockSpec((1, cb), lambda j: (0, j)),
            pl.BlockSpec((2 * n, n), lambda j: (0, 0)),
            pl.BlockSpec((radix, n, LANES), lambda j: (0, 0, 0)),
            pl.BlockSpec((radix, n, LANES), lambda j: (0, 0, 0)),
        ],
        out_specs=(pl.BlockSpec((radix, n, cb), lambda j: (0, 0, j)),
                   pl.BlockSpec((radix, n, cb), lambda j: (0, 0, j))),
        scratch_shapes=[pltpu.VMEM((cb // LANES, length, LANES), f32),
                        pltpu.VMEM((radix, 2 * n, cb), f32),
                        pltpu.VMEM((radix, 2 * n, cb), f32)],
        compiler_params=_cparams(("arbitrary",)),
        name="hyena_filter_spectrum",
    )(a_un, b_un, ss, fwd, twr, twi)


def _short_conv(x, w_ref, b_ref):
    length = x.shape[0]
    rows = lax.broadcasted_iota(jnp.int32, (length, 1), 0)
    prev = jnp.where(rows > 0, pltpu.roll(x, 1, axis=0), 0.0)
    nxt = jnp.where(rows < length - 1, pltpu.roll(x, length - 1, axis=0), 0.0)
    return w_ref[0:1, :] * prev + w_ref[1:2, :] * x + w_ref[2:3, :] * nxt + b_ref[...]


def _hyena_conv_kernel(*refs, conv_a, has_gate, m_blk, g_blk):
    it = iter(refs)
    a_ref = next(it)
    aw_ref = ab_ref = None
    if conv_a:
        aw_ref, ab_ref = next(it), next(it)
    m_hbm, mw_ref, mb_ref = next(it), next(it), next(it)
    g_hbm = next(it) if has_gate else None
    skip_ref, fwd_ref, inv_ref, twr_ref, twi_ref, hr_ref, hi_ref = (next(it) for _ in range(7))
    o_ref = next(it)
    slab, z_s, m_ref = next(it), next(it), next(it)
    g_ref = next(it) if has_gate else None
    sems = next(it)
    radix, n, cb = hr_ref.shape
    tiles = [slice(t * LANES, (t + 1) * LANES) for t in range(cb // LANES)]

    def side_copies():
        c, b = pl.program_id(0), pl.program_id(1)
        cps = [pltpu.make_async_copy(m_hbm.at[b, :, pl.ds((m_blk + c) * cb, cb)], m_ref, sems.at[0])]
        if has_gate:
            cps.append(pltpu.make_async_copy(g_hbm.at[b, :, pl.ds((g_blk + c) * cb, cb)], g_ref,
                                             sems.at[1]))
        return cps

    for cp in side_copies():
        cp.start()

    def a_value():
        a = a_ref[...]
        return _short_conv(a, aw_ref, ab_ref) if conv_a else a

    a = a_value()
    for s, lanes in enumerate(tiles):
        slab[s] = a[:, lanes]
    _hy_sub_transforms(slab, fwd_ref, z_s, radix, n)

    def step(i, carry):
        base = pl.multiple_of(i * HY_CHUNK, HY_CHUNK)
        rows_c = pl.ds(base, HY_CHUNK)
        rows_s = pl.ds(base + n, HY_CHUNK)
        tw = _hy_load_twiddles(twr_ref, twi_ref, rows_c, radix)
        for lanes in tiles:
            zf = _fft_list(_hy_twiddled_inputs(z_s, tw, rows_c, rows_s, lanes, radix), -1)
            ys = []
            for j in range(radix):
                hr = hr_ref[j, rows_c, lanes]
                hi = hi_ref[j, rows_c, lanes]
                zr, zi = zf[j]
                ys.append((zr * hr - zi * hi, zr * hi + zi * hr))
            vs = _fft_list(ys, +1)
            for r in range(radix):
                ur, ui = vs[r]
                if r > 0:
                    wr, wi = tw[r - 1]
                    ur, ui = wr * ur - wi * ui, wr * ui + wi * ur
                z_s[r, rows_c, lanes] = ur
                z_s[r, rows_s, lanes] = ui
        return carry

    lax.fori_loop(0, n // HY_CHUNK, step, 0)

    for r in range(radix):
        yr = jnp.dot(inv_ref[...], z_s[r].astype(BF16), preferred_element_type=F32)
        for s, lanes in enumerate(tiles):
            if radix > 1:
                slab[s, pl.ds(r, n, stride=radix), :] = yr[:, lanes]
            else:
                slab[s] = yr[:, lanes]

    for cp in side_copies():
        cp.wait()
    a = a_value()
    m = _short_conv(m_ref[...], mw_ref, mb_ref)
    for s, lanes in enumerate(tiles):
        y = m[:, lanes] * (slab[s] + skip_ref[:, lanes] * a[:, lanes])
        if has_gate:
            y = y * _silu(g_ref[:, lanes])
        o_ref[:, lanes] = y


def _hyena_conv(a_src, a_blk, a_conv, m_src, m_blk, m_conv, g_src, g_blk, skip, tables, spectra,
                order, width, cb):
    bsz, length, _ = m_src.shape
    fwd, inv, twr, twi = tables
    hr, hi = spectra
    radix, n, _ = twr.shape
    nc = width // cb
    conv_a = a_conv is not None
    has_gate = g_src is not None
    once = dict(pipeline_mode=pl.Buffered(1))
    inputs, specs = [], []

    def add(arr, spec):
        inputs.append(arr)
        specs.append(spec)

    add(a_src, pl.BlockSpec((None, length, cb), lambda c, b: (b, 0, a_blk + c)))
    if conv_a:
        add(a_conv[0], pl.BlockSpec((HY_CONV, cb), lambda c, b: (0, a_conv[2] + c)))
        add(a_conv[1], pl.BlockSpec((1, cb), lambda c, b: (0, a_conv[2] + c)))
    add(m_src, pl.BlockSpec(memory_space=pl.ANY))
    add(m_conv[0], pl.BlockSpec((HY_CONV, cb), lambda c, b: (0, m_conv[2] + c)))
    add(m_conv[1], pl.BlockSpec((1, cb), lambda c, b: (0, m_conv[2] + c)))
    if has_gate:
        add(g_src, pl.BlockSpec(memory_space=pl.ANY))
    add(skip, pl.BlockSpec((None, 1, cb), lambda c, b: (order, 0, c)))
    add(fwd, pl.BlockSpec((2 * n, n), lambda c, b: (0, 0), **once))
    add(inv, pl.BlockSpec((n, 2 * n), lambda c, b: (0, 0), **once))
    add(twr, pl.BlockSpec((radix, n, LANES), lambda c, b: (0, 0, 0), **once))
    add(twi, pl.BlockSpec((radix, n, LANES), lambda c, b: (0, 0, 0), **once))
    add(hr, pl.BlockSpec((radix, n, cb), lambda c, b: (0, 0, order * nc + c), **once))
    add(hi, pl.BlockSpec((radix, n, cb), lambda c, b: (0, 0, order * nc + c), **once))
    side_bufs = [pltpu.VMEM((length, cb), F32)] * (2 if has_gate else 1)
    return pl.pallas_call(
        functools.partial(_hyena_conv_kernel, conv_a=conv_a, has_gate=has_gate, m_blk=m_blk,
                          g_blk=g_blk),
        out_shape=jax.ShapeDtypeStruct((bsz, length, width), F32),
        grid=(nc, bsz),
        in_specs=specs,
        out_specs=pl.BlockSpec((None, length, cb), lambda c, b: (b, 0, c)),
        scratch_shapes=[pltpu.VMEM((cb // LANES, length, LANES), F32),
                        pltpu.VMEM((radix, 2 * n, cb), F32)] + side_bufs
                       + [pltpu.SemaphoreType.DMA((2,))],
        compiler_params=_cparams(("arbitrary", "arbitrary")),
        name="hyena_long_conv_o%d" % order,
    )(*inputs)


def _outproj_kernel(x_ref, ya_ref, ga_ref, yb_ref, of_ref, ob_ref, gc_ref, hgn_ref, w_ref, mod_ref,
                    fg_ref, o_ref, *, final, n_heads):
    wa = ya_ref.shape[1]
    wb = yb_ref.shape[1]
    wc = of_ref.shape[1]
    dv = wc // n_heads
    ya = (ya_ref[...] * _silu(ga_ref[...])).astype(BF16)
    acc = jnp.dot(ya, w_ref[0:wa, :], preferred_element_type=F32)
    acc = acc + jnp.dot(yb_ref[...].astype(BF16), w_ref[wa:wa + wb, :], preferred_element_type=F32)
    oc = of_ref[...] + ob_ref[...]
    gate_c = _silu(gc_ref[...]) * hgn_ref[...]
    for h in range(n_heads):
        sl = slice(h * dv, (h + 1) * dv)
        och = oc[:, sl]
        ms = jnp.mean(och * och, axis=-1, keepdims=True)
        ych = (och * lax.rsqrt(ms + EPS) * gate_c[:, sl]).astype(BF16)
        acc = acc + jnp.dot(ych, w_ref[wa + wb + h * dv:wa + wb + (h + 1) * dv, :],
                            preferred_element_type=F32)
    xn = x_ref[...] + mod_ref[2:3, :] * acc
    if final:
        ms = jnp.mean(xn * xn, axis=-1, keepdims=True)
        xn = xn * lax.rsqrt(ms + EPS) * fg_ref[...]
    o_ref[...] = xn


def _out_projection(x, ya, proj, ga_blk, yb, o_dirs, gc_blk, hg_norm, w_out_bf16, mod, final_g, final,
                    tm):
    bsz, t, d = x.shape
    wa = ya.shape[2]
    wb = yb.shape[2]
    wc = o_dirs.shape[3]
    tm = min(tm, t)
    return pl.pallas_call(
        functools.partial(_outproj_kernel, final=final, n_heads=HG_HEADS),
        out_shape=jax.ShapeDtypeStruct((bsz, t, d), F32),
        grid=(bsz, t // tm),
        in_specs=[
            pl.BlockSpec((None, tm, d), lambda b, i: (b, i, 0)),
            pl.BlockSpec((None, tm, wa), lambda b, i: (b, i, 0)),
            pl.BlockSpec((None, tm, wa), lambda b, i: (b, i, ga_blk)),
            pl.BlockSpec((None, tm, wb), lambda b, i: (b, i, 0)),
            pl.BlockSpec((None, None, tm, wc), lambda b, i: (0, b, i, 0)),
            pl.BlockSpec((None, None, tm, wc), lambda b, i: (1, b, i, 0)),
            pl.BlockSpec((None, tm, wc), lambda b, i: (b, i, gc_blk)),
            pl.BlockSpec((1, wc), lambda b, i: (0, 0)),
            pl.BlockSpec((wa + wb + wc, d), lambda b, i: (0, 0)),
            pl.BlockSpec((None, 3, d), lambda b, i: (b, 0, 0)),
            pl.BlockSpec((1, d), lambda b, i: (0, 0)),
        ],
        out_specs=pl.BlockSpec((None, tm, d), lambda b, i: (b, i, 0)),
        compiler_params=_cparams(("arbitrary", "arbitrary")),
        name="gate_outproj_residual",
    )(x, ya, proj, yb, o_dirs, o_dirs, proj, hg_norm.reshape(1, wc), w_out_bf16, mod,
      final_g.reshape(1, d))


def _block_diag(w):
    two, nh, n, _ = w.shape
    eye = jnp.eye(nh, dtype=w.dtype)
    dense = jnp.einsum("dhij,hg->dhigj", w, eye).reshape(two, nh * n, nh * n)
    return dense.astype(BF16)


def _mixers(x, mod, norm_g, lp, init, grid_rows, with_output):
    bsz, t, d = x.shape
    wl = lp["lru_w"]
    wh = lp["hy_w"]
    wg = lp["hg_w"]
    if grid_rows is None:
        proj = _in_projection(x.reshape(1, bsz * t, d), mod[:1], norm_g, lp["w_in"], tm=1024, tn=2048)
        proj = proj.reshape(bsz, t, -1)
    else:
        proj = _in_projection(x, mod, norm_g, lp["w_in"], tm=1024, tn=2048)
    xa = proj[:, :, :wl]
    if grid_rows is not None:
        x_tm = xa.reshape(bsz, grid_rows, GRID_W, wl).transpose(2, 1, 0, 3).reshape(t, bsz, wl)
    else:
        x_tm = xa.transpose(1, 0, 2)
    h_dirs, h_fin = _rglru(x_tm, init[0], lp["lru_conv_w"], lp["lru_conv_b"], lp["wa_dense"],
                           lp["lru_ba"], lp["wx_dense"], lp["lru_bx"], lp["lru_lam"], tp=128)
    q_off = 2 * wl + 4 * wh
    o_dirs, s_fin = _hgrn2(proj, q_off // wg, q_off // wg + 1, q_off // wg + 3, lp["hg_lb"], init[1],
                           tT=512)
    states = (h_fin, s_fin)
    if not with_output:
        return None, states
    ya_tm = h_dirs[0] + h_dirs[1]
    if grid_rows is not None:
        ya = ya_tm.reshape(GRID_W, grid_rows, bsz, wl).transpose(2, 1, 0, 3).reshape(bsz, t, wl)
    else:
        ya = ya_tm.transpose(1, 0, 2)
    tables = lp["dft"][t]
    spectra = _hyena_spectra(t, tables, lp["hy_w1"], lp["hy_b1"], lp["hy_w2"],
                             lp["hy_b2"], lp["hy_w3"], lp["hy_freq"], wh, tl=512)
    cb = min(wh, 256)
    ub_blk = (2 * wl) // cb
    nb = wh // cb
    cw, cbias = lp["hy_conv_w"], lp["hy_conv_b"].reshape(1, -1)
    z = _hyena_conv(proj, ub_blk, (cw, cbias, 0), proj, ub_blk + nb, (cw, cbias, nb), None, 0,
                    lp["hy_skip"], tables, spectra, 0, wh, cb)
    yb = _hyena_conv(z, 0, None, proj, ub_blk + 2 * nb, (cw, cbias, 2 * nb), proj, ub_blk + 3 * nb,
                     lp["hy_skip"], tables, spectra, 1, wh, cb)
    return (ya, yb, o_dirs, proj), states


def kernel(x, c, ctx, c_ctx, norm_g, w_mod, b_mod, w_in, w_out, lru_conv_w, lru_conv_b, lru_wa, lru_ba,
           lru_wx, lru_bx, lru_lam, hy_conv_w, hy_conv_b, hy_w1, hy_b1, hy_w2, hy_b2, hy_w3, hy_freq,
           hy_skip, hg_lb_logits, hg_norm, final_g):
    bsz, seq, d = x.shape
    depth = w_mod.shape[0]
    ctx_len = ctx.shape[1]
    rows = seq // GRID_W
    wl = lru_lam.shape[-1]
    wh = hy_skip.shape[-1]
    wg = hg_norm.shape[-1]
    dk = wg // HG_HEADS

    p = jax.nn.softmax(hg_lb_logits.astype(F32), axis=0)
    lower_bounds = jnp.cumsum(p, axis=0) - p[0]

    n_rows = -(-(bsz + 1) // SUBLANES) * SUBLANES
    cvec = jnp.zeros((n_rows, d), F32).at[:bsz].set(c).at[bsz].set(c_ctx)
    mod_all = _modulation(cvec, w_mod, b_mod)

    dft = {length: _hyena_tables(length) for length in (seq, ctx_len)}

    zero_states = (jnp.zeros((2, bsz, wl), F32), jnp.zeros((2, bsz, HG_HEADS, dk, dk), F32))
    xc = ctx
    for layer in range(depth):
        last = layer == depth - 1
        mod = mod_all[layer, :bsz].reshape(bsz, 3, d)
        mod_c = jnp.broadcast_to(mod_all[layer, bsz].reshape(1, 3, d), (bsz, 3, d))
        lp = dict(
            w_in=w_in[layer].astype(BF16), lru_w=wl, hy_w=wh, hg_w=wg,
            lru_conv_w=lru_conv_w[layer], lru_conv_b=lru_conv_b[layer],
            wa_dense=_block_diag(lru_wa[layer]), wx_dense=_block_diag(lru_wx[layer]),
            lru_ba=lru_ba[layer], lru_bx=lru_bx[layer], lru_lam=lru_lam[layer],
            hy_conv_w=hy_conv_w[layer], hy_conv_b=hy_conv_b[layer], hy_w1=hy_w1[layer],
            hy_b1=hy_b1[layer], hy_w2=hy_w2[layer], hy_b2=hy_b2[layer], hy_w3=hy_w3[layer],
            hy_freq=hy_freq[layer], hy_skip=hy_skip[layer].reshape(HY_ORDER, 1, wh),
            hg_lb=lower_bounds[layer], dft=dft)
        w_out_l = w_out[layer].astype(BF16)
        ga_blk = 1
        gc_blk = (2 * wl + 4 * wh + 4 * wg) // wg
        pieces_c, ctx_states = _mixers(xc, mod_c, norm_g[layer], lp, zero_states, None, not last)
        pieces, _ = _mixers(x, mod, norm_g[layer], lp, ctx_states, rows, True)
        ya, yb, o_dirs, proj = pieces
        x = _out_projection(x, ya, proj, ga_blk, yb, o_dirs, gc_blk, hg_norm[layer], w_out_l, mod,
                            final_g, last, tm=512)
        if not last:
            ya, yb, o_dirs, proj = pieces_c
            xc = _out_projection(xc, ya, proj, ga_blk, yb, o_dirs, gc_blk, hg_norm[layer], w_out_l,
                                 mod_c, final_g, False, tm=512)
    return x
```

```python
import functools
import math

import numpy as np
import jax
import jax.numpy as jnp
from jax import lax
from jax.experimental import pallas as pl
from jax.experimental.pallas import tpu as pltpu

GRID_W = 64
LRU_HEADS = 8
LRU_CONV = 4
LRU_C = 8.0
HY_ORDER = 2
HY_CONV = 3
HY_BANDS = 16
HY_FAST_DECAY = 0.3
HY_SLOW_DECAY = 1.5
HY_TARGET = 1e-2
HG_HEADS = 8
HG_CHUNK = 64
HG_SAFE_SPAN = 60.0
EPS = 1e-6
TINY = 1e-12

LANES = 128
SUBLANES = 8
ROW_GROUP = 16
VMEM_LIMIT_BYTES = 56 * 1024 * 1024

F32 = jnp.float32
BF16 = jnp.bfloat16
PROJ_DTYPE = BF16


def _cparams(semantics):
    return pltpu.CompilerParams(dimension_semantics=semantics, vmem_limit_bytes=VMEM_LIMIT_BYTES)


def _sigmoid(x):
    return 1.0 / (1.0 + jnp.exp(-x))


def _silu(x):
    return x * _sigmoid(x)


def _mod_kernel(c_ref, w_ref, b_ref, o_ref):
    s = _silu(c_ref[...])
    o_ref[...] = jnp.dot(s, w_ref[...], preferred_element_type=F32,
                         precision=lax.Precision.HIGHEST) + b_ref[...]


def _modulation(cvec, w_mod, b_mod):
    depth, d, d3 = w_mod.shape
    r = cvec.shape[0]
    return pl.pallas_call(
        _mod_kernel,
        out_shape=jax.ShapeDtypeStruct((depth, r, d3), F32),
        grid=(depth, d3 // d),
        in_specs=[
            pl.BlockSpec((r, d), lambda l, j: (0, 0)),
            pl.BlockSpec((None, d, d), lambda l, j: (l, 0, j)),
            pl.BlockSpec((None, 1, d), lambda l, j: (l, 0, j)),
        ],
        out_specs=pl.BlockSpec((None, r, d), lambda l, j: (l, 0, j)),
        compiler_params=_cparams(("arbitrary", "arbitrary")),
        name="adaln_modulation",
    )(cvec, w_mod, b_mod.reshape(depth, 1, d3))


def _inproj_kernel(x_ref, mod_ref, g_ref, w_ref, o_ref, h_s):
    @pl.when(pl.program_id(2) == 0)
    def _():
        x = x_ref[...]
        ms = jnp.mean(x * x, axis=-1, keepdims=True)
        y = x * lax.rsqrt(ms + EPS) * g_ref[...]
        h = y * (1.0 + mod_ref[1:2, :]) + mod_ref[0:1, :]
        h_s[...] = h.astype(BF16)

    o_ref[...] = jnp.dot(h_s[...], w_ref[...], preferred_element_type=F32).astype(o_ref.dtype)


def _in_projection(x, mod, gain, w_bf16, tm, tn):
    bsz, t, d = x.shape
    d_in = w_bf16.shape[1]
    tm = min(tm, t)
    tn = min(tn, d_in)
    return pl.pallas_call(
        _inproj_kernel,
        out_shape=jax.ShapeDtypeStruct((bsz, t, d_in), PROJ_DTYPE),
        grid=(bsz, t // tm, d_in // tn),
        in_specs=[
            pl.BlockSpec((None, tm, d), lambda b, i, j: (b, i, 0)),
            pl.BlockSpec((None, 3, d), lambda b, i, j: (b, 0, 0)),
            pl.BlockSpec((1, d), lambda b, i, j: (0, 0)),
            pl.BlockSpec((d, tn), lambda b, i, j: (0, j)),
        ],
        out_specs=pl.BlockSpec((None, tm, tn), lambda b, i, j: (b, i, j)),
        scratch_shapes=[pltpu.VMEM((tm, d), BF16)],
        compiler_params=_cparams(("arbitrary", "arbitrary", "arbitrary")),
        name="rmsnorm_adaln_inproj",
    )(x, mod, gain.reshape(1, d), w_bf16)


def _lru_kernel(xprev_ref, x_ref, xnext_ref, h0_ref, cw_ref, cb_ref, wa_ref, ba_ref, wx_ref,
                bx_ref, lam_ref, h_ref, hfin_ref, xpad, a_s, b_s, hstate):
    d = pl.program_id(0)
    i = pl.program_id(1)
    n = pl.num_programs(1)
    idx = jnp.where(d == 0, i, n - 1 - i)
    tp, bsz, ch = x_ref.shape

    @pl.when(i == 0)
    def _():
        hstate[...] = h0_ref[...]

    xpad[0:2] = jnp.where(idx > 0, xprev_ref[...], 0.0)
    xpad[2:2 + tp] = x_ref[...]
    xpad[2 + tp:3 + tp] = jnp.where(idx < n - 1, xnext_ref[...], 0.0)
    u = cb_ref[...] + cw_ref[0:1, :] * xpad[0:tp]
    for k in range(1, LRU_CONV):
        u = u + cw_ref[k:k + 1, :] * xpad[k:k + tp]
    u2 = u.reshape(tp * bsz, ch)
    ub = u2.astype(BF16)
    r = _sigmoid(jnp.dot(ub, wa_ref[...], preferred_element_type=F32) + ba_ref[...])
    gi = _sigmoid(jnp.dot(ub, wx_ref[...], preferred_element_type=F32) + bx_ref[...])
    nlam = -lam_ref[...]
    softplus = jnp.maximum(nlam, 0.0) + jnp.log(1.0 + jnp.exp(-jnp.abs(nlam)))
    log_a = (-LRU_C) * r * softplus
    a = jnp.exp(log_a)
    bb = jnp.sqrt(jnp.maximum(1.0 - a * a, TINY)) * (gi * u2)
    a_s[...] = a.reshape(tp, bsz, ch)
    b_s[...] = bb.reshape(tp, bsz, ch)

    def body(t, h):
        tt = jnp.where(d == 0, t, tp - 1 - t)
        h = a_s[tt] * h + b_s[tt]
        h_ref[tt] = h
        return h

    h = lax.fori_loop(0, tp, body, hstate[...], unroll=8)
    hstate[...] = h
    hfin_ref[...] = h


def _rglru(x_tm, h0, conv_w, conv_b, wa_dense, ba, wx_dense, bx, lam, tp):
    p, bsz, ch = x_tm.shape
    tp = min(tp, p)
    n = p // tp

    def tile(d, i):
        return jnp.where(d == 0, i, n - 1 - i)

    vec = lambda a: a.reshape(2, 1, ch)
    return pl.pallas_call(
        _lru_kernel,
        out_shape=(jax.ShapeDtypeStruct((2, p, bsz, ch), F32),
                   jax.ShapeDtypeStruct((2, bsz, ch), F32)),
        grid=(2, n),
        in_specs=[
            pl.BlockSpec((2, bsz, ch), lambda d, i: (jnp.maximum(tile(d, i) * (tp // 2) - 1, 0), 0, 0)),
            pl.BlockSpec((tp, bsz, ch), lambda d, i: (tile(d, i), 0, 0)),
            pl.BlockSpec((1, bsz, ch), lambda d, i: (jnp.minimum((tile(d, i) + 1) * tp, p - 1), 0, 0)),
            pl.BlockSpec((None, bsz, ch), lambda d, i: (d, 0, 0)),
            pl.BlockSpec((LRU_CONV, ch), lambda d, i: (0, 0)),
            pl.BlockSpec((1, ch), lambda d, i: (0, 0)),
            pl.BlockSpec((None, ch, ch), lambda d, i: (d, 0, 0)),
            pl.BlockSpec((None, 1, ch), lambda d, i: (d, 0, 0)),
            pl.BlockSpec((None, ch, ch), lambda d, i: (d, 0, 0)),
            pl.BlockSpec((None, 1, ch), lambda d, i: (d, 0, 0)),
            pl.BlockSpec((None, 1, ch), lambda d, i: (d, 0, 0)),
        ],
        out_specs=(
            pl.BlockSpec((None, tp, bsz, ch), lambda d, i: (d, tile(d, i), 0, 0)),
            pl.BlockSpec((None, bsz, ch), lambda d, i: (d, 0, 0)),
        ),
        scratch_shapes=[
            pltpu.VMEM((tp + LRU_CONV - 1, bsz, ch), F32),
            pltpu.VMEM((tp, bsz, ch), F32),
            pltpu.VMEM((tp, bsz, ch), F32),
            pltpu.VMEM((bsz, ch), F32),
        ],
        compiler_params=_cparams(("arbitrary", "arbitrary")),
        name="rglru_bidir_scan",
    )(x_tm, x_tm, x_tm, h0, conv_w, conv_b.reshape(1, ch), wa_dense, vec(ba), wx_dense, vec(bx),
      vec(lam))


def _hgrn2_kernel(q_ref, z_ref, v_ref, lb_ref, s0_ref, o_ref, sfin_ref, st_a, st_b, st_c):
    d = pl.program_id(0)
    i = pl.program_id(2)
    tT = q_ref.shape[0]
    nh, dv, dk = st_a.shape
    c = HG_CHUNK
    nch = tT // c

    @pl.when(i == 0)
    def _():
        st_a[...] = s0_ref[...]

    row = lax.broadcasted_iota(jnp.int32, (c, c), 0)
    col = lax.broadcasted_iota(jnp.int32, (c, c), 1)
    fwd = d == 0
    keep = (col - row) * jnp.where(fwd, 1, -1) <= 0
    tri = jnp.where(keep, 1.0, 0.0).astype(BF16)
    tri3 = jnp.concatenate([tri, tri, tri], axis=1)
    nt_dims = (((1,), (1,)), ((), ()))
    tn_dims = (((0,), (0,)), ((), ()))

    def gates(z, lb):
        e = jnp.exp(-jnp.abs(z))
        s_big = 1.0 / (1.0 + e)
        sig = jnp.where(z >= 0, s_big, e * s_big)
        f = lb + (1.0 - lb) * sig
        return jnp.maximum(f, TINY), 1.0 - f

    heads = [(slice(h * dk, (h + 1) * dk), slice(h * dv, (h + 1) * dv)) for h in range(nh)]

    def chunk_base(j):
        return pl.multiple_of(jnp.where(fwd, j, nch - 1 - j) * c, c)

    def stage_a(rows):
        out = []
        for sl, _ in heads:
            f, kk = gates(z_ref[rows, sl].astype(F32), lb_ref[:, sl])
            g = jnp.log(f)
            g1 = g.astype(BF16)
            r1 = g - g1.astype(F32)
            g2 = r1.astype(BF16)
            g3 = (r1 - g2.astype(F32)).astype(BF16)
            gc = jnp.dot(tri3, jnp.concatenate([g1, g2, g3], axis=0), preferred_element_type=F32)
            out.append((kk, gc))
        return out

    def stage_b(rows, from_a):
        out = []
        span = jnp.zeros((1, dk), F32)
        for (sl, _), (kk, gc) in zip(heads, from_a):
            ref_row = gc[c // 2:c // 2 + 1, :]
            tot = jnp.where(fwd, gc[c - 1:c, :], gc[0:1, :])
            span = jnp.maximum(span, jnp.maximum(jnp.abs(gc[0:1, :] - ref_row),
                                                 jnp.abs(gc[c - 1:c, :] - ref_row)))
            e_q = jnp.exp(gc - ref_row)
            qp = q_ref[rows, sl].astype(F32) * e_q
            kp = kk * (1.0 / e_q)
            qpp = (qp * jnp.exp(ref_row)).astype(BF16)
            kpp = (kp * jnp.exp(tot - ref_row)).astype(BF16)
            sc = lax.dot_general(qp.astype(BF16), kp.astype(BF16), nt_dims, preferred_element_type=F32)
            out.append((sc, qpp, kpp, jnp.exp(tot)))
        return out, span

    def stage_c(rows, from_b, src, dst):
        for h, ((_, sv), (sc, qpp, kpp, dec)) in enumerate(zip(heads, from_b)):
            vb = v_ref[rows, sv].astype(BF16)
            s_t = src[h]
            sc = jnp.where(keep, sc, 0.0).astype(BF16)
            o = jnp.dot(sc, vb, preferred_element_type=F32)
            o = o + lax.dot_general(qpp, s_t.astype(BF16), nt_dims, preferred_element_type=F32)
            o_ref[rows, sv] = o
            upd = lax.dot_general(vb, kpp, tn_dims, preferred_element_type=F32)
            dst[h] = s_t * dec + upd

    def exact_chunk(base, src, dst):
        sub = lax.broadcasted_iota(jnp.int32, (ROW_GROUP, 1), 0)
        dst[...] = src[...]

        def step(u, carry):
            t = base + jnp.where(fwd, u, c - 1 - u)
            grp = pl.ds(pl.multiple_of((t // ROW_GROUP) * ROW_GROUP, ROW_GROUP), ROW_GROUP)
            live = sub == t % ROW_GROUP
            for h, (sl, sv) in enumerate(heads):
                f_g, k_g = gates(z_ref[grp, sl].astype(F32), lb_ref[:, sl])
                f = jnp.sum(jnp.where(live, f_g, 0.0), axis=0, keepdims=True)
                v_g = jnp.where(live, v_ref[grp, sv].astype(F32), 0.0)
                s_new = dst[h] * f + lax.dot_general(v_g, k_g, tn_dims, preferred_element_type=F32)
                dst[h] = s_new
                o_g = lax.dot_general(q_ref[grp, sl].astype(F32), s_new, nt_dims,
                                      preferred_element_type=F32)
                o_ref[grp, sv] = jnp.where(live, o_g, o_ref[grp, sv])
            return carry

        lax.fori_loop(0, c, step, 0)

    def chunk_pair(jj, carry):
        base0, base1 = chunk_base(2 * jj), chunk_base(2 * jj + 1)
        rows0, rows1 = pl.ds(base0, c), pl.ds(base1, c)
        b0, span0 = stage_b(rows0, stage_a(rows0))
        a1 = stage_a(rows1)
        stage_c(rows0, b0, st_a, st_b)
        b1, span1 = stage_b(rows1, a1)
        stage_c(rows1, b1, st_b, st_c)

        @pl.when(jnp.max(jnp.maximum(span0, span1)) > HG_SAFE_SPAN)
        def _():
            exact_chunk(base0, st_a, st_b)
            exact_chunk(base1, st_b, st_c)

        st_a[...] = st_c[...]
        return carry

    lax.fori_loop(0, nch // 2, chunk_pair, 0)
    sfin_ref[...] = st_a[...]


def _hgrn2(proj, q_blk, z_blk, v_blk, lb, s0, tT):
    bsz, t, _ = proj.shape
    _, _, nh, dv, dk = s0.shape
    w = nh * dk
    tT = min(tT, t)
    n = t // tT
    assert (tT // HG_CHUNK) % 2 == 0, "chunks are processed in pairs"

    def tile(d, i):
        return jnp.where(d == 0, i, n - 1 - i)

    return pl.pallas_call(
        _hgrn2_kernel,
        out_shape=(jax.ShapeDtypeStruct((2, bsz, t, nh * dv), F32),
                   jax.ShapeDtypeStruct((2, bsz, nh, dv, dk), F32)),
        grid=(2, bsz, n),
        in_specs=[
            pl.BlockSpec((None, tT, w), lambda d, b, i: (b, tile(d, i), q_blk)),
            pl.BlockSpec((None, tT, w), lambda d, b, i: (b, tile(d, i), z_blk + d)),
            pl.BlockSpec((None, tT, w), lambda d, b, i: (b, tile(d, i), v_blk)),
            pl.BlockSpec((None, 1, w), lambda d, b, i: (d, 0, 0)),
            pl.BlockSpec((None, None, nh, dv, dk), lambda d, b, i: (d, b, 0, 0, 0)),
        ],
        out_specs=(
            pl.BlockSpec((None, None, tT, nh * dv), lambda d, b, i: (d, b, tile(d, i), 0)),
            pl.BlockSpec((None, None, nh, dv, dk), lambda d, b, i: (d, b, 0, 0, 0)),
        ),
        scratch_shapes=[pltpu.VMEM((nh, dv, dk), F32)] * 3,
        compiler_params=_cparams(("arbitrary", "arbitrary", "arbitrary")),
        name="hgrn2_bidir_chunk_scan",
    )(proj, proj, proj, lb.reshape(2, 1, w), s0)


HY_MAX_RADIX = 8
HY_MIN_SUBLEN = 128
HY_CHUNK = 8


def _hyena_radix(length):
    return max(1, min(HY_MAX_RADIX, length // HY_MIN_SUBLEN))


def _hyena_tables(length):
    radix = _hyena_radix(length)
    n = length // radix
    k2 = 2 * jnp.arange(n, dtype=jnp.int32) + 1
    ph = (k2[:, None] * jnp.arange(n, dtype=jnp.int32)[None, :]) % (4 * n)
    psi = ph.astype(F32) * (2.0 * math.pi / (4 * n))
    fc, fs = jnp.cos(psi), jnp.sin(psi)
    fwd = jnp.concatenate([fc, fs], axis=0).astype(BF16)
    inv = jnp.concatenate([fc.T, -fs.T], axis=1).astype(BF16)
    phr = (jnp.arange(radix, dtype=jnp.int32)[:, None] * k2[None, :]) % (4 * length)
    phi = phr.astype(F32) * (2.0 * math.pi / (4 * length))
    rep = lambda a: jnp.broadcast_to(a[:, :, None], (radix, n, LANES))
    return fwd, inv, rep(jnp.cos(phi)), rep(jnp.sin(phi))


def _cmul_const(x, c, s):
    re, im = x
    if abs(s) < 1e-12:
        return (re, im) if c > 0 else (-re, -im)
    if abs(c) < 1e-12:
        return (-im, re) if s > 0 else (im, -re)
    return (re * c - im * s, re * s + im * c)


def _fft_list(xs, sign):
    r = len(xs)
    if r == 1:
        return xs
    ev = _fft_list(xs[0::2], sign)
    od = _fft_list(xs[1::2], sign)
    out = [None] * r
    for j in range(r // 2):
        ang = sign * 2.0 * math.pi * j / r
        t = _cmul_const(od[j], math.cos(ang), math.sin(ang))
        out[j] = (ev[j][0] + t[0], ev[j][1] + t[1])
        out[j + r // 2] = (ev[j][0] - t[0], ev[j][1] - t[1])
    return out


def _hy_sub_transforms(slab, fwd_ref, z_ref, radix, n):
    n_slabs = slab.shape[0]
    for r in range(radix):
        parts = [slab[s, pl.ds(r, n, stride=radix), :] if radix > 1 else slab[s]
                 for s in range(n_slabs)]
        xr = parts[0] if n_slabs == 1 else jnp.concatenate(parts, axis=1)
        z_ref[r] = jnp.dot(fwd_ref[...], xr.astype(BF16), preferred_element_type=F32)


def _hy_load_twiddles(twr_ref, twi_ref, rows_c, radix):
    return [(twr_ref[r, rows_c, :], twi_ref[r, rows_c, :]) for r in range(1, radix)]


def _hy_twiddled_inputs(z_ref, tw, rows_c, rows_s, lanes, radix):
    xs = []
    for r in range(radix):
        c = z_ref[r, rows_c, lanes]
        s = z_ref[r, rows_s, lanes]
        if r == 0:
            xs.append((c, -s))
        else:
            wr, wi = tw[r - 1]
            xs.append((c * wr - s * wi, -(c * wi + s * wr)))
    return xs


def _filter_kernel(emb_ref, w1_ref, b1_ref, w2_ref, b2_ref, w3_ref, fr_ref, dl_ref,
                   a_ref, b_ref, ss_ref, *, length):
    i = pl.program_id(0)
    tl = emb_ref.shape[0]
    cw = dl_ref.shape[1]
    fr = fr_ref[...]
    hi = lax.Precision.HIGHEST
    hdn = jnp.sin(fr * (jnp.dot(emb_ref[...], w1_ref[...], preferred_element_type=F32,
                                precision=hi) + b1_ref[...]))
    hdn = jnp.sin(fr * (jnp.dot(hdn, w2_ref[...], preferred_element_type=F32,
                                precision=hi) + b2_ref[...]))
    h = jnp.dot(hdn.astype(BF16), w3_ref[...].astype(BF16), preferred_element_type=F32)
    pos = (lax.broadcasted_iota(jnp.int32, (tl, 1), 0) + i * tl)
    t = pos.astype(F32) / float(length)
    win = jnp.exp(-t * dl_ref[...])
    not_first = pos > 0

    @pl.when(i == 0)
    def _():
        ss_ref[...] = jnp.zeros_like(ss_ref)

    for o in range(HY_ORDER):
        hpos = h[:, (2 * o) * cw:(2 * o + 1) * cw] * win
        hneg = jnp.where(not_first, h[:, (2 * o + 1) * cw:(2 * o + 2) * cw] * win, 0.0)
        a_ref[:, o * cw:(o + 1) * cw] = hpos + hneg
        b_ref[:, o * cw:(o + 1) * cw] = hneg - hpos
        ss_ref[:, o * cw:(o + 1) * cw] += jnp.sum(hpos * hpos + hneg * hneg, axis=0, keepdims=True)


def _spectrum_kernel(a_ref, b_ref, ss_ref, fwd_ref, twr_ref, twi_ref, hr_ref, hi_ref, slab, za, zb,
                     *, scale):
    radix, n, cb = hr_ref.shape
    for s in range(cb // LANES):
        slab[s] = a_ref[:, s * LANES:(s + 1) * LANES]
    _hy_sub_transforms(slab, fwd_ref, za, radix, n)
    for s in range(cb // LANES):
        slab[s] = b_ref[:, s * LANES:(s + 1) * LANES]
    _hy_sub_transforms(slab, fwd_ref, zb, radix, n)
    norm = lax.rsqrt(ss_ref[...] + TINY) * scale

    def step(i, carry):
        base = pl.multiple_of(i * HY_CHUNK, HY_CHUNK)
        rows_c = pl.ds(base, HY_CHUNK)
        rows_s = pl.ds(base + n, HY_CHUNK)
        tw = _hy_load_twiddles(twr_ref, twi_ref, rows_c, radix)
        for t in range(cb // LANES):
            lanes = slice(t * LANES, (t + 1) * LANES)
            fa = _fft_list(_hy_twiddled_inputs(za, tw, rows_c, rows_s, lanes, radix), -1)
            fb = _fft_list(_hy_twiddled_inputs(zb, tw, rows_c, rows_s, lanes, radix), -1)
            for j in range(radix):
                hr_ref[j, rows_c, lanes] = fa[j][0] * norm[:, lanes]
                hi_ref[j, rows_c, lanes] = -fb[j][1] * norm[:, lanes]
        return carry

    lax.fori_loop(0, n // HY_CHUNK, step, 0)


def _hyena_spectra(length, tables, w1, b1, w2, b2, w3, freq, width, tl):
    f32 = F32
    pos = jnp.arange(length, dtype=f32)
    t = pos / length
    bands = jnp.linspace(1e-4, HY_BANDS - 1, HY_BANDS, dtype=f32)
    ang = (2.0 * math.pi / length) * pos[:, None] * bands[None, :]
    emb = jnp.concatenate([t[:, None], jnp.cos(ang), jnp.sin(ang)], axis=-1)
    deltas = jnp.abs(jnp.linspace(math.log(HY_TARGET) / HY_SLOW_DECAY,
                                  math.log(HY_TARGET) / HY_FAST_DECAY, width, dtype=f32))
    n_emb = -(-emb.shape[1] // LANES) * LANES
    w1 = jnp.pad(w1.astype(f32), ((0, n_emb - emb.shape[1]), (0, 0)))
    emb = jnp.pad(emb, ((0, 0), (0, n_emb - emb.shape[1])))
    hid = w1.shape[1]
    cols = HY_ORDER * width
    tl = min(tl, length)
    full = lambda *s: pl.BlockSpec(s, lambda i: (0,) * len(s))
    a_un, b_un, ss = pl.pallas_call(
        functools.partial(_filter_kernel, length=length),
        out_shape=(jax.ShapeDtypeStruct((length, cols), f32),
                   jax.ShapeDtypeStruct((length, cols), f32),
                   jax.ShapeDtypeStruct((1, cols), f32)),
        grid=(length // tl,),
        in_specs=[
            pl.BlockSpec((tl, n_emb), lambda i: (i, 0)),
            full(n_emb, hid), full(1, hid), full(hid, hid), full(1, hid),
            full(hid, 2 * cols), full(1, hid), full(1, width),
        ],
        out_specs=(pl.BlockSpec((tl, cols), lambda i: (i, 0)),
                   pl.BlockSpec((tl, cols), lambda i: (i, 0)),
                   pl.BlockSpec((1, cols), lambda i: (0, 0))),
        compiler_params=_cparams(("arbitrary",)),
        name="hyena_filter_mlp",
    )(emb, w1, b1.reshape(1, hid), w2, b2.reshape(1, hid), w3, freq.reshape(1, hid),
      deltas.reshape(1, width))
    fwd, _, twr, twi = tables
    radix, n, _ = twr.shape
    cb = LANES
    return pl.pallas_call(
        functools.partial(_spectrum_kernel, scale=1.0 / length),
        out_shape=(jax.ShapeDtypeStruct((radix, n, cols), f32),
                   jax.ShapeDtypeStruct((radix, n, cols), f32)),
        grid=(cols // cb,),
        in_specs=[
            pl.BlockSpec((length, cb), lambda j: (0, j)),
            pl.BlockSpec((length, cb), lambda j: (0, j)),
            pl.BlockSpec((1, cb), lambda j: (0, j)),
            pl.BlockSpec((2 * n, n), lambda j: (0, 0)),
            pl.BlockSpec((radix, n, LANES), lambda j: (0, 0, 0)),
            pl.BlockSpec((radix, n, LANES), lambda j: (0, 0, 0)),
        ],
        out_specs=(pl.BlockSpec((radix, n, cb), lambda j: (0, 0, j)),
                   pl.BlockSpec((radix, n, cb), lambda j: (0, 0, j))),
        scratch_shapes=[pltpu.VMEM((cb // LANES, length, LANES), f32),
                        pltpu.VMEM((radix, 2 * n, cb), f32),
                        pltpu.VMEM((radix, 2 * n, cb), f32)],
        compiler_params=_cparams(("arbitrary",)),
        name="hyena_filter_spectrum",
    )(a_un, b_un, ss, fwd, twr, twi)


def _short_conv(x, w_ref, b_ref):
    length = x.shape[0]
    rows = lax.broadcasted_iota(jnp.int32, (length, 1), 0)
    prev = jnp.where(rows > 0, pltpu.roll(x, 1, axis=0), 0.0)
    nxt = jnp.where(rows < length - 1, pltpu.roll(x, length - 1, axis=0), 0.0)
    return w_ref[0:1, :] * prev + w_ref[1:2, :] * x + w_ref[2:3, :] * nxt + b_ref[...]


def _hyena_conv_kernel(*refs, conv_a, has_gate, m_blk, g_blk):
    it = iter(refs)
    a_ref = next(it)
    aw_ref = ab_ref = None
    if conv_a:
        aw_ref, ab_ref = next(it), next(it)
    m_hbm, mw_ref, mb_ref = next(it), next(it), next(it)
    g_hbm = next(it) if has_gate else None
    skip_ref, fwd_ref, inv_ref, twr_ref, twi_ref, hr_ref, hi_ref = (next(it) for _ in range(7))
    o_ref = next(it)
    slab, z_s, m_ref = next(it), next(it), next(it)
    g_ref = next(it) if has_gate else None
    sems = next(it)
    radix, n, cb = hr_ref.shape
    tiles = [slice(t * LANES, (t + 1) * LANES) for t in range(cb // LANES)]

    def side_copies():
        c, b = pl.program_id(0), pl.program_id(1)
        cps = [pltpu.make_async_copy(m_hbm.at[b, :, pl.ds((m_blk + c) * cb, cb)], m_ref, sems.at[0])]
        if has_gate:
            cps.append(pltpu.make_async_copy(g_hbm.at[b, :, pl.ds((g_blk + c) * cb, cb)], g_ref,
                                             sems.at[1]))
        return cps

    for cp in side_copies():
        cp.start()

    def a_value():
        a = a_ref[...].astype(F32)
        return _short_conv(a, aw_ref, ab_ref) if conv_a else a

    a = a_value()
    for s, lanes in enumerate(tiles):
        slab[s] = a[:, lanes]
    _hy_sub_transforms(slab, fwd_ref, z_s, radix, n)

    def step(i, carry):
        base = pl.multiple_of(i * HY_CHUNK, HY_CHUNK)
        rows_c = pl.ds(base, HY_CHUNK)
        rows_s = pl.ds(base + n, HY_CHUNK)
        tw = _hy_load_twiddles(twr_ref, twi_ref, rows_c, radix)
        for lanes in tiles:
            zf = _fft_list(_hy_twiddled_inputs(z_s, tw, rows_c, rows_s, lanes, radix), -1)
            ys = []
            for j in range(radix):
                hr = hr_ref[j, rows_c, lanes]
                hi = hi_ref[j, rows_c, lanes]
                zr, zi = zf[j]
                ys.append((zr * hr - zi * hi, zr * hi + zi * hr))
            vs = _fft_list(ys, +1)
            for r in range(radix):
                ur, ui = vs[r]
                if r > 0:
                    wr, wi = tw[r - 1]
                    ur, ui = wr * ur - wi * ui, wr * ui + wi * ur
                z_s[r, rows_c, lanes] = ur
                z_s[r, rows_s, lanes] = ui
        return carry

    lax.fori_loop(0, n // HY_CHUNK, step, 0)

    for r in range(radix):
        yr = jnp.dot(inv_ref[...], z_s[r].astype(BF16), preferred_element_type=F32)
        for s, lanes in enumerate(tiles):
            if radix > 1:
                slab[s, pl.ds(r, n, stride=radix), :] = yr[:, lanes]
            else:
                slab[s] = yr[:, lanes]

    for cp in side_copies():
        cp.wait()
    a = a_value()
    m = _short_conv(m_ref[...].astype(F32), mw_ref, mb_ref)
    for s, lanes in enumerate(tiles):
        y = m[:, lanes] * (slab[s] + skip_ref[:, lanes] * a[:, lanes])
        if has_gate:
            y = y * _silu(g_ref[:, lanes].astype(F32))
        o_ref[:, lanes] = y.astype(o_ref.dtype)


def _hyena_conv(a_src, a_blk, a_conv, m_src, m_blk, m_conv, g_src, g_blk, skip, tables, spectra,
                order, width, cb, out_dtype):
    bsz, length, _ = m_src.shape
    fwd, inv, twr, twi = tables
    hr, hi = spectra
    radix, n, _ = twr.shape
    nc = width // cb
    conv_a = a_conv is not None
    has_gate = g_src is not None
    once = dict(pipeline_mode=pl.Buffered(1))
    inputs, specs = [], []

    def add(arr, spec):
        inputs.append(arr)
        specs.append(spec)

    add(a_src, pl.BlockSpec((None, length, cb), lambda c, b: (b, 0, a_blk + c)))
    if conv_a:
        add(a_conv[0], pl.BlockSpec((HY_CONV, cb), lambda c, b: (0, a_conv[2] + c)))
        add(a_conv[1], pl.BlockSpec((1, cb), lambda c, b: (0, a_conv[2] + c)))
    add(m_src, pl.BlockSpec(memory_space=pl.ANY))
    add(m_conv[0], pl.BlockSpec((HY_CONV, cb), lambda c, b: (0, m_conv[2] + c)))
    add(m_conv[1], pl.BlockSpec((1, cb), lambda c, b: (0, m_conv[2] + c)))
    if has_gate:
        add(g_src, pl.BlockSpec(memory_space=pl.ANY))
    add(skip, pl.BlockSpec((None, 1, cb), lambda c, b: (order, 0, c)))
    add(fwd, pl.BlockSpec((2 * n, n), lambda c, b: (0, 0), **once))
    add(inv, pl.BlockSpec((n, 2 * n), lambda c, b: (0, 0), **once))
    add(twr, pl.BlockSpec((radix, n, LANES), lambda c, b: (0, 0, 0), **once))
    add(twi, pl.BlockSpec((radix, n, LANES), lambda c, b: (0, 0, 0), **once))
    add(hr, pl.BlockSpec((radix, n, cb), lambda c, b: (0, 0, order * nc + c), **once))
    add(hi, pl.BlockSpec((radix, n, cb), lambda c, b: (0, 0, order * nc + c), **once))
    side_bufs = [pltpu.VMEM((length, cb), m_src.dtype)] * (2 if has_gate else 1)
    return pl.pallas_call(
        functools.partial(_hyena_conv_kernel, conv_a=conv_a, has_gate=has_gate, m_blk=m_blk,
                          g_blk=g_blk),
        out_shape=jax.ShapeDtypeStruct((bsz, length, width), out_dtype),
        grid=(nc, bsz),
        in_specs=specs,
        out_specs=pl.BlockSpec((None, length, cb), lambda c, b: (b, 0, c)),
        scratch_shapes=[pltpu.VMEM((cb // LANES, length, LANES), F32),
                        pltpu.VMEM((radix, 2 * n, cb), F32)] + side_bufs
                       + [pltpu.SemaphoreType.DMA((2,))],
        compiler_params=_cparams(("arbitrary", "arbitrary")),
        name="hyena_long_conv_o%d" % order,
    )(*inputs)


def _outproj_kernel(x_ref, ya_ref, ga_ref, yb_ref, of_ref, ob_ref, gc_ref, hgn_ref, w_ref, mod_ref,
                    fg_ref, o_ref, *, final, n_heads):
    wa = ya_ref.shape[1]
    wb = yb_ref.shape[1]
    wc = of_ref.shape[1]
    dv = wc // n_heads
    ya = (ya_ref[...] * _silu(ga_ref[...].astype(F32))).astype(BF16)
    acc = jnp.dot(ya, w_ref[0:wa, :], preferred_element_type=F32)
    acc = acc + jnp.dot(yb_ref[...].astype(BF16), w_ref[wa:wa + wb, :], preferred_element_type=F32)
    oc = of_ref[...] + ob_ref[...]
    gate_c = _silu(gc_ref[...].astype(F32)) * hgn_ref[...]
    for h in range(n_heads):
        sl = slice(h * dv, (h + 1) * dv)
        och = oc[:, sl]
        ms = jnp.mean(och * och, axis=-1, keepdims=True)
        ych = (och * lax.rsqrt(ms + EPS) * gate_c[:, sl]).astype(BF16)
        acc = acc + jnp.dot(ych, w_ref[wa + wb + h * dv:wa + wb + (h + 1) * dv, :],
                            preferred_element_type=F32)
    xn = x_ref[...] + mod_ref[2:3, :] * acc
    if final:
        ms = jnp.mean(xn * xn, axis=-1, keepdims=True)
        xn = xn * lax.rsqrt(ms + EPS) * fg_ref[...]
    o_ref[...] = xn


def _out_projection(x, ya, proj, ga_blk, yb, o_dirs, gc_blk, hg_norm, w_out_bf16, mod, final_g, final,
                    tm):
    bsz, t, d = x.shape
    wa = ya.shape[2]
    wb = yb.shape[2]
    wc = o_dirs.shape[3]
    tm = min(tm, t)
    return pl.pallas_call(
        functools.partial(_outproj_kernel, final=final, n_heads=HG_HEADS),
        out_shape=jax.ShapeDtypeStruct((bsz, t, d), F32),
        grid=(bsz, t // tm),
        in_specs=[
            pl.BlockSpec((None, tm, d), lambda b, i: (b, i, 0)),
            pl.BlockSpec((None, tm, wa), lambda b, i: (b, i, 0)),
            pl.BlockSpec((None, tm, wa), lambda b, i: (b, i, ga_blk)),
            pl.BlockSpec((None, tm, wb), lambda b, i: (b, i, 0)),
            pl.BlockSpec((None, None, tm, wc), lambda b, i: (0, b, i, 0)),
            pl.BlockSpec((None, None, tm, wc), lambda b, i: (1, b, i, 0)),
            pl.BlockSpec((None, tm, wc), lambda b, i: (b, i, gc_blk)),
            pl.BlockSpec((1, wc), lambda b, i: (0, 0)),
            pl.BlockSpec((wa + wb + wc, d), lambda b, i: (0, 0)),
            pl.BlockSpec((None, 3, d), lambda b, i: (b, 0, 0)),
            pl.BlockSpec((1, d), lambda b, i: (0, 0)),
        ],
        out_specs=pl.BlockSpec((None, tm, d), lambda b, i: (b, i, 0)),
        compiler_params=_cparams(("arbitrary", "arbitrary")),
        name="gate_outproj_residual",
    )(x, ya, proj, yb, o_dirs, o_dirs, proj, hg_norm.reshape(1, wc), w_out_bf16, mod,
      final_g.reshape(1, d))


def _block_diag(w):
    two, nh, n, _ = w.shape
    eye = jnp.eye(nh, dtype=w.dtype)
    dense = jnp.einsum("dhij,hg->dhigj", w, eye).reshape(two, nh * n, nh * n)
    return dense.astype(BF16)


def _mixers(x, mod, norm_g, lp, init, grid_rows, with_output):
    bsz, t, d = x.shape
    wl = lp["lru_w"]
    wh = lp["hy_w"]
    wg = lp["hg_w"]
    if grid_rows is None:
        proj = _in_projection(x.reshape(1, bsz * t, d), mod[:1], norm_g, lp["w_in"], tm=1024, tn=4096)
        proj = proj.reshape(bsz, t, -1)
    else:
        proj = _in_projection(x, mod, norm_g, lp["w_in"], tm=1024, tn=4096)
    xa = proj[:, :, :wl].astype(F32)
    if grid_rows is not None:
        x_tm = xa.reshape(bsz, grid_rows, GRID_W, wl).transpose(2, 1, 0, 3).reshape(t, bsz, wl)
    else:
        x_tm = xa.transpose(1, 0, 2)
    h_dirs, h_fin = _rglru(x_tm, init[0], lp["lru_conv_w"], lp["lru_conv_b"], lp["wa_dense"],
                           lp["lru_ba"], lp["wx_dense"], lp["lru_bx"], lp["lru_lam"], tp=128)
    q_off = 2 * wl + 4 * wh
    o_dirs, s_fin = _hgrn2(proj, q_off // wg, q_off // wg + 1, q_off // wg + 3, lp["hg_lb"], init[1],
                           tT=512)
    states = (h_fin, s_fin)
    if not with_output:
        return None, states
    ya_tm = h_dirs[0] + h_dirs[1]
    if grid_rows is not None:
        ya = ya_tm.reshape(GRID_W, grid_rows, bsz, wl).transpose(2, 1, 0, 3).reshape(bsz, t, wl)
    else:
        ya = ya_tm.transpose(1, 0, 2)
    tables = lp["dft"][t]
    spectra = _hyena_spectra(t, tables, lp["hy_w1"], lp["hy_b1"], lp["hy_w2"],
                             lp["hy_b2"], lp["hy_w3"], lp["hy_freq"], wh, tl=512)
    cb = min(wh, 256)
    ub_blk = (2 * wl) // cb
    nb = wh // cb
    cw, cbias = lp["hy_conv_w"], lp["hy_conv_b"].reshape(1, -1)
    z = _hyena_conv(proj, ub_blk, (cw, cbias, 0), proj, ub_blk + nb, (cw, cbias, nb), None, 0,
                    lp["hy_skip"], tables, spectra, 0, wh, cb, F32)
    yb = _hyena_conv(z, 0, None, proj, ub_blk + 2 * nb, (cw, cbias, 2 * nb), proj, ub_blk + 3 * nb,
                     lp["hy_skip"], tables, spectra, 1, wh, cb, BF16)
    return (ya, yb, o_dirs, proj), states


def kernel(x, c, ctx, c_ctx, norm_g, w_mod, b_mod, w_in, w_out, lru_conv_w, lru_conv_b, lru_wa, lru_ba,
           lru_wx, lru_bx, lru_lam, hy_conv_w, hy_conv_b, hy_w1, hy_b1, hy_w2, hy_b2, hy_w3, hy_freq,
           hy_skip, hg_lb_logits, hg_norm, final_g):
    bsz, seq, d = x.shape
    depth = w_mod.shape[0]
    ctx_len = ctx.shape[1]
    rows = seq // GRID_W
    wl = lru_lam.shape[-1]
    wh = hy_skip.shape[-1]
    wg = hg_norm.shape[-1]
    dk = wg // HG_HEADS

    p = jax.nn.softmax(hg_lb_logits.astype(F32), axis=0)
    lower_bounds = jnp.cumsum(p, axis=0) - p[0]

    n_rows = -(-(bsz + 1) // SUBLANES) * SUBLANES
    cvec = jnp.zeros((n_rows, d), F32).at[:bsz].set(c).at[bsz].set(c_ctx)
    mod_all = _modulation(cvec, w_mod, b_mod)

    dft = {length: _hyena_tables(length) for length in (seq, ctx_len)}

    zero_states = (jnp.zeros((2, bsz, wl), F32), jnp.zeros((2, bsz, HG_HEADS, dk, dk), F32))
    xc = ctx
    for layer in range(depth):
        last = layer == depth - 1
        mod = mod_all[layer, :bsz].reshape(bsz, 3, d)
        mod_c = jnp.broadcast_to(mod_all[layer, bsz].reshape(1, 3, d), (bsz, 3, d))
        lp = dict(
            w_in=w_in[layer].astype(BF16), lru_w=wl, hy_w=wh, hg_w=wg,
            lru_conv_w=lru_conv_w[layer], lru_conv_b=lru_conv_b[layer],
            wa_dense=_block_diag(lru_wa[layer]), wx_dense=_block_diag(lru_wx[layer]),
            lru_ba=lru_ba[layer], lru_bx=lru_bx[layer], lru_lam=lru_lam[layer],
            hy_conv_w=hy_conv_w[layer], hy_conv_b=hy_conv_b[layer], hy_w1=hy_w1[layer],
            hy_b1=hy_b1[layer], hy_w2=hy_w2[layer], hy_b2=hy_b2[layer], hy_w3=hy_w3[layer],
            hy_freq=hy_freq[layer], hy_skip=hy_skip[layer].reshape(HY_ORDER, 1, wh),
            hg_lb=lower_bounds[layer], dft=dft)
        w_out_l = w_out[layer].astype(BF16)
        ga_blk = 1
        gc_blk = (2 * wl + 4 * wh + 4 * wg) // wg
        pieces_c, ctx_states = _mixers(xc, mod_c, norm_g[layer], lp, zero_states, None, not last)
        pieces, _ = _mixers(x, mod, norm_g[layer], lp, ctx_states, rows, True)
        ya, yb, o_dirs, proj = pieces
        x = _out_projection(x, ya, proj, ga_blk, yb, o_dirs, gc_blk, hg_norm[layer], w_out_l, mod,
                            final_g, last, tm=512)
        if not last:
            ya, yb, o_dirs, proj = pieces_c
            xc = _out_projection(xc, ya, proj, ga_blk, yb, o_dirs, gc_blk, hg_norm[layer], w_out_l,
                                 mod_c, final_g, False, tm=512)
    return x
```

```python
import functools
import math

import numpy as np
import jax
import jax.numpy as jnp
from jax import lax
from jax.experimental import pallas as pl
from jax.experimental.pallas import tpu as pltpu

GRID_W = 64
LRU_HEADS = 8
LRU_CONV = 4
LRU_C = 8.0
HY_ORDER = 2
HY_CONV = 3
HY_BANDS = 16
HY_FAST_DECAY = 0.3
HY_SLOW_DECAY = 1.5
HY_TARGET = 1e-2
HG_HEADS = 8
HG_CHUNK = 64
HG_GROUP = 8
HG_SAFE_SPAN = 60.0
EPS = 1e-6
TINY = 1e-12

LANES = 128
SUBLANES = 8
ROW_GROUP = 16
VMEM_LIMIT_BYTES = 56 * 1024 * 1024

F32 = jnp.float32
BF16 = jnp.bfloat16
PROJ_DTYPE = BF16


def _cparams(semantics):
    return pltpu.CompilerParams(dimension_semantics=semantics, vmem_limit_bytes=VMEM_LIMIT_BYTES)


def _sigmoid(x):
    return 1.0 / (1.0 + jnp.exp(-x))


def _silu(x):
    return x * _sigmoid(x)


def _mod_kernel(c_ref, w_ref, b_ref, o_ref):
    s = _silu(c_ref[...])
    o_ref[...] = jnp.dot(s, w_ref[...], preferred_element_type=F32,
                         precision=lax.Precision.HIGHEST) + b_ref[...]


def _modulation(cvec, w_mod, b_mod):
    depth, d, d3 = w_mod.shape
    r = cvec.shape[0]
    return pl.pallas_call(
        _mod_kernel,
        out_shape=jax.ShapeDtypeStruct((depth, r, d3), F32),
        grid=(depth, d3 // d),
        in_specs=[
            pl.BlockSpec((r, d), lambda l, j: (0, 0)),
            pl.BlockSpec((None, d, d), lambda l, j: (l, 0, j)),
            pl.BlockSpec((None, 1, d), lambda l, j: (l, 0, j)),
        ],
        out_specs=pl.BlockSpec((None, r, d), lambda l, j: (l, 0, j)),
        compiler_params=_cparams(("arbitrary", "arbitrary")),
        name="adaln_modulation",
    )(cvec, w_mod, b_mod.reshape(depth, 1, d3))


def _inproj_kernel(x_ref, mod_ref, g_ref, w_ref, o_ref, h_s):
    @pl.when(pl.program_id(2) == 0)
    def _():
        x = x_ref[...]
        ms = jnp.mean(x * x, axis=-1, keepdims=True)
        y = x * lax.rsqrt(ms + EPS) * g_ref[...]
        h = y * (1.0 + mod_ref[1:2, :]) + mod_ref[0:1, :]
        h_s[...] = h.astype(BF16)

    o_ref[...] = jnp.dot(h_s[...], w_ref[...], preferred_element_type=F32).astype(o_ref.dtype)


def _in_projection(x, mod, gain, w_bf16, tm, tn):
    bsz, t, d = x.shape
    d_in = w_bf16.shape[1]
    tm = min(tm, t)
    tn = min(tn, d_in)
    return pl.pallas_call(
        _inproj_kernel,
        out_shape=jax.ShapeDtypeStruct((bsz, t, d_in), PROJ_DTYPE),
        grid=(bsz, t // tm, d_in // tn),
        in_specs=[
            pl.BlockSpec((None, tm, d), lambda b, i, j: (b, i, 0)),
            pl.BlockSpec((None, 3, d), lambda b, i, j: (b, 0, 0)),
            pl.BlockSpec((1, d), lambda b, i, j: (0, 0)),
            pl.BlockSpec((d, tn), lambda b, i, j: (0, j)),
        ],
        out_specs=pl.BlockSpec((None, tm, tn), lambda b, i, j: (b, i, j)),
        scratch_shapes=[pltpu.VMEM((tm, d), BF16)],
        compiler_params=_cparams(("arbitrary", "arbitrary", "arbitrary")),
        name="rmsnorm_adaln_inproj",
    )(x, mod, gain.reshape(1, d), w_bf16)


def _lru_kernel(xprev_ref, x_ref, xnext_ref, h0_ref, cw_ref, cb_ref, wa_ref, ba_ref, wx_ref,
                bx_ref, lam_ref, h_ref, hfin_ref, xpad, a_s, b_s, hstate):
    d = pl.program_id(0)
    i = pl.program_id(1)
    n = pl.num_programs(1)
    idx = jnp.where(d == 0, i, n - 1 - i)
    tp, bsz, ch = x_ref.shape

    @pl.when(i == 0)
    def _():
        hstate[...] = h0_ref[...]

    xpad[0:2] = jnp.where(idx > 0, xprev_ref[...], 0.0)
    xpad[2:2 + tp] = x_ref[...]
    xpad[2 + tp:3 + tp] = jnp.where(idx < n - 1, xnext_ref[...], 0.0)
    u = cb_ref[...] + cw_ref[0:1, :] * xpad[0:tp]
    for k in range(1, LRU_CONV):
        u = u + cw_ref[k:k + 1, :] * xpad[k:k + tp]
    u2 = u.reshape(tp * bsz, ch)
    ub = u2.astype(BF16)
    r = _sigmoid(jnp.dot(ub, wa_ref[...], preferred_element_type=F32) + ba_ref[...])
    gi = _sigmoid(jnp.dot(ub, wx_ref[...], preferred_element_type=F32) + bx_ref[...])
    nlam = -lam_ref[...]
    softplus = jnp.maximum(nlam, 0.0) + jnp.log(1.0 + jnp.exp(-jnp.abs(nlam)))
    log_a = (-LRU_C) * r * softplus
    a = jnp.exp(log_a)
    bb = jnp.sqrt(jnp.maximum(1.0 - a * a, TINY)) * (gi * u2)
    a_s[...] = a.reshape(tp, bsz, ch)
    b_s[...] = bb.reshape(tp, bsz, ch)

    def body(t, h):
        tt = jnp.where(d == 0, t, tp - 1 - t)
        h = a_s[tt] * h + b_s[tt]
        h_ref[tt] = h
        return h

    h = lax.fori_loop(0, tp, body, hstate[...], unroll=8)
    hstate[...] = h
    hfin_ref[...] = h


def _rglru(x_tm, h0, conv_w, conv_b, wa_dense, ba, wx_dense, bx, lam, tp):
    p, bsz, ch = x_tm.shape
    tp = min(tp, p)
    n = p // tp

    def tile(d, i):
        return jnp.where(d == 0, i, n - 1 - i)

    vec = lambda a: a.reshape(2, 1, ch)
    return pl.pallas_call(
        _lru_kernel,
        out_shape=(jax.ShapeDtypeStruct((2, p, bsz, ch), F32),
                   jax.ShapeDtypeStruct((2, bsz, ch), F32)),
        grid=(2, n),
        in_specs=[
            pl.BlockSpec((2, bsz, ch), lambda d, i: (jnp.maximum(tile(d, i) * (tp // 2) - 1, 0), 0, 0)),
            pl.BlockSpec((tp, bsz, ch), lambda d, i: (tile(d, i), 0, 0)),
            pl.BlockSpec((1, bsz, ch), lambda d, i: (jnp.minimum((tile(d, i) + 1) * tp, p - 1), 0, 0)),
            pl.BlockSpec((None, bsz, ch), lambda d, i: (d, 0, 0)),
            pl.BlockSpec((LRU_CONV, ch), lambda d, i: (0, 0)),
            pl.BlockSpec((1, ch), lambda d, i: (0, 0)),
            pl.BlockSpec((None, ch, ch), lambda d, i: (d, 0, 0)),
            pl.BlockSpec((None, 1, ch), lambda d, i: (d, 0, 0)),
            pl.BlockSpec((None, ch, ch), lambda d, i: (d, 0, 0)),
            pl.BlockSpec((None, 1, ch), lambda d, i: (d, 0, 0)),
            pl.BlockSpec((None, 1, ch), lambda d, i: (d, 0, 0)),
        ],
        out_specs=(
            pl.BlockSpec((None, tp, bsz, ch), lambda d, i: (d, tile(d, i), 0, 0)),
            pl.BlockSpec((None, bsz, ch), lambda d, i: (d, 0, 0)),
        ),
        scratch_shapes=[
            pltpu.VMEM((tp + LRU_CONV - 1, bsz, ch), F32),
            pltpu.VMEM((tp, bsz, ch), F32),
            pltpu.VMEM((tp, bsz, ch), F32),
            pltpu.VMEM((bsz, ch), F32),
        ],
        compiler_params=_cparams(("arbitrary", "arbitrary")),
        name="rglru_bidir_scan",
    )(x_tm, x_tm, x_tm, h0, conv_w, conv_b.reshape(1, ch), wa_dense, vec(ba), wx_dense, vec(bx),
      vec(lam))


def _hgrn2_kernel(q_ref, z_ref, v_ref, lb_ref, s0_ref, o_ref, sfin_ref, *states):
    d = pl.program_id(0)
    i = pl.program_id(2)
    tT = q_ref.shape[0]
    nh, dv, dk = states[0].shape
    c = HG_CHUNK
    nch = tT // c
    group = len(states) - 1

    @pl.when(i == 0)
    def _():
        states[0][...] = s0_ref[...]

    row = lax.broadcasted_iota(jnp.int32, (c, c), 0)
    col = lax.broadcasted_iota(jnp.int32, (c, c), 1)
    fwd = d == 0
    keep = (col - row) * jnp.where(fwd, 1, -1) <= 0
    tri = jnp.where(keep, 1.0, 0.0).astype(BF16)
    tri2 = jnp.concatenate([tri, tri], axis=1)
    nt_dims = (((1,), (1,)), ((), ()))
    tn_dims = (((0,), (0,)), ((), ()))

    def gates(z, lb):
        e = jnp.exp(-jnp.abs(z))
        s_big = 1.0 / (1.0 + e)
        sig = jnp.where(z >= 0, s_big, e * s_big)
        f = lb + (1.0 - lb) * sig
        return jnp.maximum(f, TINY), 1.0 - f

    heads = [(slice(h * dk, (h + 1) * dk), slice(h * dv, (h + 1) * dv)) for h in range(nh)]

    def chunk_base(j):
        return pl.multiple_of(jnp.where(fwd, j, nch - 1 - j) * c, c)

    def stage_a(rows):
        out = []
        for sl, _ in heads:
            f, kk = gates(z_ref[rows, sl].astype(F32), lb_ref[:, sl])
            g = jnp.log(f)
            g1 = g.astype(BF16)
            g2 = (g - g1.astype(F32)).astype(BF16)
            gc = jnp.dot(tri2, jnp.concatenate([g1, g2], axis=0), preferred_element_type=F32)
            out.append((kk, gc))
        return out

    def stage_b(rows, from_a, src, dst):
        out = []
        span = jnp.zeros((1, dk), F32)
        for h, ((sl, sv), (kk, gc)) in enumerate(zip(heads, from_a)):
            ref_row = gc[c // 2:c // 2 + 1, :]
            tot = jnp.where(fwd, gc[c - 1:c, :], gc[0:1, :])
            span = jnp.maximum(span, jnp.maximum(jnp.abs(gc[0:1, :] - ref_row),
                                                 jnp.abs(gc[c - 1:c, :] - ref_row)))
            e_q = jnp.exp(gc - ref_row)
            qp = q_ref[rows, sl].astype(F32) * e_q
            kp = kk * (1.0 / e_q)
            qpp = (qp * jnp.exp(ref_row)).astype(BF16)
            kpp = (kp * jnp.exp(tot - ref_row)).astype(BF16)
            sc = lax.dot_general(qp.astype(BF16), kp.astype(BF16), nt_dims, preferred_element_type=F32)
            s_t = src[h]
            inter = lax.dot_general(qpp, s_t.astype(BF16), nt_dims, preferred_element_type=F32)
            upd = lax.dot_general(v_ref[rows, sv].astype(BF16), kpp, tn_dims, preferred_element_type=F32)
            dst[h] = s_t * jnp.exp(tot) + upd
            out.append((sc, inter))
        return out, span

    def stage_c(rows, from_b):
        for (_, sv), (sc, inter) in zip(heads, from_b):
            sc = jnp.where(keep, sc, 0.0).astype(BF16)
            o_ref[rows, sv] = inter + jnp.dot(sc, v_ref[rows, sv].astype(BF16), preferred_element_type=F32)

    def exact_chunk(base, src, dst):
        sub = lax.broadcasted_iota(jnp.int32, (ROW_GROUP, 1), 0)
        dst[...] = src[...]

        def step(u, carry):
            t = base + jnp.where(fwd, u, c - 1 - u)
            grp = pl.ds(pl.multiple_of((t // ROW_GROUP) * ROW_GROUP, ROW_GROUP), ROW_GROUP)
            live = sub == t % ROW_GROUP
            for h, (sl, sv) in enumerate(heads):
                f_g, k_g = gates(z_ref[grp, sl].astype(F32), lb_ref[:, sl])
                f = jnp.sum(jnp.where(live, f_g, 0.0), axis=0, keepdims=True)
                v_g = jnp.where(live, v_ref[grp, sv].astype(F32), 0.0)
                s_new = dst[h] * f + lax.dot_general(v_g, k_g, tn_dims, preferred_element_type=F32)
                dst[h] = s_new
                o_g = lax.dot_general(q_ref[grp, sl].astype(F32), s_new, nt_dims,
                                      preferred_element_type=F32)
                o_ref[grp, sv] = jnp.where(live, o_g, o_ref[grp, sv])
            return carry

        lax.fori_loop(0, c, step, 0)

    def chunk_group(jj, carry):
        bases = [chunk_base(group * jj + g) for g in range(group)]
        rows = [pl.ds(b, c) for b in bases]
        span = jnp.zeros((1, dk), F32)
        from_a = stage_a(rows[0])
        for g in range(group):
            from_b, span_g = stage_b(rows[g], from_a, states[g], states[g + 1])
            span = jnp.maximum(span, span_g)
            if g + 1 < group:
                from_a = stage_a(rows[g + 1])
            stage_c(rows[g], from_b)

        @pl.when(jnp.max(span) > HG_SAFE_SPAN)
        def _():
            for g in range(group):
                exact_chunk(bases[g], states[g], states[g + 1])

        states[0][...] = states[group][...]
        return carry

    lax.fori_loop(0, nch // group, chunk_group, 0)
    sfin_ref[...] = states[0][...]


def _hgrn2(proj, q_blk, z_blk, v_blk, lb, s0, tT):
    bsz, t, _ = proj.shape
    _, _, nh, dv, dk = s0.shape
    w = nh * dk
    tT = min(tT, t)
    n = t // tT
    nch = tT // HG_CHUNK
    group = max(g for g in (1, 2, 4, HG_GROUP) if nch % g == 0)

    def tile(d, i):
        return jnp.where(d == 0, i, n - 1 - i)

    return pl.pallas_call(
        _hgrn2_kernel,
        out_shape=(jax.ShapeDtypeStruct((2, bsz, t, nh * dv), F32),
                   jax.ShapeDtypeStruct((2, bsz, nh, dv, dk), F32)),
        grid=(2, bsz, n),
        in_specs=[
            pl.BlockSpec((None, tT, w), lambda d, b, i: (b, tile(d, i), q_blk)),
            pl.BlockSpec((None, tT, w), lambda d, b, i: (b, tile(d, i), z_blk + d)),
            pl.BlockSpec((None, tT, w), lambda d, b, i: (b, tile(d, i), v_blk)),
            pl.BlockSpec((None, 1, w), lambda d, b, i: (d, 0, 0)),
            pl.BlockSpec((None, None, nh, dv, dk), lambda d, b, i: (d, b, 0, 0, 0)),
        ],
        out_specs=(
            pl.BlockSpec((None, None, tT, nh * dv), lambda d, b, i: (d, b, tile(d, i), 0)),
            pl.BlockSpec((None, None, nh, dv, dk), lambda d, b, i: (d, b, 0, 0, 0)),
        ),
        scratch_shapes=[pltpu.VMEM((nh, dv, dk), F32)] * (group + 1),
        compiler_params=_cparams(("arbitrary", "arbitrary", "arbitrary")),
        name="hgrn2_bidir_chunk_scan",
    )(proj, proj, proj, lb.reshape(2, 1, w), s0)


HY_MAX_RADIX = 8
HY_MIN_SUBLEN = 128
HY_CHUNK = 8


def _hyena_radix(length):
    return max(1, min(HY_MAX_RADIX, length // HY_MIN_SUBLEN))


def _hyena_tables(length):
    radix = _hyena_radix(length)
    n = length // radix
    k2 = 2 * jnp.arange(n, dtype=jnp.int32) + 1
    ph = (k2[:, None] * jnp.arange(n, dtype=jnp.int32)[None, :]) % (4 * n)
    psi = ph.astype(F32) * (2.0 * math.pi / (4 * n))
    fc, fs = jnp.cos(psi), jnp.sin(psi)
    fwd = jnp.concatenate([fc, fs], axis=0).astype(BF16)
    inv = jnp.concatenate([fc.T, -fs.T], axis=1).astype(BF16)
    phr = (jnp.arange(radix, dtype=jnp.int32)[:, None] * k2[None, :]) % (4 * length)
    phi = phr.astype(F32) * (2.0 * math.pi / (4 * length))
    rep = lambda a: jnp.broadcast_to(a[:, :, None], (radix, n, LANES))
    return fwd, inv, rep(jnp.cos(phi)), rep(jnp.sin(phi))


def _cmul_const(x, c, s):
    re, im = x
    if abs(s) < 1e-12:
        return (re, im) if c > 0 else (-re, -im)
    if abs(c) < 1e-12:
        return (-im, re) if s > 0 else (im, -re)
    return (re * c - im * s, re * s + im * c)


def _fft_list(xs, sign):
    r = len(xs)
    if r == 1:
        return xs
    ev = _fft_list(xs[0::2], sign)
    od = _fft_list(xs[1::2], sign)
    out = [None] * r
    for j in range(r // 2):
        ang = sign * 2.0 * math.pi * j / r
        t = _cmul_const(od[j], math.cos(ang), math.sin(ang))
        out[j] = (ev[j][0] + t[0], ev[j][1] + t[1])
        out[j + r // 2] = (ev[j][0] - t[0], ev[j][1] - t[1])
    return out


def _hy_sub_transforms(slab, fwd_ref, z_ref, radix, n):
    n_slabs = slab.shape[0]
    for r in range(radix):
        parts = [slab[s, pl.ds(r, n, stride=radix), :] if radix > 1 else slab[s]
                 for s in range(n_slabs)]
        xr = parts[0] if n_slabs == 1 else jnp.concatenate(parts, axis=1)
        z_ref[r] = jnp.dot(fwd_ref[...], xr.astype(BF16), preferred_element_type=F32)


def _hy_load_twiddles(twr_ref, twi_ref, rows_c, radix):
    return [(twr_ref[r, rows_c, :], twi_ref[r, rows_c, :]) for r in range(1, radix)]


def _hy_twiddled_inputs(z_ref, tw, rows_c, rows_s, lanes, radix):
    xs = []
    for r in range(radix):
        c = z_ref[r, rows_c, lanes]
        s = z_ref[r, rows_s, lanes]
        if r == 0:
            xs.append((c, -s))
        else:
            wr, wi = tw[r - 1]
            xs.append((c * wr - s * wi, -(c * wi + s * wr)))
    return xs


def _filter_kernel(emb_ref, w1_ref, b1_ref, w2_ref, b2_ref, w3_ref, fr_ref, dl_ref,
                   a_ref, b_ref, ss_ref, *, length):
    i = pl.program_id(0)
    tl = emb_ref.shape[0]
    cw = dl_ref.shape[1]
    fr = fr_ref[...]
    hi = lax.Precision.HIGHEST
    hdn = jnp.sin(fr * (jnp.dot(emb_ref[...], w1_ref[...], preferred_element_type=F32,
                                precision=hi) + b1_ref[...]))
    hdn = jnp.sin(fr * (jnp.dot(hdn, w2_ref[...], preferred_element_type=F32,
                                precision=hi) + b2_ref[...]))
    h = jnp.dot(hdn.astype(BF16), w3_ref[...].astype(BF16), preferred_element_type=F32)
    pos = (lax.broadcasted_iota(jnp.int32, (tl, 1), 0) + i * tl)
    t = pos.astype(F32) / float(length)
    win = jnp.exp(-t * dl_ref[...])
    not_first = pos > 0

    @pl.when(i == 0)
    def _():
        ss_ref[...] = jnp.zeros_like(ss_ref)

    for o in range(HY_ORDER):
        hpos = h[:, (2 * o) * cw:(2 * o + 1) * cw] * win
        hneg = jnp.where(not_first, h[:, (2 * o + 1) * cw:(2 * o + 2) * cw] * win, 0.0)
        a_ref[:, o * cw:(o + 1) * cw] = hpos + hneg
        b_ref[:, o * cw:(o + 1) * cw] = hneg - hpos
        ss_ref[:, o * cw:(o + 1) * cw] += jnp.sum(hpos * hpos + hneg * hneg, axis=0, keepdims=True)


def _spectrum_kernel(a_ref, b_ref, ss_ref, fwd_ref, twr_ref, twi_ref, hr_ref, hi_ref, slab, za, zb,
                     *, scale):
    radix, n, cb = hr_ref.shape
    for s in range(cb // LANES):
        slab[s] = a_ref[:, s * LANES:(s + 1) * LANES]
    _hy_sub_transforms(slab, fwd_ref, za, radix, n)
    for s in range(cb // LANES):
        slab[s] = b_ref[:, s * LANES:(s + 1) * LANES]
    _hy_sub_transforms(slab, fwd_ref, zb, radix, n)
    norm = lax.rsqrt(ss_ref[...] + TINY) * scale

    def step(i, carry):
        base = pl.multiple_of(i * HY_CHUNK, HY_CHUNK)
        rows_c = pl.ds(base, HY_CHUNK)
        rows_s = pl.ds(base + n, HY_CHUNK)
        tw = _hy_load_twiddles(twr_ref, twi_ref, rows_c, radix)
        for t in range(cb // LANES):
            lanes = slice(t * LANES, (t + 1) * LANES)
            fa = _fft_list(_hy_twiddled_inputs(za, tw, rows_c, rows_s, lanes, radix), -1)
            fb = _fft_list(_hy_twiddled_inputs(zb, tw, rows_c, rows_s, lanes, radix), -1)
            for j in range(radix):
                hr_ref[j, rows_c, lanes] = fa[j][0] * norm[:, lanes]
                hi_ref[j, rows_c, lanes] = -fb[j][1] * norm[:, lanes]
        return carry

    lax.fori_loop(0, n // HY_CHUNK, step, 0)


def _hyena_spectra(length, tables, w1, b1, w2, b2, w3, freq, width, tl):
    f32 = F32
    pos = jnp.arange(length, dtype=f32)
    t = pos / length
    bands = jnp.linspace(1e-4, HY_BANDS - 1, HY_BANDS, dtype=f32)
    ang = (2.0 * math.pi / length) * pos[:, None] * bands[None, :]
    emb = jnp.concatenate([t[:, None], jnp.cos(ang), jnp.sin(ang)], axis=-1)
    deltas = jnp.abs(jnp.linspace(math.log(HY_TARGET) / HY_SLOW_DECAY,
                                  math.log(HY_TARGET) / HY_FAST_DECAY, width, dtype=f32))
    n_emb = -(-emb.shape[1] // LANES) * LANES
    w1 = jnp.pad(w1.astype(f32), ((0, n_emb - emb.shape[1]), (0, 0)))
    emb = jnp.pad(emb, ((0, 0), (0, n_emb - emb.shape[1])))
    hid = w1.shape[1]
    cols = HY_ORDER * width
    tl = min(tl, length)
    full = lambda *s: pl.BlockSpec(s, lambda i: (0,) * len(s))
    a_un, b_un, ss = pl.pallas_call(
        functools.partial(_filter_kernel, length=length),
        out_shape=(jax.ShapeDtypeStruct((length, cols), f32),
                   jax.ShapeDtypeStruct((length, cols), f32),
                   jax.ShapeDtypeStruct((1, cols), f32)),
        grid=(length // tl,),
        in_specs=[
            pl.BlockSpec((tl, n_emb), lambda i: (i, 0)),
            full(n_emb, hid), full(1, hid), full(hid, hid), full(1, hid),
            full(hid, 2 * cols), full(1, hid), full(1, width),
        ],
        out_specs=(pl.BlockSpec((tl, cols), lambda i: (i, 0)),
                   pl.BlockSpec((tl, cols), lambda i: (i, 0)),
                   pl.BlockSpec((1, cols), lambda i: (0, 0))),
        compiler_params=_cparams(("arbitrary",)),
        name="hyena_filter_mlp",
    )(emb, w1, b1.reshape(1, hid), w2, b2.reshape(1, hid), w3, freq.reshape(1, hid),
      deltas.reshape(1, width))
    fwd, _, twr, twi = tables
    radix, n, _ = twr.shape
    cb = LANES
    return pl.pallas_call(
        functools.partial(_spectrum_kernel, scale=1.0 / length),
        out_shape=(jax.ShapeDtypeStruct((radix, n, cols), f32),
                   jax.ShapeDtypeStruct((radix, n, cols), f32)),
        grid=(cols // cb,),
        in_specs=[
            pl.BlockSpec((length, cb), lambda j: (0, j)),
            pl.BlockSpec((length, cb), lambda j: (0, j)),
            pl.BlockSpec((1, cb), lambda j: (0, j)),
            pl.BlockSpec((2 * n, n), lambda j: (0, 0)),
            pl.BlockSpec((radix, n, LANES), lambda j: (0, 0, 0)),
            pl.BlockSpec((radix, n, LANES), lambda j: (0, 0, 0)),
        ],
        out_specs=(pl.BlockSpec((radix, n, cb), lambda j: (0, 0, j)),
                   pl.BlockSpec((radix, n, cb), lambda j: (0, 0, j))),
        scratch_shapes=[pltpu.VMEM((cb // LANES, length, LANES), f32),
                        pltpu.VMEM((radix, 2 * n, cb), f32),
                        pltpu.VMEM((radix, 2 * n, cb), f32)],
        compiler_params=_cparams(("arbitrary",)),
        name="hyena_filter_spectrum",
    )(a_un, b_un, ss, fwd, twr, twi)


def _short_conv(x, w_ref, b_ref):
    length = x.shape[0]
    rows = lax.broadcasted_iota(jnp.int32, (length, 1), 0)
    prev = jnp.where(rows > 0, pltpu.roll(x, 1, axis=0), 0.0)
    nxt = jnp.where(rows < length - 1, pltpu.roll(x, length - 1, axis=0), 0.0)
    return w_ref[0:1, :] * prev + w_ref[1:2, :] * x + w_ref[2:3, :] * nxt + b_ref[...]


def _hyena_conv_kernel(*refs, conv_a, has_gate, m_blk, g_blk):
    it = iter(refs)
    a_ref = next(it)
    aw_ref = ab_ref = None
    if conv_a:
        aw_ref, ab_ref = next(it), next(it)
    m_hbm, mw_ref, mb_ref = next(it), next(it), next(it)
    g_hbm = next(it) if has_gate else None
    skip_ref, fwd_ref, inv_ref, twr_ref, twi_ref, hr_ref, hi_ref = (next(it) for _ in range(7))
    o_ref = next(it)
    slab, z_s = next(it), next(it)
    a_keep = next(it) if conv_a else None
    m_ref = next(it)
    g_ref = next(it) if has_gate else None
    sems = next(it)
    radix, n, cb = hr_ref.shape
    tiles = [slice(t * LANES, (t + 1) * LANES) for t in range(cb // LANES)]

    def side_copies():
        c, b = pl.program_id(0), pl.program_id(1)
        cps = [pltpu.make_async_copy(m_hbm.at[b, :, pl.ds((m_blk + c) * cb, cb)], m_ref, sems.at[0])]
        if has_gate:
            cps.append(pltpu.make_async_copy(g_hbm.at[b, :, pl.ds((g_blk + c) * cb, cb)], g_ref,
                                             sems.at[1]))
        return cps

    for cp in side_copies():
        cp.start()

    def a_value():
        a = a_ref[...].astype(F32)
        return _short_conv(a, aw_ref, ab_ref) if conv_a else a

    a = a_value()
    if conv_a:
        a_keep[...] = a
    for s, lanes in enumerate(tiles):
        slab[s] = a[:, lanes]
    _hy_sub_transforms(slab, fwd_ref, z_s, radix, n)

    def step(i, carry):
        base = pl.multiple_of(i * HY_CHUNK, HY_CHUNK)
        rows_c = pl.ds(base, HY_CHUNK)
        rows_s = pl.ds(base + n, HY_CHUNK)
        tw = _hy_load_twiddles(twr_ref, twi_ref, rows_c, radix)
        for lanes in tiles:
            zf = _fft_list(_hy_twiddled_inputs(z_s, tw, rows_c, rows_s, lanes, radix), -1)
            ys = []
            for j in range(radix):
                hr = hr_ref[j, rows_c, lanes]
                hi = hi_ref[j, rows_c, lanes]
                zr, zi = zf[j]
                ys.append((zr * hr - zi * hi, zr * hi + zi * hr))
            vs = _fft_list(ys, +1)
            for r in range(radix):
                ur, ui = vs[r]
                if r > 0:
                    wr, wi = tw[r - 1]
                    ur, ui = wr * ur - wi * ui, wr * ui + wi * ur
                z_s[r, rows_c, lanes] = ur
                z_s[r, rows_s, lanes] = ui
        return carry

    lax.fori_loop(0, n // HY_CHUNK, step, 0)

    for r in range(radix):
        yr = jnp.dot(inv_ref[...], z_s[r].astype(BF16), preferred_element_type=F32)
        for s, lanes in enumerate(tiles):
            if radix > 1:
                slab[s, pl.ds(r, n, stride=radix), :] = yr[:, lanes]
            else:
                slab[s] = yr[:, lanes]

    for cp in side_copies():
        cp.wait()
    a = a_keep[...] if conv_a else a_value()
    m = _short_conv(m_ref[...].astype(F32), mw_ref, mb_ref)
    for s, lanes in enumerate(tiles):
        y = m[:, lanes] * (slab[s] + skip_ref[:, lanes] * a[:, lanes])
        if has_gate:
            y = y * _silu(g_ref[:, lanes].astype(F32))
        o_ref[:, lanes] = y.astype(o_ref.dtype)


def _hyena_conv(a_src, a_blk, a_conv, m_src, m_blk, m_conv, g_src, g_blk, skip, tables, spectra,
                order, width, cb, out_dtype):
    bsz, length, _ = m_src.shape
    fwd, inv, twr, twi = tables
    hr, hi = spectra
    radix, n, _ = twr.shape
    nc = width // cb
    conv_a = a_conv is not None
    has_gate = g_src is not None
    once = dict(pipeline_mode=pl.Buffered(1))
    inputs, specs = [], []

    def add(arr, spec):
        inputs.append(arr)
        specs.append(spec)

    add(a_src, pl.BlockSpec((None, length, cb), lambda c, b: (b, 0, a_blk + c)))
    if conv_a:
        add(a_conv[0], pl.BlockSpec((HY_CONV, cb), lambda c, b: (0, a_conv[2] + c)))
        add(a_conv[1], pl.BlockSpec((1, cb), lambda c, b: (0, a_conv[2] + c)))
    add(m_src, pl.BlockSpec(memory_space=pl.ANY))
    add(m_conv[0], pl.BlockSpec((HY_CONV, cb), lambda c, b: (0, m_conv[2] + c)))
    add(m_conv[1], pl.BlockSpec((1, cb), lambda c, b: (0, m_conv[2] + c)))
    if has_gate:
        add(g_src, pl.BlockSpec(memory_space=pl.ANY))
    add(skip, pl.BlockSpec((None, 1, cb), lambda c, b: (order, 0, c)))
    add(fwd, pl.BlockSpec((2 * n, n), lambda c, b: (0, 0), **once))
    add(inv, pl.BlockSpec((n, 2 * n), lambda c, b: (0, 0), **once))
    add(twr, pl.BlockSpec((radix, n, LANES), lambda c, b: (0, 0, 0), **once))
    add(twi, pl.BlockSpec((radix, n, LANES), lambda c, b: (0, 0, 0), **once))
    add(hr, pl.BlockSpec((radix, n, cb), lambda c, b: (0, 0, order * nc + c), **once))
    add(hi, pl.BlockSpec((radix, n, cb), lambda c, b: (0, 0, order * nc + c), **once))
    side_bufs = [pltpu.VMEM((length, cb), m_src.dtype)] * (2 if has_gate else 1)
    return pl.pallas_call(
        functools.partial(_hyena_conv_kernel, conv_a=conv_a, has_gate=has_gate, m_blk=m_blk,
                          g_blk=g_blk),
        out_shape=jax.ShapeDtypeStruct((bsz, length, width), out_dtype),
        grid=(nc, bsz),
        in_specs=specs,
        out_specs=pl.BlockSpec((None, length, cb), lambda c, b: (b, 0, c)),
        scratch_shapes=[pltpu.VMEM((cb // LANES, length, LANES), F32),
                        pltpu.VMEM((radix, 2 * n, cb), F32)]
                       + ([pltpu.VMEM((length, cb), F32)] if conv_a else []) + side_bufs
                       + [pltpu.SemaphoreType.DMA((2,))],
        compiler_params=_cparams(("arbitrary", "arbitrary")),
        name="hyena_long_conv_o%d" % order,
    )(*inputs)


def _outproj_kernel(x_ref, ya_ref, ga_ref, yb_ref, of_ref, ob_ref, gc_ref, hgn_ref, w_ref, mod_ref,
                    fg_ref, o_ref, *, final, n_heads):
    wa = ya_ref.shape[1]
    wb = yb_ref.shape[1]
    wc = of_ref.shape[1]
    dv = wc // n_heads
    ya = (ya_ref[...] * _silu(ga_ref[...].astype(F32))).astype(BF16)
    acc = jnp.dot(ya, w_ref[0:wa, :], preferred_element_type=F32)
    acc = acc + jnp.dot(yb_ref[...].astype(BF16), w_ref[wa:wa + wb, :], preferred_element_type=F32)
    oc = of_ref[...] + ob_ref[...]
    gate_c = _silu(gc_ref[...].astype(F32)) * hgn_ref[...]
    for h in range(n_heads):
        sl = slice(h * dv, (h + 1) * dv)
        och = oc[:, sl]
        ms = jnp.mean(och * och, axis=-1, keepdims=True)
        ych = (och * lax.rsqrt(ms + EPS) * gate_c[:, sl]).astype(BF16)
        acc = acc + jnp.dot(ych, w_ref[wa + wb + h * dv:wa + wb + (h + 1) * dv, :],
                            preferred_element_type=F32)
    xn = x_ref[...] + mod_ref[2:3, :] * acc
    if final:
        ms = jnp.mean(xn * xn, axis=-1, keepdims=True)
        xn = xn * lax.rsqrt(ms + EPS) * fg_ref[...]
    o_ref[...] = xn


def _out_projection(x, ya, proj, ga_blk, yb, o_dirs, gc_blk, hg_norm, w_out_bf16, mod, final_g, final,
                    tm):
    bsz, t, d = x.shape
    wa = ya.shape[2]
    wb = yb.shape[2]
    wc = o_dirs.shape[3]
    tm = min(tm, t)
    return pl.pallas_call(
        functools.partial(_outproj_kernel, final=final, n_heads=HG_HEADS),
        out_shape=jax.ShapeDtypeStruct((bsz, t, d), F32),
        grid=(bsz, t // tm),
        in_specs=[
            pl.BlockSpec((None, tm, d), lambda b, i: (b, i, 0)),
            pl.BlockSpec((None, tm, wa), lambda b, i: (b, i, 0)),
            pl.BlockSpec((None, tm, wa), lambda b, i: (b, i, ga_blk)),
            pl.BlockSpec((None, tm, wb), lambda b, i: (b, i, 0)),
            pl.BlockSpec((None, None, tm, wc), lambda b, i: (0, b, i, 0)),
            pl.BlockSpec((None, None, tm, wc), lambda b, i: (1, b, i, 0)),
            pl.BlockSpec((None, tm, wc), lambda b, i: (b, i, gc_blk)),
            pl.BlockSpec((1, wc), lambda b, i: (0, 0)),
            pl.BlockSpec((wa + wb + wc, d), lambda b, i: (0, 0)),
            pl.BlockSpec((None, 3, d), lambda b, i: (b, 0, 0)),
            pl.BlockSpec((1, d), lambda b, i: (0, 0)),
        ],
        out_specs=pl.BlockSpec((None, tm, d), lambda b, i: (b, i, 0)),
        compiler_params=_cparams(("arbitrary", "arbitrary")),
        name="gate_outproj_residual",
    )(x, ya, proj, yb, o_dirs, o_dirs, proj, hg_norm.reshape(1, wc), w_out_bf16, mod,
      final_g.reshape(1, d))


def _block_diag(w):
    two, nh, n, _ = w.shape
    eye = jnp.eye(nh, dtype=w.dtype)
    dense = jnp.einsum("dhij,hg->dhigj", w, eye).reshape(two, nh * n, nh * n)
    return dense.astype(BF16)


def _mixers(x, mod, norm_g, lp, init, grid_rows, with_output):
    bsz, t, d = x.shape
    wl = lp["lru_w"]
    wh = lp["hy_w"]
    wg = lp["hg_w"]
    if grid_rows is None:
        proj = _in_projection(x.reshape(1, bsz * t, d), mod[:1], norm_g, lp["w_in"], tm=1024, tn=4096)
        proj = proj.reshape(bsz, t, -1)
    else:
        proj = _in_projection(x, mod, norm_g, lp["w_in"], tm=1024, tn=4096)
    xa = proj[:, :, :wl].astype(F32)
    if grid_rows is not None:
        x_tm = xa.reshape(bsz, grid_rows, GRID_W, wl).transpose(2, 1, 0, 3).reshape(t, bsz, wl)
    else:
        x_tm = xa.transpose(1, 0, 2)
    h_dirs, h_fin = _rglru(x_tm, init[0], lp["lru_conv_w"], lp["lru_conv_b"], lp["wa_dense"],
                           lp["lru_ba"], lp["wx_dense"], lp["lru_bx"], lp["lru_lam"], tp=128)
    q_off = 2 * wl + 4 * wh
    o_dirs, s_fin = _hgrn2(proj, q_off // wg, q_off // wg + 1, q_off // wg + 3, lp["hg_lb"], init[1],
                           tT=512)
    states = (h_fin, s_fin)
    if not with_output:
        return None, states
    ya_tm = h_dirs[0] + h_dirs[1]
    if grid_rows is not None:
        ya = ya_tm.reshape(GRID_W, grid_rows, bsz, wl).transpose(2, 1, 0, 3).reshape(bsz, t, wl)
    else:
        ya = ya_tm.transpose(1, 0, 2)
    tables = lp["dft"][t]
    spectra = _hyena_spectra(t, tables, lp["hy_w1"], lp["hy_b1"], lp["hy_w2"],
                             lp["hy_b2"], lp["hy_w3"], lp["hy_freq"], wh, tl=512)
    cb = min(wh, 256)
    ub_blk = (2 * wl) // cb
    nb = wh // cb
    cw, cbias = lp["hy_conv_w"], lp["hy_conv_b"].reshape(1, -1)
    z = _hyena_conv(proj, ub_blk, (cw, cbias, 0), proj, ub_blk + nb, (cw, cbias, nb), None, 0,
                    lp["hy_skip"], tables, spectra, 0, wh, cb, F32)
    yb = _hyena_conv(z, 0, None, proj, ub_blk + 2 * nb, (cw, cbias, 2 * nb), proj, ub_blk + 3 * nb,
                     lp["hy_skip"], tables, spectra, 1, wh, cb, BF16)
    return (ya, yb, o_dirs, proj), states


def kernel(x, c, ctx, c_ctx, norm_g, w_mod, b_mod, w_in, w_out, lru_conv_w, lru_conv_b, lru_wa, lru_ba,
           lru_wx, lru_bx, lru_lam, hy_conv_w, hy_conv_b, hy_w1, hy_b1, hy_w2, hy_b2, hy_w3, hy_freq,
           hy_skip, hg_lb_logits, hg_norm, final_g):
    bsz, seq, d = x.shape
    depth = w_mod.shape[0]
    ctx_len = ctx.shape[1]
    rows = seq // GRID_W
    wl = lru_lam.shape[-1]
    wh = hy_skip.shape[-1]
    wg = hg_norm.shape[-1]
    dk = wg // HG_HEADS

    p = jax.nn.softmax(hg_lb_logits.astype(F32), axis=0)
    lower_bounds = jnp.cumsum(p, axis=0) - p[0]

    n_rows = -(-(bsz + 1) // SUBLANES) * SUBLANES
    cvec = jnp.zeros((n_rows, d), F32).at[:bsz].set(c).at[bsz].set(c_ctx)
    mod_all = _modulation(cvec, w_mod, b_mod)

    dft = {length: _hyena_tables(length) for length in (seq, ctx_len)}

    zero_states = (jnp.zeros((2, bsz, wl), F32), jnp.zeros((2, bsz, HG_HEADS, dk, dk), F32))
    xc = ctx
    for layer in range(depth):
        last = layer == depth - 1
        mod = mod_all[layer, :bsz].reshape(bsz, 3, d)
        mod_c = jnp.broadcast_to(mod_all[layer, bsz].reshape(1, 3, d), (bsz, 3, d))
        lp = dict(
            w_in=w_in[layer].astype(BF16), lru_w=wl, hy_w=wh, hg_w=wg,
            lru_conv_w=lru_conv_w[layer], lru_conv_b=lru_conv_b[layer],
            wa_dense=_block_diag(lru_wa[layer]), wx_dense=_block_diag(lru_wx[layer]),
            lru_ba=lru_ba[layer], lru_bx=lru_bx[layer], lru_lam=lru_lam[layer],
            hy_conv_w=hy_conv_w[layer], hy_conv_b=hy_conv_b[layer], hy_w1=hy_w1[layer],
            hy_b1=hy_b1[layer], hy_w2=hy_w2[layer], hy_b2=hy_b2[layer], hy_w3=hy_w3[layer],
            hy_freq=hy_freq[layer], hy_skip=hy_skip[layer].reshape(HY_ORDER, 1, wh),
            hg_lb=lower_bounds[layer], dft=dft)
        w_out_l = w_out[layer].astype(BF16)
        ga_blk = 1
        gc_blk = (2 * wl + 4 * wh + 4 * wg) // wg
        pieces_c, ctx_states = _mixers(xc, mod_c, norm_g[layer], lp, zero_states, None, not last)
        pieces, _ = _mixers(x, mod, norm_g[layer], lp, ctx_states, rows, True)
        ya, yb, o_dirs, proj = pieces
        x = _out_projection(x, ya, proj, ga_blk, yb, o_dirs, gc_blk, hg_norm[layer], w_out_l, mod,
                            final_g, last, tm=512)
        if not last:
            ya, yb, o_dirs, proj = pieces_c
            xc = _out_projection(xc, ya, proj, ga_blk, yb, o_dirs, gc_blk, hg_norm[layer], w_out_l,
                                 mod_c, final_g, False, tm=512)
    return x
```

```python
import functools
import math

import numpy as np
import jax
import jax.numpy as jnp
from jax import lax
from jax.experimental import pallas as pl
from jax.experimental.pallas import tpu as pltpu

GRID_W = 64
LRU_HEADS = 8
LRU_CONV = 4
LRU_C = 8.0
HY_ORDER = 2
HY_CONV = 3
HY_BANDS = 16
HY_FAST_DECAY = 0.3
HY_SLOW_DECAY = 1.5
HY_TARGET = 1e-2
HG_HEADS = 8
HG_CHUNK = 64
HG_GROUP = 8
HG_SAFE_SPAN = 60.0
EPS = 1e-6
TINY = 1e-12

LANES = 128
SUBLANES = 8
ROW_GROUP = 16
VMEM_LIMIT_BYTES = 56 * 1024 * 1024

F32 = jnp.float32
BF16 = jnp.bfloat16
PROJ_DTYPE = BF16


def _cparams(semantics):
    return pltpu.CompilerParams(dimension_semantics=semantics, vmem_limit_bytes=VMEM_LIMIT_BYTES)


def _sigmoid(x):
    return 1.0 / (1.0 + jnp.exp(-x))


def _sigmoid_tanh(x):
    return 0.5 * jnp.tanh(0.5 * x) + 0.5


def _silu(x):
    return x * _sigmoid(x)


def _mod_kernel(c_ref, w_ref, b_ref, o_ref):
    s = _silu(c_ref[...])
    o_ref[...] = jnp.dot(s, w_ref[...], preferred_element_type=F32,
                         precision=lax.Precision.HIGHEST) + b_ref[...]


def _modulation(cvec, w_mod, b_mod):
    depth, d, d3 = w_mod.shape
    r = cvec.shape[0]
    return pl.pallas_call(
        _mod_kernel,
        out_shape=jax.ShapeDtypeStruct((depth, r, d3), F32),
        grid=(depth, d3 // d),
        in_specs=[
            pl.BlockSpec((r, d), lambda l, j: (0, 0)),
            pl.BlockSpec((None, d, d), lambda l, j: (l, 0, j)),
            pl.BlockSpec((None, 1, d), lambda l, j: (l, 0, j)),
        ],
        out_specs=pl.BlockSpec((None, r, d), lambda l, j: (l, 0, j)),
        compiler_params=_cparams(("arbitrary", "arbitrary")),
        name="adaln_modulation",
    )(cvec, w_mod, b_mod.reshape(depth, 1, d3))


def _inproj_kernel(x_ref, mod_ref, g_ref, w_ref, o_ref, h_s):
    @pl.when(pl.program_id(2) == 0)
    def _():
        x = x_ref[...]
        ms = jnp.mean(x * x, axis=-1, keepdims=True)
        y = x * lax.rsqrt(ms + EPS) * g_ref[...]
        h = y * (1.0 + mod_ref[1:2, :]) + mod_ref[0:1, :]
        h_s[...] = h.astype(BF16)

    o_ref[...] = jnp.dot(h_s[...], w_ref[...], preferred_element_type=F32).astype(o_ref.dtype)


def _in_projection(x, mod, gain, w_bf16, tm, tn):
    bsz, t, d = x.shape
    d_in = w_bf16.shape[1]
    tm = min(tm, t)
    tn = min(tn, d_in)
    return pl.pallas_call(
        _inproj_kernel,
        out_shape=jax.ShapeDtypeStruct((bsz, t, d_in), PROJ_DTYPE),
        grid=(bsz, t // tm, d_in // tn),
        in_specs=[
            pl.BlockSpec((None, tm, d), lambda b, i, j: (b, i, 0)),
            pl.BlockSpec((None, 3, d), lambda b, i, j: (b, 0, 0)),
            pl.BlockSpec((1, d), lambda b, i, j: (0, 0)),
            pl.BlockSpec((d, tn), lambda b, i, j: (0, j)),
        ],
        out_specs=pl.BlockSpec((None, tm, tn), lambda b, i, j: (b, i, j)),
        scratch_shapes=[pltpu.VMEM((tm, d), BF16)],
        compiler_params=_cparams(("arbitrary", "arbitrary", "arbitrary")),
        name="rmsnorm_adaln_inproj",
    )(x, mod, gain.reshape(1, d), w_bf16)


def _lru_kernel(xprev_ref, x_ref, xnext_ref, h0_ref, cw_ref, cb_ref, wa_ref, ba_ref, wx_ref,
                bx_ref, lam_ref, h_ref, hfin_ref, xpad, a_s, b_s, hstate):
    d = pl.program_id(0)
    i = pl.program_id(1)
    n = pl.num_programs(1)
    idx = jnp.where(d == 0, i, n - 1 - i)
    tp, bsz, ch = x_ref.shape

    @pl.when(i == 0)
    def _():
        hstate[...] = h0_ref[...]

    xpad[0:2] = jnp.where(idx > 0, xprev_ref[...], 0.0)
    xpad[2:2 + tp] = x_ref[...]
    xpad[2 + tp:3 + tp] = jnp.where(idx < n - 1, xnext_ref[...], 0.0)
    u = cb_ref[...] + cw_ref[0:1, :] * xpad[0:tp]
    for k in range(1, LRU_CONV):
        u = u + cw_ref[k:k + 1, :] * xpad[k:k + tp]
    u2 = u.reshape(tp * bsz, ch)
    ub = u2.astype(BF16)
    r = _sigmoid_tanh(jnp.dot(ub, wa_ref[...], preferred_element_type=F32) + ba_ref[...])
    gi = _sigmoid_tanh(jnp.dot(ub, wx_ref[...], preferred_element_type=F32) + bx_ref[...])
    nlam = -lam_ref[...]
    softplus = jnp.maximum(nlam, 0.0) + jnp.log(1.0 + jnp.exp(-jnp.abs(nlam)))
    a = jnp.exp(r * ((-LRU_C) * softplus))
    var = jnp.maximum(1.0 - a * a, TINY)
    bb = (var * lax.rsqrt(var)) * (gi * u2)
    a_s[...] = a.reshape(tp, bsz, ch)
    b_s[...] = bb.reshape(tp, bsz, ch)

    def body(t, h):
        tt = jnp.where(d == 0, t, tp - 1 - t)
        h = a_s[tt] * h + b_s[tt]
        h_ref[tt] = h
        return h

    h = lax.fori_loop(0, tp, body, hstate[...], unroll=8)
    hstate[...] = h
    hfin_ref[...] = h


def _rglru(x_tm, h0, conv_w, conv_b, wa_dense, ba, wx_dense, bx, lam, tp):
    p, bsz, ch = x_tm.shape
    tp = min(tp, p)
    n = p // tp

    def tile(d, i):
        return jnp.where(d == 0, i, n - 1 - i)

    vec = lambda a: a.reshape(2, 1, ch)
    return pl.pallas_call(
        _lru_kernel,
        out_shape=(jax.ShapeDtypeStruct((2, p, bsz, ch), F32),
                   jax.ShapeDtypeStruct((2, bsz, ch), F32)),
        grid=(2, n),
        in_specs=[
            pl.BlockSpec((2, bsz, ch), lambda d, i: (jnp.maximum(tile(d, i) * (tp // 2) - 1, 0), 0, 0)),
            pl.BlockSpec((tp, bsz, ch), lambda d, i: (tile(d, i), 0, 0)),
            pl.BlockSpec((1, bsz, ch), lambda d, i: (jnp.minimum((tile(d, i) + 1) * tp, p - 1), 0, 0)),
            pl.BlockSpec((None, bsz, ch), lambda d, i: (d, 0, 0)),
            pl.BlockSpec((LRU_CONV, ch), lambda d, i: (0, 0)),
            pl.BlockSpec((1, ch), lambda d, i: (0, 0)),
            pl.BlockSpec((None, ch, ch), lambda d, i: (d, 0, 0)),
            pl.BlockSpec((None, 1, ch), lambda d, i: (d, 0, 0)),
            pl.BlockSpec((None, ch, ch), lambda d, i: (d, 0, 0)),
            pl.BlockSpec((None, 1, ch), lambda d, i: (d, 0, 0)),
            pl.BlockSpec((None, 1, ch), lambda d, i: (d, 0, 0)),
        ],
        out_specs=(
            pl.BlockSpec((None, tp, bsz, ch), lambda d, i: (d, tile(d, i), 0, 0)),
            pl.BlockSpec((None, bsz, ch), lambda d, i: (d, 0, 0)),
        ),
        scratch_shapes=[
            pltpu.VMEM((tp + LRU_CONV - 1, bsz, ch), F32),
            pltpu.VMEM((tp, bsz, ch), F32),
            pltpu.VMEM((tp, bsz, ch), F32),
            pltpu.VMEM((bsz, ch), F32),
        ],
        compiler_params=_cparams(("arbitrary", "arbitrary")),
        name="rglru_bidir_scan",
    )(x_tm, x_tm, x_tm, h0, conv_w, conv_b.reshape(1, ch), wa_dense, vec(ba), wx_dense, vec(bx),
      vec(lam))


def _hgrn2_kernel(q_ref, z_ref, v_ref, lb_ref, s0_ref, o_ref, sfin_ref, *states):
    d = pl.program_id(0)
    i = pl.program_id(2)
    tT = q_ref.shape[0]
    nh, dv, dk = states[0].shape
    c = HG_CHUNK
    nch = tT // c
    group = len(states) - 1

    @pl.when(i == 0)
    def _():
        states[0][...] = s0_ref[...]

    row = lax.broadcasted_iota(jnp.int32, (c, c), 0)
    col = lax.broadcasted_iota(jnp.int32, (c, c), 1)
    fwd = d == 0
    keep = (col - row) * jnp.where(fwd, 1, -1) <= 0
    tri = jnp.where(keep, 1.0, 0.0).astype(BF16)
    tri2 = jnp.concatenate([tri, tri], axis=1)
    nt_dims = (((1,), (1,)), ((), ()))
    tn_dims = (((0,), (0,)), ((), ()))

    def gates(z, lb):
        e = jnp.exp(-jnp.abs(z))
        s_big = 1.0 / (1.0 + e)
        sig = jnp.where(z >= 0, s_big, e * s_big)
        f = lb + (1.0 - lb) * sig
        return jnp.maximum(f, TINY), 1.0 - f

    heads = [(slice(h * dk, (h + 1) * dk), slice(h * dv, (h + 1) * dv)) for h in range(nh)]

    def chunk_base(j):
        return pl.multiple_of(jnp.where(fwd, j, nch - 1 - j) * c, c)

    def stage_a(rows):
        out = []
        for sl, _ in heads:
            f, kk = gates(z_ref[rows, sl].astype(F32), lb_ref[:, sl])
            g = jnp.log(f)
            g1 = g.astype(BF16)
            g2 = (g - g1.astype(F32)).astype(BF16)
            gc = jnp.dot(tri2, jnp.concatenate([g1, g2], axis=0), preferred_element_type=F32)
            out.append((kk, gc))
        return out

    def stage_b(rows, from_a, src, dst):
        out = []
        span = jnp.zeros((1, dk), F32)
        for h, ((sl, sv), (kk, gc)) in enumerate(zip(heads, from_a)):
            ref_row = gc[c // 2:c // 2 + 1, :]
            tot = jnp.where(fwd, gc[c - 1:c, :], gc[0:1, :])
            span = jnp.maximum(span, jnp.maximum(jnp.abs(gc[0:1, :] - ref_row),
                                                 jnp.abs(gc[c - 1:c, :] - ref_row)))
            e_q = jnp.exp(gc - ref_row)
            qp = q_ref[rows, sl].astype(F32) * e_q
            kp = kk * (1.0 / e_q)
            qpp = (qp * jnp.exp(ref_row)).astype(BF16)
            kpp = (kp * jnp.exp(tot - ref_row)).astype(BF16)
            sc = lax.dot_general(qp.astype(BF16), kp.astype(BF16), nt_dims, preferred_element_type=F32)
            s_t = src[h]
            inter = lax.dot_general(qpp, s_t.astype(BF16), nt_dims, preferred_element_type=F32)
            upd = lax.dot_general(v_ref[rows, sv].astype(BF16), kpp, tn_dims, preferred_element_type=F32)
            dst[h] = s_t * jnp.exp(tot) + upd
            out.append((sc, inter))
        return out, span

    def stage_c(rows, from_b):
        for (_, sv), (sc, inter) in zip(heads, from_b):
            sc = jnp.where(keep, sc, 0.0).astype(BF16)
            o_ref[rows, sv] = inter + jnp.dot(sc, v_ref[rows, sv].astype(BF16), preferred_element_type=F32)

    def exact_chunk(base, src, dst):
        sub = lax.broadcasted_iota(jnp.int32, (ROW_GROUP, 1), 0)
        dst[...] = src[...]

        def step(u, carry):
            t = base + jnp.where(fwd, u, c - 1 - u)
            grp = pl.ds(pl.multiple_of((t // ROW_GROUP) * ROW_GROUP, ROW_GROUP), ROW_GROUP)
            live = sub == t % ROW_GROUP
            for h, (sl, sv) in enumerate(heads):
                f_g, k_g = gates(z_ref[grp, sl].astype(F32), lb_ref[:, sl])
                f = jnp.sum(jnp.where(live, f_g, 0.0), axis=0, keepdims=True)
                v_g = jnp.where(live, v_ref[grp, sv].astype(F32), 0.0)
                s_new = dst[h] * f + lax.dot_general(v_g, k_g, tn_dims, preferred_element_type=F32)
                dst[h] = s_new
                o_g = lax.dot_general(q_ref[grp, sl].astype(F32), s_new, nt_dims,
                                      preferred_element_type=F32)
                o_ref[grp, sv] = jnp.where(live, o_g, o_ref[grp, sv])
            return carry

        lax.fori_loop(0, c, step, 0)

    def chunk_group(jj, carry):
        bases = [chunk_base(group * jj + g) for g in range(group)]
        rows = [pl.ds(b, c) for b in bases]
        span = jnp.zeros((1, dk), F32)
        from_a = stage_a(rows[0])
        for g in range(group):
            from_b, span_g = stage_b(rows[g], from_a, states[g], states[g + 1])
            span = jnp.maximum(span, span_g)
            if g + 1 < group:
                from_a = stage_a(rows[g + 1])
            stage_c(rows[g], from_b)

        @pl.when(jnp.max(span) > HG_SAFE_SPAN)
        def _():
            for g in range(group):
                exact_chunk(bases[g], states[g], states[g + 1])

        states[0][...] = states[group][...]
        return carry

    lax.fori_loop(0, nch // group, chunk_group, 0)
    sfin_ref[...] = states[0][...]


def _hgrn2(proj, q_blk, z_blk, v_blk, lb, s0, tT):
    bsz, t, _ = proj.shape
    _, _, nh, dv, dk = s0.shape
    w = nh * dk
    tT = min(tT, t)
    n = t // tT
    nch = tT // HG_CHUNK
    group = max(g for g in (1, 2, 4, HG_GROUP) if nch % g == 0)

    def tile(d, i):
        return jnp.where(d == 0, i, n - 1 - i)

    return pl.pallas_call(
        _hgrn2_kernel,
        out_shape=(jax.ShapeDtypeStruct((2, bsz, t, nh * dv), F32),
                   jax.ShapeDtypeStruct((2, bsz, nh, dv, dk), F32)),
        grid=(2, bsz, n),
        in_specs=[
            pl.BlockSpec((None, tT, w), lambda d, b, i: (b, tile(d, i), q_blk)),
            pl.BlockSpec((None, tT, w), lambda d, b, i: (b, tile(d, i), z_blk + d)),
            pl.BlockSpec((None, tT, w), lambda d, b, i: (b, tile(d, i), v_blk)),
            pl.BlockSpec((None, 1, w), lambda d, b, i: (d, 0, 0)),
            pl.BlockSpec((None, None, nh, dv, dk), lambda d, b, i: (d, b, 0, 0, 0)),
        ],
        out_specs=(
            pl.BlockSpec((None, None, tT, nh * dv), lambda d, b, i: (d, b, tile(d, i), 0)),
            pl.BlockSpec((None, None, nh, dv, dk), lambda d, b, i: (d, b, 0, 0, 0)),
        ),
        scratch_shapes=[pltpu.VMEM((nh, dv, dk), F32)] * (group + 1),
        compiler_params=_cparams(("arbitrary", "arbitrary", "arbitrary")),
        name="hgrn2_bidir_chunk_scan",
    )(proj, proj, proj, lb.reshape(2, 1, w), s0)


HY_MAX_RADIX = 8
HY_MIN_SUBLEN = 128
HY_CHUNK = 8


def _hyena_radix(length):
    return max(1, min(HY_MAX_RADIX, length // HY_MIN_SUBLEN))


def _hyena_tables(length):
    radix = _hyena_radix(length)
    n = length // radix
    k2 = 2 * jnp.arange(n, dtype=jnp.int32) + 1
    ph = (k2[:, None] * jnp.arange(n, dtype=jnp.int32)[None, :]) % (4 * n)
    psi = ph.astype(F32) * (2.0 * math.pi / (4 * n))
    fc, fs = jnp.cos(psi), jnp.sin(psi)
    fwd = jnp.concatenate([fc, fs], axis=0).astype(BF16)
    inv = jnp.concatenate([fc.T, -fs.T], axis=1).astype(BF16)
    phr = (jnp.arange(radix, dtype=jnp.int32)[:, None] * k2[None, :]) % (4 * length)
    phi = phr.astype(F32) * (2.0 * math.pi / (4 * length))
    rep = lambda a: jnp.broadcast_to(a[:, :, None], (radix, n, LANES))
    return fwd, inv, rep(jnp.cos(phi)), rep(jnp.sin(phi))


def _cmul_const(x, c, s):
    re, im = x
    if abs(s) < 1e-12:
        return (re, im) if c > 0 else (-re, -im)
    if abs(c) < 1e-12:
        return (-im, re) if s > 0 else (im, -re)
    return (re * c - im * s, re * s + im * c)


def _fft_list(xs, sign):
    r = len(xs)
    if r == 1:
        return xs
    ev = _fft_list(xs[0::2], sign)
    od = _fft_list(xs[1::2], sign)
    out = [None] * r
    for j in range(r // 2):
        ang = sign * 2.0 * math.pi * j / r
        t = _cmul_const(od[j], math.cos(ang), math.sin(ang))
        out[j] = (ev[j][0] + t[0], ev[j][1] + t[1])
        out[j + r // 2] = (ev[j][0] - t[0], ev[j][1] - t[1])
    return out


def _hy_sub_transforms(slab, fwd_ref, z_ref, radix, n):
    n_slabs = slab.shape[0]
    for r in range(radix):
        parts = [slab[s, pl.ds(r, n, stride=radix), :] if radix > 1 else slab[s]
                 for s in range(n_slabs)]
        xr = parts[0] if n_slabs == 1 else jnp.concatenate(parts, axis=1)
        z_ref[r] = jnp.dot(fwd_ref[...], xr.astype(BF16), preferred_element_type=F32)


def _hy_load_twiddles(twr_ref, twi_ref, rows_c, radix):
    return [(twr_ref[r, rows_c, :], twi_ref[r, rows_c, :]) for r in range(1, radix)]


def _hy_twiddled_inputs(z_ref, tw, rows_c, rows_s, lanes, radix):
    xs = []
    for r in range(radix):
        c = z_ref[r, rows_c, lanes]
        s = z_ref[r, rows_s, lanes]
        if r == 0:
            xs.append((c, -s))
        else:
            wr, wi = tw[r - 1]
            xs.append((c * wr - s * wi, -(c * wi + s * wr)))
    return xs


def _filter_kernel(emb_ref, w1_ref, b1_ref, w2_ref, b2_ref, w3_ref, fr_ref, dl_ref,
                   a_ref, b_ref, ss_ref, *, length):
    i = pl.program_id(0)
    tl = emb_ref.shape[0]
    cw = dl_ref.shape[1]
    fr = fr_ref[...]
    hi = lax.Precision.HIGHEST
    hdn = jnp.sin(fr * (jnp.dot(emb_ref[...], w1_ref[...], preferred_element_type=F32,
                                precision=hi) + b1_ref[...]))
    hdn = jnp.sin(fr * (jnp.dot(hdn, w2_ref[...], preferred_element_type=F32,
                                precision=hi) + b2_ref[...]))
    h = jnp.dot(hdn.astype(BF16), w3_ref[...].astype(BF16), preferred_element_type=F32)
    pos = (lax.broadcasted_iota(jnp.int32, (tl, 1), 0) + i * tl)
    t = pos.astype(F32) / float(length)
    win = jnp.exp(-t * dl_ref[...])
    not_first = pos > 0

    @pl.when(i == 0)
    def _():
        ss_ref[...] = jnp.zeros_like(ss_ref)

    for o in range(HY_ORDER):
        hpos = h[:, (2 * o) * cw:(2 * o + 1) * cw] * win
        hneg = jnp.where(not_first, h[:, (2 * o + 1) * cw:(2 * o + 2) * cw] * win, 0.0)
        a_ref[:, o * cw:(o + 1) * cw] = hpos + hneg
        b_ref[:, o * cw:(o + 1) * cw] = hneg - hpos
        ss_ref[:, o * cw:(o + 1) * cw] += jnp.sum(hpos * hpos + hneg * hneg, axis=0, keepdims=True)


def _spectrum_kernel(a_ref, b_ref, ss_ref, fwd_ref, twr_ref, twi_ref, hr_ref, hi_ref, slab, za, zb,
                     *, scale):
    radix, n, cb = hr_ref.shape
    for s in range(cb // LANES):
        slab[s] = a_ref[:, s * LANES:(s + 1) * LANES]
    _hy_sub_transforms(slab, fwd_ref, za, radix, n)
    for s in range(cb // LANES):
        slab[s] = b_ref[:, s * LANES:(s + 1) * LANES]
    _hy_sub_transforms(slab, fwd_ref, zb, radix, n)
    norm = lax.rsqrt(ss_ref[...] + TINY) * scale

    def step(i, carry):
        base = pl.multiple_of(i * HY_CHUNK, HY_CHUNK)
        rows_c = pl.ds(base, HY_CHUNK)
        rows_s = pl.ds(base + n, HY_CHUNK)
        tw = _hy_load_twiddles(twr_ref, twi_ref, rows_c, radix)
        for t in range(cb // LANES):
            lanes = slice(t * LANES, (t + 1) * LANES)
            fa = _fft_list(_hy_twiddled_inputs(za, tw, rows_c, rows_s, lanes, radix), -1)
            fb = _fft_list(_hy_twiddled_inputs(zb, tw, rows_c, rows_s, lanes, radix), -1)
            for j in range(radix):
                hr_ref[j, rows_c, lanes] = fa[j][0] * norm[:, lanes]
                hi_ref[j, rows_c, lanes] = -fb[j][1] * norm[:, lanes]
        return carry

    lax.fori_loop(0, n // HY_CHUNK, step, 0)


def _hyena_spectra(length, tables, w1, b1, w2, b2, w3, freq, width, tl):
    f32 = F32
    pos = jnp.arange(length, dtype=f32)
    t = pos / length
    bands = jnp.linspace(1e-4, HY_BANDS - 1, HY_BANDS, dtype=f32)
    ang = (2.0 * math.pi / length) * pos[:, None] * bands[None, :]
    emb = jnp.concatenate([t[:, None], jnp.cos(ang), jnp.sin(ang)], axis=-1)
    deltas = jnp.abs(jnp.linspace(math.log(HY_TARGET) / HY_SLOW_DECAY,
                                  math.log(HY_TARGET) / HY_FAST_DECAY, width, dtype=f32))
    n_emb = -(-emb.shape[1] // LANES) * LANES
    w1 = jnp.pad(w1.astype(f32), ((0, n_emb - emb.shape[1]), (0, 0)))
    emb = jnp.pad(emb, ((0, 0), (0, n_emb - emb.shape[1])))
    hid = w1.shape[1]
    cols = HY_ORDER * width
    tl = min(tl, length)
    full = lambda *s: pl.BlockSpec(s, lambda i: (0,) * len(s))
    a_un, b_un, ss = pl.pallas_call(
        functools.partial(_filter_kernel, length=length),
        out_shape=(jax.ShapeDtypeStruct((length, cols), f32),
                   jax.ShapeDtypeStruct((length, cols), f32),
                   jax.ShapeDtypeStruct((1, cols), f32)),
        grid=(length // tl,),
        in_specs=[
            pl.BlockSpec((tl, n_emb), lambda i: (i, 0)),
            full(n_emb, hid), full(1, hid), full(hid, hid), full(1, hid),
            full(hid, 2 * cols), full(1, hid), full(1, width),
        ],
        out_specs=(pl.BlockSpec((tl, cols), lambda i: (i, 0)),
                   pl.BlockSpec((tl, cols), lambda i: (i, 0)),
                   pl.BlockSpec((1, cols), lambda i: (0, 0))),
        compiler_params=_cparams(("arbitrary",)),
        name="hyena_filter_mlp",
    )(emb, w1, b1.reshape(1, hid), w2, b2.reshape(1, hid), w3, freq.reshape(1, hid),
      deltas.reshape(1, width))
    fwd, _, twr, twi = tables
    radix, n, _ = twr.shape
    cb = LANES
    return pl.pallas_call(
        functools.partial(_spectrum_kernel, scale=1.0 / length),
        out_shape=(jax.ShapeDtypeStruct((radix, n, cols), f32),
                   jax.ShapeDtypeStruct((radix, n, cols), f32)),
        grid=(cols // cb,),
        in_specs=[
            pl.BlockSpec((length, cb), lambda j: (0, j)),
            pl.BlockSpec((length, cb), lambda j: (0, j)),
            pl.BlockSpec((1, cb), lambda j: (0, j)),
            pl.BlockSpec((2 * n, n), lambda j: (0, 0)),
            pl.BlockSpec((radix, n, LANES), lambda j: (0, 0, 0)),
            pl.BlockSpec((radix, n, LANES), lambda j: (0, 0, 0)),
        ],
        out_specs=(pl.BlockSpec((radix, n, cb), lambda j: (0, 0, j)),
                   pl.BlockSpec((radix, n, cb), lambda j: (0, 0, j))),
        scratch_shapes=[pltpu.VMEM((cb // LANES, length, LANES), f32),
                        pltpu.VMEM((radix, 2 * n, cb), f32),
                        pltpu.VMEM((radix, 2 * n, cb), f32)],
        compiler_params=_cparams(("arbitrary",)),
        name="hyena_filter_spectrum",
    )(a_un, b_un, ss, fwd, twr, twi)


def _short_conv(x, w_ref, b_ref):
    length = x.shape[0]
    rows = lax.broadcasted_iota(jnp.int32, (length, 1), 0)
    prev = jnp.where(rows > 0, pltpu.roll(x, 1, axis=0), 0.0)
    nxt = jnp.where(rows < length - 1, pltpu.roll(x, length - 1, axis=0), 0.0)
    return w_ref[0:1, :] * prev + w_ref[1:2, :] * x + w_ref[2:3, :] * nxt + b_ref[...]


def _hyena_conv_kernel(*refs, conv_a, has_gate, m_blk, g_blk):
    it = iter(refs)
    a_ref = next(it)
    aw_ref = ab_ref = None
    if conv_a:
        aw_ref, ab_ref = next(it), next(it)
    m_hbm, mw_ref, mb_ref = next(it), next(it), next(it)
    g_hbm = next(it) if has_gate else None
    skip_ref, fwd_ref, inv_ref, twr_ref, twi_ref, hr_ref, hi_ref = (next(it) for _ in range(7))
    o_ref = next(it)
    slab, z_s = next(it), next(it)
    a_keep = next(it) if conv_a else None
    m_ref = next(it)
    g_ref = next(it) if has_gate else None
    sems = next(it)
    radix, n, cb = hr_ref.shape
    tiles = [slice(t * LANES, (t + 1) * LANES) for t in range(cb // LANES)]

    def side_copies():
        c, b = pl.program_id(0), pl.program_id(1)
        cps = [pltpu.make_async_copy(m_hbm.at[b, :, pl.ds((m_blk + c) * cb, cb)], m_ref, sems.at[0])]
        if has_gate:
            cps.append(pltpu.make_async_copy(g_hbm.at[b, :, pl.ds((g_blk + c) * cb, cb)], g_ref,
                                             sems.at[1]))
        return cps

    for cp in side_copies():
        cp.start()

    def a_value():
        a = a_ref[...].astype(F32)
        return _short_conv(a, aw_ref, ab_ref) if conv_a else a

    a = a_value()
    if conv_a:
        a_keep[...] = a
    for s, lanes in enumerate(tiles):
        slab[s] = a[:, lanes]
    _hy_sub_transforms(slab, fwd_ref, z_s, radix, n)

    def step(i, carry):
        base = pl.multiple_of(i * HY_CHUNK, HY_CHUNK)
        rows_c = pl.ds(base, HY_CHUNK)
        rows_s = pl.ds(base + n, HY_CHUNK)
        tw = _hy_load_twiddles(twr_ref, twi_ref, rows_c, radix)
        for lanes in tiles:
            zf = _fft_list(_hy_twiddled_inputs(z_s, tw, rows_c, rows_s, lanes, radix), -1)
            ys = []
            for j in range(radix):
                hr = hr_ref[j, rows_c, lanes]
                hi = hi_ref[j, rows_c, lanes]
                zr, zi = zf[j]
                ys.append((zr * hr - zi * hi, zr * hi + zi * hr))
            vs = _fft_list(ys, +1)
            for r in range(radix):
                ur, ui = vs[r]
                if r > 0:
                    wr, wi = tw[r - 1]
                    ur, ui = wr * ur - wi * ui, wr * ui + wi * ur
                z_s[r, rows_c, lanes] = ur
                z_s[r, rows_s, lanes] = ui
        return carry

    lax.fori_loop(0, n // HY_CHUNK, step, 0)

    for r in range(radix):
        yr = jnp.dot(inv_ref[...], z_s[r].astype(BF16), preferred_element_type=F32)
        for s, lanes in enumerate(tiles):
            if radix > 1:
                slab[s, pl.ds(r, n, stride=radix), :] = yr[:, lanes]
            else:
                slab[s] = yr[:, lanes]

    for cp in side_copies():
        cp.wait()
    a = a_keep[...] if conv_a else a_value()
    m = _short_conv(m_ref[...].astype(F32), mw_ref, mb_ref)
    for s, lanes in enumerate(tiles):
        y = m[:, lanes] * (slab[s] + skip_ref[:, lanes] * a[:, lanes])
        if has_gate:
            y = y * _silu(g_ref[:, lanes].astype(F32))
        o_ref[:, lanes] = y.astype(o_ref.dtype)


def _hyena_conv(a_src, a_blk, a_conv, m_src, m_blk, m_conv, g_src, g_blk, skip, tables, spectra,
                order, width, cb, out_dtype):
    bsz, length, _ = m_src.shape
    fwd, inv, twr, twi = tables
    hr, hi = spectra
    radix, n, _ = twr.shape
    nc = width // cb
    conv_a = a_conv is not None
    has_gate = g_src is not None
    once = dict(pipeline_mode=pl.Buffered(1))
    inputs, specs = [], []

    def add(arr, spec):
        inputs.append(arr)
        specs.append(spec)

    add(a_src, pl.BlockSpec((None, length, cb), lambda c, b: (b, 0, a_blk + c)))
    if conv_a:
        add(a_conv[0], pl.BlockSpec((HY_CONV, cb), lambda c, b: (0, a_conv[2] + c)))
        add(a_conv[1], pl.BlockSpec((1, cb), lambda c, b: (0, a_conv[2] + c)))
    add(m_src, pl.BlockSpec(memory_space=pl.ANY))
    add(m_conv[0], pl.BlockSpec((HY_CONV, cb), lambda c, b: (0, m_conv[2] + c)))
    add(m_conv[1], pl.BlockSpec((1, cb), lambda c, b: (0, m_conv[2] + c)))
    if has_gate:
        add(g_src, pl.BlockSpec(memory_space=pl.ANY))
    add(skip, pl.BlockSpec((None, 1, cb), lambda c, b: (order, 0, c)))
    add(fwd, pl.BlockSpec((2 * n, n), lambda c, b: (0, 0), **once))
    add(inv, pl.BlockSpec((n, 2 * n), lambda c, b: (0, 0), **once))
    add(twr, pl.BlockSpec((radix, n, LANES), lambda c, b: (0, 0, 0), **once))
    add(twi, pl.BlockSpec((radix, n, LANES), lambda c, b: (0, 0, 0), **once))
    add(hr, pl.BlockSpec((radix, n, cb), lambda c, b: (0, 0, order * nc + c), **once))
    add(hi, pl.BlockSpec((radix, n, cb), lambda c, b: (0, 0, order * nc + c), **once))
    side_bufs = [pltpu.VMEM((length, cb), m_src.dtype)] * (2 if has_gate else 1)
    return pl.pallas_call(
        functools.partial(_hyena_conv_kernel, conv_a=conv_a, has_gate=has_gate, m_blk=m_blk,
                          g_blk=g_blk),
        out_shape=jax.ShapeDtypeStruct((bsz, length, width), out_dtype),
        grid=(nc, bsz),
        in_specs=specs,
        out_specs=pl.BlockSpec((None, length, cb), lambda c, b: (b, 0, c)),
        scratch_shapes=[pltpu.VMEM((cb // LANES, length, LANES), F32),
                        pltpu.VMEM((radix, 2 * n, cb), F32)]
                       + ([pltpu.VMEM((length, cb), F32)] if conv_a else []) + side_bufs
                       + [pltpu.SemaphoreType.DMA((2,))],
        compiler_params=_cparams(("arbitrary", "arbitrary")),
        name="hyena_long_conv_o%d" % order,
    )(*inputs)


def _outproj_kernel(x_ref, ya_ref, ga_ref, yb_ref, of_ref, ob_ref, gc_ref, hgn_ref, w_ref, mod_ref,
                    fg_ref, o_ref, *, final, n_heads):
    wa = ya_ref.shape[1]
    wb = yb_ref.shape[1]
    wc = of_ref.shape[1]
    dv = wc // n_heads
    ya = (ya_ref[...] * _silu(ga_ref[...].astype(F32))).astype(BF16)
    acc = jnp.dot(ya, w_ref[0:wa, :], preferred_element_type=F32)
    acc = acc + jnp.dot(yb_ref[...].astype(BF16), w_ref[wa:wa + wb, :], preferred_element_type=F32)
    oc = of_ref[...] + ob_ref[...]
    gate_c = _silu(gc_ref[...].astype(F32)) * hgn_ref[...]
    for h in range(n_heads):
        sl = slice(h * dv, (h + 1) * dv)
        och = oc[:, sl]
        ms = jnp.mean(och * och, axis=-1, keepdims=True)
        ych = (och * lax.rsqrt(ms + EPS) * gate_c[:, sl]).astype(BF16)
        acc = acc + jnp.dot(ych, w_ref[wa + wb + h * dv:wa + wb + (h + 1) * dv, :],
                            preferred_element_type=F32)
    xn = x_ref[...] + mod_ref[2:3, :] * acc
    if final:
        ms = jnp.mean(xn * xn, axis=-1, keepdims=True)
        xn = xn * lax.rsqrt(ms + EPS) * fg_ref[...]
    o_ref[...] = xn


def _out_projection(x, ya, proj, ga_blk, yb, o_dirs, gc_blk, hg_norm, w_out_bf16, mod, final_g, final,
                    tm):
    bsz, t, d = x.shape
    wa = ya.shape[2]
    wb = yb.shape[2]
    wc = o_dirs.shape[3]
    tm = min(tm, t)
    return pl.pallas_call(
        functools.partial(_outproj_kernel, final=final, n_heads=HG_HEADS),
        out_shape=jax.ShapeDtypeStruct((bsz, t, d), F32),
        grid=(bsz, t // tm),
        in_specs=[
            pl.BlockSpec((None, tm, d), lambda b, i: (b, i, 0)),
            pl.BlockSpec((None, tm, wa), lambda b, i: (b, i, 0)),
            pl.BlockSpec((None, tm, wa), lambda b, i: (b, i, ga_blk)),
            pl.BlockSpec((None, tm, wb), lambda b, i: (b, i, 0)),
            pl.BlockSpec((None, None, tm, wc), lambda b, i: (0, b, i, 0)),
            pl.BlockSpec((None, None, tm, wc), lambda b, i: (1, b, i, 0)),
            pl.BlockSpec((None, tm, wc), lambda b, i: (b, i, gc_blk)),
            pl.BlockSpec((1, wc), lambda b, i: (0, 0)),
            pl.BlockSpec((wa + wb + wc, d), lambda b, i: (0, 0)),
            pl.BlockSpec((None, 3, d), lambda b, i: (b, 0, 0)),
            pl.BlockSpec((1, d), lambda b, i: (0, 0)),
        ],
        out_specs=pl.BlockSpec((None, tm, d), lambda b, i: (b, i, 0)),
        compiler_params=_cparams(("arbitrary", "arbitrary")),
        name="gate_outproj_residual",
    )(x, ya, proj, yb, o_dirs, o_dirs, proj, hg_norm.reshape(1, wc), w_out_bf16, mod,
      final_g.reshape(1, d))


def _block_diag(w):
    two, nh, n, _ = w.shape
    eye = jnp.eye(nh, dtype=w.dtype)
    dense = jnp.einsum("dhij,hg->dhigj", w, eye).reshape(two, nh * n, nh * n)
    return dense.astype(BF16)


def _mixers(x, mod, norm_g, lp, init, grid_rows, with_output):
    bsz, t, d = x.shape
    wl = lp["lru_w"]
    wh = lp["hy_w"]
    wg = lp["hg_w"]
    if grid_rows is None:
        proj = _in_projection(x.reshape(1, bsz * t, d), mod[:1], norm_g, lp["w_in"], tm=2048, tn=2048)
        proj = proj.reshape(bsz, t, -1)
    else:
        proj = _in_projection(x, mod, norm_g, lp["w_in"], tm=1024, tn=4096)
    xa = proj[:, :, :wl].astype(F32)
    if grid_rows is not None:
        x_tm = xa.reshape(bsz, grid_rows, GRID_W, wl).transpose(2, 1, 0, 3).reshape(t, bsz, wl)
    else:
        x_tm = xa.transpose(1, 0, 2)
    h_dirs, h_fin = _rglru(x_tm, init[0], lp["lru_conv_w"], lp["lru_conv_b"], lp["wa_dense"],
                           lp["lru_ba"], lp["wx_dense"], lp["lru_bx"], lp["lru_lam"], tp=128)
    q_off = 2 * wl + 4 * wh
    o_dirs, s_fin = _hgrn2(proj, q_off // wg, q_off // wg + 1, q_off // wg + 3, lp["hg_lb"], init[1],
                           tT=512)
    states = (h_fin, s_fin)
    if not with_output:
        return None, states
    ya_tm = h_dirs[0] + h_dirs[1]
    if grid_rows is not None:
        ya = ya_tm.reshape(GRID_W, grid_rows, bsz, wl).transpose(2, 1, 0, 3).reshape(bsz, t, wl)
    else:
        ya = ya_tm.transpose(1, 0, 2)
    tables = lp["dft"][t]
    spectra = _hyena_spectra(t, tables, lp["hy_w1"], lp["hy_b1"], lp["hy_w2"],
                             lp["hy_b2"], lp["hy_w3"], lp["hy_freq"], wh, tl=512)
    cb = min(wh, 256)
    ub_blk = (2 * wl) // cb
    nb = wh // cb
    cw, cbias = lp["hy_conv_w"], lp["hy_conv_b"].reshape(1, -1)
    z = _hyena_conv(proj, ub_blk, (cw, cbias, 0), proj, ub_blk + nb, (cw, cbias, nb), None, 0,
                    lp["hy_skip"], tables, spectra, 0, wh, cb, F32)
    yb = _hyena_conv(z, 0, None, proj, ub_blk + 2 * nb, (cw, cbias, 2 * nb), proj, ub_blk + 3 * nb,
                     lp["hy_skip"], tables, spectra, 1, wh, cb, BF16)
    return (ya, yb, o_dirs, proj), states


def kernel(x, c, ctx, c_ctx, norm_g, w_mod, b_mod, w_in, w_out, lru_conv_w, lru_conv_b, lru_wa, lru_ba,
           lru_wx, lru_bx, lru_lam, hy_conv_w, hy_conv_b, hy_w1, hy_b1, hy_w2, hy_b2, hy_w3, hy_freq,
           hy_skip, hg_lb_logits, hg_norm, final_g):
    bsz, seq, d = x.shape
    depth = w_mod.shape[0]
    ctx_len = ctx.shape[1]
    rows = seq // GRID_W
    wl = lru_lam.shape[-1]
    wh = hy_skip.shape[-1]
    wg = hg_norm.shape[-1]
    dk = wg // HG_HEADS

    p = jax.nn.softmax(hg_lb_logits.astype(F32), axis=0)
    lower_bounds = jnp.cumsum(p, axis=0) - p[0]

    n_rows = -(-(bsz + 1) // SUBLANES) * SUBLANES
    cvec = jnp.zeros((n_rows, d), F32).at[:bsz].set(c).at[bsz].set(c_ctx)
    mod_all = _modulation(cvec, w_mod, b_mod)

    dft = {length: _hyena_tables(length) for length in (seq, ctx_len)}

    zero_states = (jnp.zeros((2, bsz, wl), F32), jnp.zeros((2, bsz, HG_HEADS, dk, dk), F32))
    xc = ctx
    for layer in range(depth):
        last = layer == depth - 1
        mod = mod_all[layer, :bsz].reshape(bsz, 3, d)
        mod_c = jnp.broadcast_to(mod_all[layer, bsz].reshape(1, 3, d), (bsz, 3, d))
        lp = dict(
            w_in=w_in[layer].astype(BF16), lru_w=wl, hy_w=wh, hg_w=wg,
            lru_conv_w=lru_conv_w[layer], lru_conv_b=lru_conv_b[layer],
            wa_dense=_block_diag(lru_wa[layer]), wx_dense=_block_diag(lru_wx[layer]),
            lru_ba=lru_ba[layer], lru_bx=lru_bx[layer], lru_lam=lru_lam[layer],
            hy_conv_w=hy_conv_w[layer], hy_conv_b=hy_conv_b[layer], hy_w1=hy_w1[layer],
            hy_b1=hy_b1[layer], hy_w2=hy_w2[layer], hy_b2=hy_b2[layer], hy_w3=hy_w3[layer],
            hy_freq=hy_freq[layer], hy_skip=hy_skip[layer].reshape(HY_ORDER, 1, wh),
            hg_lb=lower_bounds[layer], dft=dft)
        w_out_l = w_out[layer].astype(BF16)
        ga_blk = 1
        gc_blk = (2 * wl + 4 * wh + 4 * wg) // wg
        pieces_c, ctx_states = _mixers(xc, mod_c, norm_g[layer], lp, zero_states, None, not last)
        pieces, _ = _mixers(x, mod, norm_g[layer], lp, ctx_states, rows, True)
        ya, yb, o_dirs, proj = pieces
        x = _out_projection(x, ya, proj, ga_blk, yb, o_dirs, gc_blk, hg_norm[layer], w_out_l, mod,
                            final_g, last, tm=512)
        if not last:
            ya, yb, o_dirs, proj = pieces_c
            xc = _out_projection(xc, ya, proj, ga_blk, yb, o_dirs, gc_blk, hg_norm[layer], w_out_l,
                                 mod_c, final_g, False, tm=512)
    return x
```

```python
import functools
import math

import jax
import jax.numpy as jnp
from jax import lax
from jax.experimental import pallas as pl
from jax.experimental.pallas import tpu as pltpu

GRID_W = 64
LRU_HEADS = 8
LRU_CONV = 4
LRU_C = 8.0
HY_ORDER = 2
HY_CONV = 3
HY_BANDS = 16
HY_FAST_DECAY = 0.3
HY_SLOW_DECAY = 1.5
HY_TARGET = 1e-2
HG_HEADS = 8
HG_CHUNK = 64
HG_GROUP = 8
HG_SAFE_SPAN = 60.0
EPS = 1e-6
TINY = 1e-12

LANES = 128
SUBLANES = 8
ROW_GROUP = 16
VMEM_LIMIT_BYTES = 56 * 1024 * 1024

INPROJ_TILE = (1024, 4096)
INPROJ_CTX_TILE = (2048, 2048)
LRU_STEPS = 128
HG_ROWS = 512
HY_FILTER_ROWS = 512
HY_CHANNELS = 256
OUTPROJ_ROWS = 512

F32 = jnp.float32
BF16 = jnp.bfloat16
PROJ_DTYPE = BF16


def _cparams(semantics):
    return pltpu.CompilerParams(dimension_semantics=semantics, vmem_limit_bytes=VMEM_LIMIT_BYTES)


def _sigmoid(x):
    return 1.0 / (1.0 + jnp.exp(-x))


def _sigmoid_tanh(x):
    return 0.5 * jnp.tanh(0.5 * x) + 0.5


def _silu(x):
    return x * _sigmoid(x)


def _mod_kernel(c_ref, w_ref, b_ref, o_ref):
    s = _silu(c_ref[...])
    o_ref[...] = jnp.dot(s, w_ref[...], preferred_element_type=F32,
                         precision=lax.Precision.HIGHEST) + b_ref[...]


def _modulation(cvec, w_mod, b_mod):
    depth, d, d3 = w_mod.shape
    r = cvec.shape[0]
    return pl.pallas_call(
        _mod_kernel,
        out_shape=jax.ShapeDtypeStruct((depth, r, d3), F32),
        grid=(depth, d3 // d),
        in_specs=[
            pl.BlockSpec((r, d), lambda l, j: (0, 0)),
            pl.BlockSpec((None, d, d), lambda l, j: (l, 0, j)),
            pl.BlockSpec((None, 1, d), lambda l, j: (l, 0, j)),
        ],
        out_specs=pl.BlockSpec((None, r, d), lambda l, j: (l, 0, j)),
        compiler_params=_cparams(("arbitrary", "arbitrary")),
        name="adaln_modulation",
    )(cvec, w_mod, b_mod.reshape(depth, 1, d3))


def _inproj_kernel(x_ref, mod_ref, g_ref, w_ref, o_ref, h_s):
    @pl.when(pl.program_id(2) == 0)
    def _():
        x = x_ref[...]
        ms = jnp.mean(x * x, axis=-1, keepdims=True)
        y = x * lax.rsqrt(ms + EPS) * g_ref[...]
        h = y * (1.0 + mod_ref[1:2, :]) + mod_ref[0:1, :]
        h_s[...] = h.astype(BF16)

    o_ref[...] = jnp.dot(h_s[...], w_ref[...], preferred_element_type=F32).astype(o_ref.dtype)


def _in_projection(x, mod, gain, w_bf16, tm, tn):
    bsz, t, d = x.shape
    d_in = w_bf16.shape[1]
    tm = min(tm, t)
    tn = min(tn, d_in)
    return pl.pallas_call(
        _inproj_kernel,
        out_shape=jax.ShapeDtypeStruct((bsz, t, d_in), PROJ_DTYPE),
        grid=(bsz, t // tm, d_in // tn),
        in_specs=[
            pl.BlockSpec((None, tm, d), lambda b, i, j: (b, i, 0)),
            pl.BlockSpec((None, 3, d), lambda b, i, j: (b, 0, 0)),
            pl.BlockSpec((1, d), lambda b, i, j: (0, 0)),
            pl.BlockSpec((d, tn), lambda b, i, j: (0, j)),
        ],
        out_specs=pl.BlockSpec((None, tm, tn), lambda b, i, j: (b, i, j)),
        scratch_shapes=[pltpu.VMEM((tm, d), BF16)],
        compiler_params=_cparams(("arbitrary", "arbitrary", "arbitrary")),
        name="rmsnorm_adaln_inproj",
    )(x, mod, gain.reshape(1, d), w_bf16)


def _lru_kernel(xprev_ref, x_ref, xnext_ref, h0_ref, cw_ref, cb_ref, wa_ref, ba_ref, wx_ref,
                bx_ref, lam_ref, h_ref, hfin_ref, xpad, a_s, b_s, hstate):
    d = pl.program_id(0)
    i = pl.program_id(1)
    n = pl.num_programs(1)
    idx = jnp.where(d == 0, i, n - 1 - i)
    tp, bsz, ch = x_ref.shape

    @pl.when(i == 0)
    def _():
        hstate[...] = h0_ref[...]

    xpad[0:2] = jnp.where(idx > 0, xprev_ref[...], 0.0)
    xpad[2:2 + tp] = x_ref[...]
    xpad[2 + tp:3 + tp] = jnp.where(idx < n - 1, xnext_ref[...], 0.0)
    u = cb_ref[...] + cw_ref[0:1, :] * xpad[0:tp]
    for k in range(1, LRU_CONV):
        u = u + cw_ref[k:k + 1, :] * xpad[k:k + tp]
    u2 = u.reshape(tp * bsz, ch)
    ub = u2.astype(BF16)
    r = _sigmoid_tanh(jnp.dot(ub, wa_ref[...], preferred_element_type=F32) + ba_ref[...])
    gi = _sigmoid_tanh(jnp.dot(ub, wx_ref[...], preferred_element_type=F32) + bx_ref[...])
    nlam = -lam_ref[...]
    softplus = jnp.maximum(nlam, 0.0) + jnp.log(1.0 + jnp.exp(-jnp.abs(nlam)))
    a = jnp.exp(r * ((-LRU_C) * softplus))
    var = jnp.maximum(1.0 - a * a, TINY)
    bb = (var * lax.rsqrt(var)) * (gi * u2)
    a_s[...] = a.reshape(tp, bsz, ch)
    b_s[...] = bb.reshape(tp, bsz, ch)

    def body(t, h):
        tt = jnp.where(d == 0, t, tp - 1 - t)
        h = a_s[tt] * h + b_s[tt]
        h_ref[tt] = h
        return h

    h = lax.fori_loop(0, tp, body, hstate[...], unroll=8)
    hstate[...] = h
    hfin_ref[...] = h


def _rglru(x_tm, h0, conv_w, conv_b, wa_dense, ba, wx_dense, bx, lam, tp):
    p, bsz, ch = x_tm.shape
    tp = min(tp, p)
    n = p // tp

    def tile(d, i):
        return jnp.where(d == 0, i, n - 1 - i)

    vec = lambda a: a.reshape(2, 1, ch)
    return pl.pallas_call(
        _lru_kernel,
        out_shape=(jax.ShapeDtypeStruct((2, p, bsz, ch), F32),
                   jax.ShapeDtypeStruct((2, bsz, ch), F32)),
        grid=(2, n),
        in_specs=[
            pl.BlockSpec((2, bsz, ch), lambda d, i: (jnp.maximum(tile(d, i) * (tp // 2) - 1, 0), 0, 0)),
            pl.BlockSpec((tp, bsz, ch), lambda d, i: (tile(d, i), 0, 0)),
            pl.BlockSpec((1, bsz, ch), lambda d, i: (jnp.minimum((tile(d, i) + 1) * tp, p - 1), 0, 0)),
            pl.BlockSpec((None, bsz, ch), lambda d, i: (d, 0, 0)),
            pl.BlockSpec((LRU_CONV, ch), lambda d, i: (0, 0)),
            pl.BlockSpec((1, ch), lambda d, i: (0, 0)),
            pl.BlockSpec((None, ch, ch), lambda d, i: (d, 0, 0)),
            pl.BlockSpec((None, 1, ch), lambda d, i: (d, 0, 0)),
            pl.BlockSpec((None, ch, ch), lambda d, i: (d, 0, 0)),
            pl.BlockSpec((None, 1, ch), lambda d, i: (d, 0, 0)),
            pl.BlockSpec((None, 1, ch), lambda d, i: (d, 0, 0)),
        ],
        out_specs=(
            pl.BlockSpec((None, tp, bsz, ch), lambda d, i: (d, tile(d, i), 0, 0)),
            pl.BlockSpec((None, bsz, ch), lambda d, i: (d, 0, 0)),
        ),
        scratch_shapes=[
            pltpu.VMEM((tp + LRU_CONV - 1, bsz, ch), F32),
            pltpu.VMEM((tp, bsz, ch), F32),
            pltpu.VMEM((tp, bsz, ch), F32),
            pltpu.VMEM((bsz, ch), F32),
        ],
        compiler_params=_cparams(("arbitrary", "arbitrary")),
        name="rglru_bidir_scan",
    )(x_tm, x_tm, x_tm, h0, conv_w, conv_b.reshape(1, ch), wa_dense, vec(ba), wx_dense, vec(bx),
      vec(lam))


def _hgrn2_kernel(q_ref, z_ref, v_ref, lb_ref, s0_ref, o_ref, sfin_ref, *states):
    d = pl.program_id(0)
    i = pl.program_id(2)
    tT = q_ref.shape[0]
    nh, dv, dk = states[0].shape
    c = HG_CHUNK
    nch = tT // c
    group = len(states) - 1

    @pl.when(i == 0)
    def _():
        states[0][...] = s0_ref[...]

    row = lax.broadcasted_iota(jnp.int32, (c, c), 0)
    col = lax.broadcasted_iota(jnp.int32, (c, c), 1)
    fwd = d == 0
    keep = (col - row) * jnp.where(fwd, 1, -1) <= 0
    tri = jnp.where(keep, 1.0, 0.0).astype(BF16)
    tri2 = jnp.concatenate([tri, tri], axis=1)
    nt_dims = (((1,), (1,)), ((), ()))
    tn_dims = (((0,), (0,)), ((), ()))

    def gates(z, lb):
        e = jnp.exp(-jnp.abs(z))
        s_big = 1.0 / (1.0 + e)
        sig = jnp.where(z >= 0, s_big, e * s_big)
        f = lb + (1.0 - lb) * sig
        return jnp.maximum(f, TINY), 1.0 - f

    heads = [(slice(h * dk, (h + 1) * dk), slice(h * dv, (h + 1) * dv)) for h in range(nh)]

    def chunk_base(j):
        return pl.multiple_of(jnp.where(fwd, j, nch - 1 - j) * c, c)

    def stage_a(rows):
        out = []
        for sl, _ in heads:
            f, kk = gates(z_ref[rows, sl].astype(F32), lb_ref[:, sl])
            g = jnp.log(f)
            g1 = g.astype(BF16)
            g2 = (g - g1.astype(F32)).astype(BF16)
            gc = jnp.dot(tri2, jnp.concatenate([g1, g2], axis=0), preferred_element_type=F32)
            out.append((kk, gc))
        return out

    def stage_b(rows, from_a, src, dst):
        out = []
        span = jnp.zeros((1, dk), F32)
        for h, ((sl, sv), (kk, gc)) in enumerate(zip(heads, from_a)):
            ref_row = gc[c // 2:c // 2 + 1, :]
            tot = jnp.where(fwd, gc[c - 1:c, :], gc[0:1, :])
            span = jnp.maximum(span, jnp.maximum(jnp.abs(gc[0:1, :] - ref_row),
                                                 jnp.abs(gc[c - 1:c, :] - ref_row)))
            e_q = jnp.exp(gc - ref_row)
            qp = q_ref[rows, sl].astype(F32) * e_q
            kp = kk * (1.0 / e_q)
            qpp = (qp * jnp.exp(ref_row)).astype(BF16)
            kpp = (kp * jnp.exp(tot - ref_row)).astype(BF16)
            sc = lax.dot_general(qp.astype(BF16), kp.astype(BF16), nt_dims, preferred_element_type=F32)
            s_t = src[h]
            inter = lax.dot_general(qpp, s_t.astype(BF16), nt_dims, preferred_element_type=F32)
            upd = lax.dot_general(v_ref[rows, sv].astype(BF16), kpp, tn_dims, preferred_element_type=F32)
            dst[h] = s_t * jnp.exp(tot) + upd
            out.append((sc, inter))
        return out, span

    def stage_c(rows, from_b):
        for (_, sv), (sc, inter) in zip(heads, from_b):
            sc = jnp.where(keep, sc, 0.0).astype(BF16)
            o_ref[rows, sv] = inter + jnp.dot(sc, v_ref[rows, sv].astype(BF16), preferred_element_type=F32)

    def exact_chunk(base, src, dst):
        sub = lax.broadcasted_iota(jnp.int32, (ROW_GROUP, 1), 0)
        dst[...] = src[...]

        def step(u, carry):
            t = base + jnp.where(fwd, u, c - 1 - u)
            grp = pl.ds(pl.multiple_of((t // ROW_GROUP) * ROW_GROUP, ROW_GROUP), ROW_GROUP)
            live = sub == t % ROW_GROUP
            for h, (sl, sv) in enumerate(heads):
                f_g, k_g = gates(z_ref[grp, sl].astype(F32), lb_ref[:, sl])
                f = jnp.sum(jnp.where(live, f_g, 0.0), axis=0, keepdims=True)
                v_g = jnp.where(live, v_ref[grp, sv].astype(F32), 0.0)
                s_new = dst[h] * f + lax.dot_general(v_g, k_g, tn_dims, preferred_element_type=F32)
                dst[h] = s_new
                o_g = lax.dot_general(q_ref[grp, sl].astype(F32), s_new, nt_dims,
                                      preferred_element_type=F32)
                o_ref[grp, sv] = jnp.where(live, o_g, o_ref[grp, sv])
            return carry

        lax.fori_loop(0, c, step, 0)

    def chunk_group(jj, carry):
        bases = [chunk_base(group * jj + g) for g in range(group)]
        rows = [pl.ds(b, c) for b in bases]
        span = jnp.zeros((1, dk), F32)
        from_a = stage_a(rows[0])
        for g in range(group):
            from_b, span_g = stage_b(rows[g], from_a, states[g], states[g + 1])
            span = jnp.maximum(span, span_g)
            if g + 1 < group:
                from_a = stage_a(rows[g + 1])
            stage_c(rows[g], from_b)

        @pl.when(jnp.max(span) > HG_SAFE_SPAN)
        def _():
            for g in range(group):
                exact_chunk(bases[g], states[g], states[g + 1])

        states[0][...] = states[group][...]
        return carry

    lax.fori_loop(0, nch // group, chunk_group, 0)
    sfin_ref[...] = states[0][...]


def _hgrn2(proj, q_blk, z_blk, v_blk, lb, s0, tT):
    bsz, t, _ = proj.shape
    _, _, nh, dv, dk = s0.shape
    w = nh * dk
    tT = min(tT, t)
    n = t // tT
    nch = tT // HG_CHUNK
    group = max(g for g in (1, 2, 4, HG_GROUP) if nch % g == 0)

    def tile(d, i):
        return jnp.where(d == 0, i, n - 1 - i)

    return pl.pallas_call(
        _hgrn2_kernel,
        out_shape=(jax.ShapeDtypeStruct((2, bsz, t, nh * dv), F32),
                   jax.ShapeDtypeStruct((2, bsz, nh, dv, dk), F32)),
        grid=(2, bsz, n),
        in_specs=[
            pl.BlockSpec((None, tT, w), lambda d, b, i: (b, tile(d, i), q_blk)),
            pl.BlockSpec((None, tT, w), lambda d, b, i: (b, tile(d, i), z_blk + d)),
            pl.BlockSpec((None, tT, w), lambda d, b, i: (b, tile(d, i), v_blk)),
            pl.BlockSpec((None, 1, w), lambda d, b, i: (d, 0, 0)),
            pl.BlockSpec((None, None, nh, dv, dk), lambda d, b, i: (d, b, 0, 0, 0)),
        ],
        out_specs=(
            pl.BlockSpec((None, None, tT, nh * dv), lambda d, b, i: (d, b, tile(d, i), 0)),
            pl.BlockSpec((None, None, nh, dv, dk), lambda d, b, i: (d, b, 0, 0, 0)),
        ),
        scratch_shapes=[pltpu.VMEM((nh, dv, dk), F32)] * (group + 1),
        compiler_params=_cparams(("arbitrary", "arbitrary", "arbitrary")),
        name="hgrn2_bidir_chunk_scan",
    )(proj, proj, proj, lb.reshape(2, 1, w), s0)


HY_MAX_RADIX = 8
HY_MIN_SUBLEN = 128
HY_CHUNK = 8


def _hyena_radix(length):
    return max(1, min(HY_MAX_RADIX, length // HY_MIN_SUBLEN))


def _hyena_tables(length):
    radix = _hyena_radix(length)
    n = length // radix
    k2 = 2 * jnp.arange(n, dtype=jnp.int32) + 1
    ph = (k2[:, None] * jnp.arange(n, dtype=jnp.int32)[None, :]) % (4 * n)
    psi = ph.astype(F32) * (2.0 * math.pi / (4 * n))
    fc, fs = jnp.cos(psi), jnp.sin(psi)
    fwd = jnp.concatenate([fc, fs], axis=0).astype(BF16)
    inv = jnp.concatenate([fc.T, -fs.T], axis=1).astype(BF16)
    phr = (jnp.arange(radix, dtype=jnp.int32)[:, None] * k2[None, :]) % (4 * length)
    phi = phr.astype(F32) * (2.0 * math.pi / (4 * length))
    rep = lambda a: jnp.broadcast_to(a[:, :, None], (radix, n, LANES))
    return fwd, inv, rep(jnp.cos(phi)), rep(jnp.sin(phi))


def _cmul_const(x, c, s):
    re, im = x
    if abs(s) < 1e-12:
        return (re, im) if c > 0 else (-re, -im)
    if abs(c) < 1e-12:
        return (-im, re) if s > 0 else (im, -re)
    return (re * c - im * s, re * s + im * c)


def _fft_list(xs, sign):
    r = len(xs)
    if r == 1:
        return xs
    ev = _fft_list(xs[0::2], sign)
    od = _fft_list(xs[1::2], sign)
    out = [None] * r
    for j in range(r // 2):
        ang = sign * 2.0 * math.pi * j / r
        t = _cmul_const(od[j], math.cos(ang), math.sin(ang))
        out[j] = (ev[j][0] + t[0], ev[j][1] + t[1])
        out[j + r // 2] = (ev[j][0] - t[0], ev[j][1] - t[1])
    return out


def _hy_sub_transforms(slab, fwd_ref, z_ref, radix, n):
    n_slabs = slab.shape[0]
    for r in range(radix):
        parts = [slab[s, pl.ds(r, n, stride=radix), :] if radix > 1 else slab[s]
                 for s in range(n_slabs)]
        xr = parts[0] if n_slabs == 1 else jnp.concatenate(parts, axis=1)
        z_ref[r] = jnp.dot(fwd_ref[...], xr.astype(BF16), preferred_element_type=F32)


def _hy_load_twiddles(twr_ref, twi_ref, rows_c, radix):
    return [(twr_ref[r, rows_c, :], twi_ref[r, rows_c, :]) for r in range(1, radix)]


def _hy_twiddled_inputs(z_ref, tw, rows_c, rows_s, lanes, radix):
    xs = []
    for r in range(radix):
        c = z_ref[r, rows_c, lanes]
        s = z_ref[r, rows_s, lanes]
        if r == 0:
            xs.append((c, -s))
        else:
            wr, wi = tw[r - 1]
            xs.append((c * wr - s * wi, -(c * wi + s * wr)))
    return xs


def _filter_kernel(emb_ref, w1_ref, b1_ref, w2_ref, b2_ref, w3_ref, fr_ref, dl_ref,
                   a_ref, b_ref, ss_ref, *, length):
    i = pl.program_id(0)
    tl = emb_ref.shape[0]
    cw = dl_ref.shape[1]
    fr = fr_ref[...]
    hi = lax.Precision.HIGHEST
    hdn = jnp.sin(fr * (jnp.dot(emb_ref[...], w1_ref[...], preferred_element_type=F32,
                                precision=hi) + b1_ref[...]))
    hdn = jnp.sin(fr * (jnp.dot(hdn, w2_ref[...], preferred_element_type=F32,
                                precision=hi) + b2_ref[...]))
    h = jnp.dot(hdn.astype(BF16), w3_ref[...].astype(BF16), preferred_element_type=F32)
    pos = (lax.broadcasted_iota(jnp.int32, (tl, 1), 0) + i * tl)
    t = pos.astype(F32) / float(length)
    win = jnp.exp(-t * dl_ref[...])
    not_first = pos > 0

    @pl.when(i == 0)
    def _():
        ss_ref[...] = jnp.zeros_like(ss_ref)

    for o in range(HY_ORDER):
        hpos = h[:, (2 * o) * cw:(2 * o + 1) * cw] * win
        hneg = jnp.where(not_first, h[:, (2 * o + 1) * cw:(2 * o + 2) * cw] * win, 0.0)
        a_ref[:, o * cw:(o + 1) * cw] = hpos + hneg
        b_ref[:, o * cw:(o + 1) * cw] = hneg - hpos
        ss_ref[:, o * cw:(o + 1) * cw] += jnp.sum(hpos * hpos + hneg * hneg, axis=0, keepdims=True)


def _spectrum_kernel(a_ref, b_ref, ss_ref, fwd_ref, twr_ref, twi_ref, hr_ref, hi_ref, slab, zab,
                     *, scale):
    radix, n, _ = hr_ref.shape
    slab[0] = a_ref[...]
    slab[1] = b_ref[...]
    _hy_sub_transforms(slab, fwd_ref, zab, radix, n)
    norm = lax.rsqrt(ss_ref[...] + TINY) * scale
    lanes_a, lanes_b = slice(0, LANES), slice(LANES, 2 * LANES)

    def step(i, carry):
        base = pl.multiple_of(i * HY_CHUNK, HY_CHUNK)
        rows_c = pl.ds(base, HY_CHUNK)
        rows_s = pl.ds(base + n, HY_CHUNK)
        tw = _hy_load_twiddles(twr_ref, twi_ref, rows_c, radix)
        fa = _fft_list(_hy_twiddled_inputs(zab, tw, rows_c, rows_s, lanes_a, radix), -1)
        fb = _fft_list(_hy_twiddled_inputs(zab, tw, rows_c, rows_s, lanes_b, radix), -1)
        for j in range(radix):
            hr_ref[j, rows_c, :] = fa[j][0] * norm
            hi_ref[j, rows_c, :] = -fb[j][1] * norm
        return carry

    lax.fori_loop(0, n // HY_CHUNK, step, 0)


def _hyena_spectra(length, tables, w1, b1, w2, b2, w3, freq, width, tl):
    f32 = F32
    pos = jnp.arange(length, dtype=f32)
    t = pos / length
    bands = jnp.linspace(1e-4, HY_BANDS - 1, HY_BANDS, dtype=f32)
    ang = (2.0 * math.pi / length) * pos[:, None] * bands[None, :]
    emb = jnp.concatenate([t[:, None], jnp.cos(ang), jnp.sin(ang)], axis=-1)
    deltas = jnp.abs(jnp.linspace(math.log(HY_TARGET) / HY_SLOW_DECAY,
                                  math.log(HY_TARGET) / HY_FAST_DECAY, width, dtype=f32))
    n_emb = -(-emb.shape[1] // LANES) * LANES
    w1 = jnp.pad(w1.astype(f32), ((0, n_emb - emb.shape[1]), (0, 0)))
    emb = jnp.pad(emb, ((0, 0), (0, n_emb - emb.shape[1])))
    hid = w1.shape[1]
    cols = HY_ORDER * width
    tl = min(tl, length)
    full = lambda *s: pl.BlockSpec(s, lambda i: (0,) * len(s))
    a_un, b_un, ss = pl.pallas_call(
        functools.partial(_filter_kernel, length=length),
        out_shape=(jax.ShapeDtypeStruct((length, cols), f32),
                   jax.ShapeDtypeStruct((length, cols), f32),
                   jax.ShapeDtypeStruct((1, cols), f32)),
        grid=(length // tl,),
        in_specs=[
            pl.BlockSpec((tl, n_emb), lambda i: (i, 0)),
            full(n_emb, hid), full(1, hid), full(hid, hid), full(1, hid),
            full(hid, 2 * cols), full(1, hid), full(1, width),
        ],
        out_specs=(pl.BlockSpec((tl, cols), lambda i: (i, 0)),
                   pl.BlockSpec((tl, cols), lambda i: (i, 0)),
                   pl.BlockSpec((1, cols), lambda i: (0, 0))),
        compiler_params=_cparams(("arbitrary",)),
        name="hyena_filter_mlp",
    )(emb, w1, b1.reshape(1, hid), w2, b2.reshape(1, hid), w3, freq.reshape(1, hid),
      deltas.reshape(1, width))
    fwd, _, twr, twi = tables
    radix, n, _ = twr.shape
    cb = LANES
    return pl.pallas_call(
        functools.partial(_spectrum_kernel, scale=1.0 / length),
        out_shape=(jax.ShapeDtypeStruct((radix, n, cols), f32),
                   jax.ShapeDtypeStruct((radix, n, cols), f32)),
        grid=(cols // cb,),
        in_specs=[
            pl.BlockSpec((length, cb), lambda j: (0, j)),
            pl.BlockSpec((length, cb), lambda j: (0, j)),
            pl.BlockSpec((1, cb), lambda j: (0, j)),
            pl.BlockSpec((2 * n, n), lambda j: (0, 0)),
            pl.BlockSpec((radix, n, LANES), lambda j: (0, 0, 0)),
            pl.BlockSpec((radix, n, LANES), lambda j: (0, 0, 0)),
        ],
        out_specs=(pl.BlockSpec((radix, n, cb), lambda j: (0, 0, j)),
                   pl.BlockSpec((radix, n, cb), lambda j: (0, 0, j))),
        scratch_shapes=[pltpu.VMEM((2, length, LANES), f32),
                        pltpu.VMEM((radix, 2 * n, 2 * LANES), f32)],
        compiler_params=_cparams(("arbitrary",)),
        name="hyena_filter_spectrum",
    )(a_un, b_un, ss, fwd, twr, twi)


def _short_conv(x, w_ref, b_ref):
    length = x.shape[0]
    rows = lax.broadcasted_iota(jnp.int32, (length, 1), 0)
    prev = jnp.where(rows > 0, pltpu.roll(x, 1, axis=0), 0.0)
    nxt = jnp.where(rows < length - 1, pltpu.roll(x, length - 1, axis=0), 0.0)
    return w_ref[0:1, :] * prev + w_ref[1:2, :] * x + w_ref[2:3, :] * nxt + b_ref[...]


def _hyena_conv_kernel(*refs, conv_a, has_gate, m_blk, g_blk):
    it = iter(refs)
    a_ref = next(it)
    aw_ref = ab_ref = None
    if conv_a:
        aw_ref, ab_ref = next(it), next(it)
    m_hbm, mw_ref, mb_ref = next(it), next(it), next(it)
    g_hbm = next(it) if has_gate else None
    skip_ref, fwd_ref, inv_ref, twr_ref, twi_ref, hr_ref, hi_ref = (next(it) for _ in range(7))
    o_ref = next(it)
    slab, z_s = next(it), next(it)
    a_keep = next(it) if conv_a else None
    m_ref = next(it)
    g_ref = next(it) if has_gate else None
    sems = next(it)
    radix, n, cb = hr_ref.shape
    tiles = [slice(t * LANES, (t + 1) * LANES) for t in range(cb // LANES)]

    def side_copies():
        c, b = pl.program_id(0), pl.program_id(1)
        cps = [pltpu.make_async_copy(m_hbm.at[b, :, pl.ds((m_blk + c) * cb, cb)], m_ref, sems.at[0])]
        if has_gate:
            cps.append(pltpu.make_async_copy(g_hbm.at[b, :, pl.ds((g_blk + c) * cb, cb)], g_ref,
                                             sems.at[1]))
        return cps

    for cp in side_copies():
        cp.start()

    def a_value():
        a = a_ref[...].astype(F32)
        return _short_conv(a, aw_ref, ab_ref) if conv_a else a

    a = a_value()
    if conv_a:
        a_keep[...] = a
    for s, lanes in enumerate(tiles):
        slab[s] = a[:, lanes]
    _hy_sub_transforms(slab, fwd_ref, z_s, radix, n)

    def step(i, carry):
        base = pl.multiple_of(i * HY_CHUNK, HY_CHUNK)
        rows_c = pl.ds(base, HY_CHUNK)
        rows_s = pl.ds(base + n, HY_CHUNK)
        tw = _hy_load_twiddles(twr_ref, twi_ref, rows_c, radix)
        for lanes in tiles:
            zf = _fft_list(_hy_twiddled_inputs(z_s, tw, rows_c, rows_s, lanes, radix), -1)
            ys = []
            for j in range(radix):
                hr = hr_ref[j, rows_c, lanes]
                hi = hi_ref[j, rows_c, lanes]
                zr, zi = zf[j]
                ys.append((zr * hr - zi * hi, zr * hi + zi * hr))
            vs = _fft_list(ys, +1)
            for r in range(radix):
                ur, ui = vs[r]
                if r > 0:
                    wr, wi = tw[r - 1]
                    ur, ui = wr * ur - wi * ui, wr * ui + wi * ur
                z_s[r, rows_c, lanes] = ur
                z_s[r, rows_s, lanes] = ui
        return carry

    lax.fori_loop(0, n // HY_CHUNK, step, 0)

    for r in range(radix):
        yr = jnp.dot(inv_ref[...], z_s[r].astype(BF16), preferred_element_type=F32)
        for s, lanes in enumerate(tiles):
            if radix > 1:
                slab[s, pl.ds(r, n, stride=radix), :] = yr[:, lanes]
            else:
                slab[s] = yr[:, lanes]

    for cp in side_copies():
        cp.wait()
    a = a_keep[...] if conv_a else a_value()
    m = _short_conv(m_ref[...].astype(F32), mw_ref, mb_ref)
    for s, lanes in enumerate(tiles):
        y = m[:, lanes] * (slab[s] + skip_ref[:, lanes] * a[:, lanes])
        if has_gate:
            y = y * _silu(g_ref[:, lanes].astype(F32))
        o_ref[:, lanes] = y.astype(o_ref.dtype)


def _hyena_conv(a_src, a_blk, a_conv, m_src, m_blk, m_conv, g_src, g_blk, skip, tables, spectra,
                order, width, cb, out_dtype):
    bsz, length, _ = m_src.shape
    fwd, inv, twr, twi = tables
    hr, hi = spectra
    radix, n, _ = twr.shape
    nc = width // cb
    conv_a = a_conv is not None
    has_gate = g_src is not None
    once = dict(pipeline_mode=pl.Buffered(1))
    inputs, specs = [], []

    def add(arr, spec):
        inputs.append(arr)
        specs.append(spec)

    add(a_src, pl.BlockSpec((None, length, cb), lambda c, b: (b, 0, a_blk + c)))
    if conv_a:
        add(a_conv[0], pl.BlockSpec((HY_CONV, cb), lambda c, b: (0, a_conv[2] + c)))
        add(a_conv[1], pl.BlockSpec((1, cb), lambda c, b: (0, a_conv[2] + c)))
    add(m_src, pl.BlockSpec(memory_space=pl.ANY))
    add(m_conv[0], pl.BlockSpec((HY_CONV, cb), lambda c, b: (0, m_conv[2] + c)))
    add(m_conv[1], pl.BlockSpec((1, cb), lambda c, b: (0, m_conv[2] + c)))
    if has_gate:
        add(g_src, pl.BlockSpec(memory_space=pl.ANY))
    add(skip, pl.BlockSpec((None, 1, cb), lambda c, b: (order, 0, c)))
    add(fwd, pl.BlockSpec((2 * n, n), lambda c, b: (0, 0), **once))
    add(inv, pl.BlockSpec((n, 2 * n), lambda c, b: (0, 0), **once))
    add(twr, pl.BlockSpec((radix, n, LANES), lambda c, b: (0, 0, 0), **once))
    add(twi, pl.BlockSpec((radix, n, LANES), lambda c, b: (0, 0, 0), **once))
    add(hr, pl.BlockSpec((radix, n, cb), lambda c, b: (0, 0, order * nc + c), **once))
    add(hi, pl.BlockSpec((radix, n, cb), lambda c, b: (0, 0, order * nc + c), **once))
    side_bufs = [pltpu.VMEM((length, cb), m_src.dtype)] * (2 if has_gate else 1)
    return pl.pallas_call(
        functools.partial(_hyena_conv_kernel, conv_a=conv_a, has_gate=has_gate, m_blk=m_blk,
                          g_blk=g_blk),
        out_shape=jax.ShapeDtypeStruct((bsz, length, width), out_dtype),
        grid=(nc, bsz),
        in_specs=specs,
        out_specs=pl.BlockSpec((None, length, cb), lambda c, b: (b, 0, c)),
        scratch_shapes=[pltpu.VMEM((cb // LANES, length, LANES), F32),
                        pltpu.VMEM((radix, 2 * n, cb), F32)]
                       + ([pltpu.VMEM((length, cb), F32)] if conv_a else []) + side_bufs
                       + [pltpu.SemaphoreType.DMA((2,))],
        compiler_params=_cparams(("arbitrary", "arbitrary")),
        name="hyena_long_conv_o%d" % order,
    )(*inputs)


def _outproj_kernel(x_ref, ya_ref, ga_ref, yb_ref, of_ref, ob_ref, gc_ref, hgn_ref, w_ref, mod_ref,
                    fg_ref, o_ref, *, final, n_heads):
    wa = ya_ref.shape[1]
    wb = yb_ref.shape[1]
    wc = of_ref.shape[1]
    dv = wc // n_heads
    ya = (ya_ref[...] * _silu(ga_ref[...].astype(F32))).astype(BF16)
    acc = jnp.dot(ya, w_ref[0:wa, :], preferred_element_type=F32)
    acc = acc + jnp.dot(yb_ref[...].astype(BF16), w_ref[wa:wa + wb, :], preferred_element_type=F32)
    oc = of_ref[...] + ob_ref[...]
    gate_c = _silu(gc_ref[...].astype(F32)) * hgn_ref[...]
    for h in range(n_heads):
        sl = slice(h * dv, (h + 1) * dv)
        och = oc[:, sl]
        ms = jnp.mean(och * och, axis=-1, keepdims=True)
        ych = (och * lax.rsqrt(ms + EPS) * gate_c[:, sl]).astype(BF16)
        acc = acc + jnp.dot(ych, w_ref[wa + wb + h * dv:wa + wb + (h + 1) * dv, :],
                            preferred_element_type=F32)
    xn = x_ref[...] + mod_ref[2:3, :] * acc
    if final:
        ms = jnp.mean(xn * xn, axis=-1, keepdims=True)
        xn = xn * lax.rsqrt(ms + EPS) * fg_ref[...]
    o_ref[...] = xn


def _out_projection(x, ya, proj, ga_blk, yb, o_dirs, gc_blk, hg_norm, w_out_bf16, mod, final_g, final,
                    tm):
    bsz, t, d = x.shape
    wa = ya.shape[2]
    wb = yb.shape[2]
    wc = o_dirs.shape[3]
    tm = min(tm, t)
    return pl.pallas_call(
        functools.partial(_outproj_kernel, final=final, n_heads=HG_HEADS),
        out_shape=jax.ShapeDtypeStruct((bsz, t, d), F32),
        grid=(bsz, t // tm),
        in_specs=[
            pl.BlockSpec((None, tm, d), lambda b, i: (b, i, 0)),
            pl.BlockSpec((None, tm, wa), lambda b, i: (b, i, 0)),
            pl.BlockSpec((None, tm, wa), lambda b, i: (b, i, ga_blk)),
            pl.BlockSpec((None, tm, wb), lambda b, i: (b, i, 0)),
            pl.BlockSpec((None, None, tm, wc), lambda b, i: (0, b, i, 0)),
            pl.BlockSpec((None, None, tm, wc), lambda b, i: (1, b, i, 0)),
            pl.BlockSpec((None, tm, wc), lambda b, i: (b, i, gc_blk)),
            pl.BlockSpec((1, wc), lambda b, i: (0, 0)),
            pl.BlockSpec((wa + wb + wc, d), lambda b, i: (0, 0)),
            pl.BlockSpec((None, 3, d), lambda b, i: (b, 0, 0)),
            pl.BlockSpec((1, d), lambda b, i: (0, 0)),
        ],
        out_specs=pl.BlockSpec((None, tm, d), lambda b, i: (b, i, 0)),
        compiler_params=_cparams(("arbitrary", "arbitrary")),
        name="gate_outproj_residual",
    )(x, ya, proj, yb, o_dirs, o_dirs, proj, hg_norm.reshape(1, wc), w_out_bf16, mod,
      final_g.reshape(1, d))


def _block_diag(w):
    two, nh, n, _ = w.shape
    eye = jnp.eye(nh, dtype=w.dtype)
    dense = jnp.einsum("dhij,hg->dhigj", w, eye).reshape(two, nh * n, nh * n)
    return dense.astype(BF16)


def _mixers(x, mod, norm_g, lp, init, grid_rows, with_output):
    bsz, t, d = x.shape
    wl = lp["lru_w"]
    wh = lp["hy_w"]
    wg = lp["hg_w"]
    if grid_rows is None:
        proj = _in_projection(x.reshape(1, bsz * t, d), mod[:1], norm_g, lp["w_in"], *INPROJ_CTX_TILE)
        proj = proj.reshape(bsz, t, -1)
    else:
        proj = _in_projection(x, mod, norm_g, lp["w_in"], *INPROJ_TILE)
    xa = proj[:, :, :wl].astype(F32)
    if grid_rows is not None:
        x_tm = xa.reshape(bsz, grid_rows, GRID_W, wl).transpose(2, 1, 0, 3).reshape(t, bsz, wl)
    else:
        x_tm = xa.transpose(1, 0, 2)
    h_dirs, h_fin = _rglru(x_tm, init[0], lp["lru_conv_w"], lp["lru_conv_b"], lp["wa_dense"],
                           lp["lru_ba"], lp["wx_dense"], lp["lru_bx"], lp["lru_lam"], tp=LRU_STEPS)
    q_off = 2 * wl + 4 * wh
    o_dirs, s_fin = _hgrn2(proj, q_off // wg, q_off // wg + 1, q_off // wg + 3, lp["hg_lb"], init[1],
                           tT=HG_ROWS)
    states = (h_fin, s_fin)
    if not with_output:
        return None, states
    ya_tm = h_dirs[0] + h_dirs[1]
    if grid_rows is not None:
        ya = ya_tm.reshape(GRID_W, grid_rows, bsz, wl).transpose(2, 1, 0, 3).reshape(bsz, t, wl)
    else:
        ya = ya_tm.transpose(1, 0, 2)
    tables = lp["dft"][t]
    spectra = _hyena_spectra(t, tables, lp["hy_w1"], lp["hy_b1"], lp["hy_w2"],
                             lp["hy_b2"], lp["hy_w3"], lp["hy_freq"], wh, tl=HY_FILTER_ROWS)
    cb = min(wh, HY_CHANNELS)
    ub_blk = (2 * wl) // cb
    nb = wh // cb
    cw, cbias = lp["hy_conv_w"], lp["hy_conv_b"].reshape(1, -1)
    z = _hyena_conv(proj, ub_blk, (cw, cbias, 0), proj, ub_blk + nb, (cw, cbias, nb), None, 0,
                    lp["hy_skip"], tables, spectra, 0, wh, cb, F32)
    yb = _hyena_conv(z, 0, None, proj, ub_blk + 2 * nb, (cw, cbias, 2 * nb), proj, ub_blk + 3 * nb,
                     lp["hy_skip"], tables, spectra, 1, wh, cb, BF16)
    return (ya, yb, o_dirs, proj), states


def kernel(x, c, ctx, c_ctx, norm_g, w_mod, b_mod, w_in, w_out, lru_conv_w, lru_conv_b, lru_wa, lru_ba,
           lru_wx, lru_bx, lru_lam, hy_conv_w, hy_conv_b, hy_w1, hy_b1, hy_w2, hy_b2, hy_w3, hy_freq,
           hy_skip, hg_lb_logits, hg_norm, final_g):
    bsz, seq, d = x.shape
    depth = w_mod.shape[0]
    ctx_len = ctx.shape[1]
    rows = seq // GRID_W
    wl = lru_lam.shape[-1]
    wh = hy_skip.shape[-1]
    wg = hg_norm.shape[-1]
    dk = wg // HG_HEADS

    p = jax.nn.softmax(hg_lb_logits.astype(F32), axis=0)
    lower_bounds = jnp.cumsum(p, axis=0) - p[0]

    n_rows = -(-(bsz + 1) // SUBLANES) * SUBLANES
    cvec = jnp.zeros((n_rows, d), F32).at[:bsz].set(c).at[bsz].set(c_ctx)
    mod_all = _modulation(cvec, w_mod, b_mod)

    dft = {length: _hyena_tables(length) for length in (seq, ctx_len)}

    zero_states = (jnp.zeros((2, bsz, wl), F32), jnp.zeros((2, bsz, HG_HEADS, dk, dk), F32))
    xc = ctx
    for layer in range(depth):
        last = layer == depth - 1
        mod = mod_all[layer, :bsz].reshape(bsz, 3, d)
        mod_c = jnp.broadcast_to(mod_all[layer, bsz].reshape(1, 3, d), (bsz, 3, d))
        lp = dict(
            w_in=w_in[layer].astype(BF16), lru_w=wl, hy_w=wh, hg_w=wg,
            lru_conv_w=lru_conv_w[layer], lru_conv_b=lru_conv_b[layer],
            wa_dense=_block_diag(lru_wa[layer]), wx_dense=_block_diag(lru_wx[layer]),
            lru_ba=lru_ba[layer], lru_bx=lru_bx[layer], lru_lam=lru_lam[layer],
            hy_conv_w=hy_conv_w[layer], hy_conv_b=hy_conv_b[layer], hy_w1=hy_w1[layer],
            hy_b1=hy_b1[layer], hy_w2=hy_w2[layer], hy_b2=hy_b2[layer], hy_w3=hy_w3[layer],
            hy_freq=hy_freq[layer], hy_skip=hy_skip[layer].reshape(HY_ORDER, 1, wh),
            hg_lb=lower_bounds[layer], dft=dft)
        w_out_l = w_out[layer].astype(BF16)
        ga_blk = 1
        gc_blk = (2 * wl + 4 * wh + 4 * wg) // wg
        pieces_c, ctx_states = _mixers(xc, mod_c, norm_g[layer], lp, zero_states, None, not last)
        pieces, _ = _mixers(x, mod, norm_g[layer], lp, ctx_states, rows, True)
        ya, yb, o_dirs, proj = pieces
        x = _out_projection(x, ya, proj, ga_blk, yb, o_dirs, gc_blk, hg_norm[layer], w_out_l, mod,
                            final_g, last, tm=OUTPROJ_ROWS)
        if not last:
            ya, yb, o_dirs, proj = pieces_c
            xc = _out_projection(xc, ya, proj, ga_blk, yb, o_dirs, gc_blk, hg_norm[layer], w_out_l,
                                 mod_c, final_g, False, tm=OUTPROJ_ROWS)
    return x
```

```python
import functools
import math

import jax
import jax.numpy as jnp
from jax import lax
from jax.experimental import pallas as pl
from jax.experimental.pallas import tpu as pltpu

GRID_W = 64
LRU_HEADS = 8
LRU_CONV = 4
LRU_C = 8.0
HY_ORDER = 2
HY_CONV = 3
HY_BANDS = 16
HY_FAST_DECAY = 0.3
HY_SLOW_DECAY = 1.5
HY_TARGET = 1e-2
HG_HEADS = 8
HG_CHUNK = 64
HG_GROUP = 8
HG_SAFE_SPAN = 60.0
EPS = 1e-6
TINY = 1e-12

LANES = 128
SUBLANES = 8
ROW_GROUP = 16
VMEM_LIMIT_BYTES = 56 * 1024 * 1024

INPROJ_TILE = (1024, 4096)
INPROJ_CTX_TILE = (2048, 2048)
LRU_STEPS = 128
HG_ROWS = 512
HY_FILTER_ROWS = 1024
HY_CHANNELS = 256
OUTPROJ_ROWS = 512

F32 = jnp.float32
BF16 = jnp.bfloat16
PROJ_DTYPE = BF16


def _cparams(semantics):
    return pltpu.CompilerParams(dimension_semantics=semantics, vmem_limit_bytes=VMEM_LIMIT_BYTES)


def _sigmoid(x):
    return 1.0 / (1.0 + jnp.exp(-x))


def _sigmoid_tanh(x):
    return 0.5 * jnp.tanh(0.5 * x) + 0.5


def _silu(x):
    return x * _sigmoid(x)


def _mod_kernel(c_ref, w_ref, b_ref, o_ref):
    s = _silu(c_ref[...])
    o_ref[...] = jnp.dot(s, w_ref[...], preferred_element_type=F32,
                         precision=lax.Precision.HIGHEST) + b_ref[...]


def _modulation(cvec, w_mod, b_mod):
    depth, d, d3 = w_mod.shape
    r = cvec.shape[0]
    return pl.pallas_call(
        _mod_kernel,
        out_shape=jax.ShapeDtypeStruct((depth, r, d3), F32),
        grid=(depth, d3 // d),
        in_specs=[
            pl.BlockSpec((r, d), lambda l, j: (0, 0)),
            pl.BlockSpec((None, d, d), lambda l, j: (l, 0, j)),
            pl.BlockSpec((None, 1, d), lambda l, j: (l, 0, j)),
        ],
        out_specs=pl.BlockSpec((None, r, d), lambda l, j: (l, 0, j)),
        compiler_params=_cparams(("arbitrary", "arbitrary")),
        name="adaln_modulation",
    )(cvec, w_mod, b_mod.reshape(depth, 1, d3))


def _inproj_kernel(x_ref, mod_ref, g_ref, w_ref, o_ref, h_s):
    @pl.when(pl.program_id(2) == 0)
    def _():
        x = x_ref[...]
        ms = jnp.mean(x * x, axis=-1, keepdims=True)
        y = x * lax.rsqrt(ms + EPS) * g_ref[...]
        h = y * (1.0 + mod_ref[1:2, :]) + mod_ref[0:1, :]
        h_s[...] = h.astype(BF16)

    o_ref[...] = jnp.dot(h_s[...], w_ref[...], preferred_element_type=F32).astype(o_ref.dtype)


def _in_projection(x, mod, gain, w_bf16, tm, tn):
    bsz, t, d = x.shape
    d_in = w_bf16.shape[1]
    tm = min(tm, t)
    tn = min(tn, d_in)
    return pl.pallas_call(
        _inproj_kernel,
        out_shape=jax.ShapeDtypeStruct((bsz, t, d_in), PROJ_DTYPE),
        grid=(bsz, t // tm, d_in // tn),
        in_specs=[
            pl.BlockSpec((None, tm, d), lambda b, i, j: (b, i, 0)),
            pl.BlockSpec((None, 3, d), lambda b, i, j: (b, 0, 0)),
            pl.BlockSpec((1, d), lambda b, i, j: (0, 0)),
            pl.BlockSpec((d, tn), lambda b, i, j: (0, j)),
        ],
        out_specs=pl.BlockSpec((None, tm, tn), lambda b, i, j: (b, i, j)),
        scratch_shapes=[pltpu.VMEM((tm, d), BF16)],
        compiler_params=_cparams(("arbitrary", "arbitrary", "arbitrary")),
        name="rmsnorm_adaln_inproj",
    )(x, mod, gain.reshape(1, d), w_bf16)


def _lru_kernel(xprev_ref, x_ref, xnext_ref, h0_ref, cw_ref, cb_ref, wa_ref, ba_ref, wx_ref,
                bx_ref, lam_ref, h_ref, hfin_ref, xpad, a_s, b_s, hstate):
    d = pl.program_id(0)
    i = pl.program_id(1)
    n = pl.num_programs(1)
    idx = jnp.where(d == 0, i, n - 1 - i)
    tp, bsz, ch = x_ref.shape

    @pl.when(i == 0)
    def _():
        hstate[...] = h0_ref[...]

    xpad[0:2] = jnp.where(idx > 0, xprev_ref[...], 0.0)
    xpad[2:2 + tp] = x_ref[...]
    xpad[2 + tp:3 + tp] = jnp.where(idx < n - 1, xnext_ref[...], 0.0)
    u = cb_ref[...] + cw_ref[0:1, :] * xpad[0:tp]
    for k in range(1, LRU_CONV):
        u = u + cw_ref[k:k + 1, :] * xpad[k:k + tp]
    u2 = u.reshape(tp * bsz, ch)
    ub = u2.astype(BF16)
    r = _sigmoid_tanh(jnp.dot(ub, wa_ref[...], preferred_element_type=F32) + ba_ref[...])
    gi = _sigmoid_tanh(jnp.dot(ub, wx_ref[...], preferred_element_type=F32) + bx_ref[...])
    nlam = -lam_ref[...]
    softplus = jnp.maximum(nlam, 0.0) + jnp.log(1.0 + jnp.exp(-jnp.abs(nlam)))
    a = jnp.exp(r * ((-LRU_C) * softplus))
    var = jnp.maximum(1.0 - a * a, TINY)
    bb = (var * lax.rsqrt(var)) * (gi * u2)
    a_s[...] = a.reshape(tp, bsz, ch)
    b_s[...] = bb.reshape(tp, bsz, ch)

    def body(t, h):
        tt = jnp.where(d == 0, t, tp - 1 - t)
        h = a_s[tt] * h + b_s[tt]
        h_ref[tt] = h
        return h

    h = lax.fori_loop(0, tp, body, hstate[...], unroll=8)
    hstate[...] = h
    hfin_ref[...] = h


def _rglru(x_tm, h0, conv_w, conv_b, wa_dense, ba, wx_dense, bx, lam, tp):
    p, bsz, ch = x_tm.shape
    tp = min(tp, p)
    n = p // tp

    def tile(d, i):
        return jnp.where(d == 0, i, n - 1 - i)

    vec = lambda a: a.reshape(2, 1, ch)
    return pl.pallas_call(
        _lru_kernel,
        out_shape=(jax.ShapeDtypeStruct((2, p, bsz, ch), F32),
                   jax.ShapeDtypeStruct((2, bsz, ch), F32)),
        grid=(2, n),
        in_specs=[
            pl.BlockSpec((2, bsz, ch), lambda d, i: (jnp.maximum(tile(d, i) * (tp // 2) - 1, 0), 0, 0)),
            pl.BlockSpec((tp, bsz, ch), lambda d, i: (tile(d, i), 0, 0)),
            pl.BlockSpec((1, bsz, ch), lambda d, i: (jnp.minimum((tile(d, i) + 1) * tp, p - 1), 0, 0)),
            pl.BlockSpec((None, bsz, ch), lambda d, i: (d, 0, 0)),
            pl.BlockSpec((LRU_CONV, ch), lambda d, i: (0, 0)),
            pl.BlockSpec((1, ch), lambda d, i: (0, 0)),
            pl.BlockSpec((None, ch, ch), lambda d, i: (d, 0, 0)),
            pl.BlockSpec((None, 1, ch), lambda d, i: (d, 0, 0)),
            pl.BlockSpec((None, ch, ch), lambda d, i: (d, 0, 0)),
            pl.BlockSpec((None, 1, ch), lambda d, i: (d, 0, 0)),
            pl.BlockSpec((None, 1, ch), lambda d, i: (d, 0, 0)),
        ],
        out_specs=(
            pl.BlockSpec((None, tp, bsz, ch), lambda d, i: (d, tile(d, i), 0, 0)),
            pl.BlockSpec((None, bsz, ch), lambda d, i: (d, 0, 0)),
        ),
        scratch_shapes=[
            pltpu.VMEM((tp + LRU_CONV - 1, bsz, ch), F32),
            pltpu.VMEM((tp, bsz, ch), F32),
            pltpu.VMEM((tp, bsz, ch), F32),
            pltpu.VMEM((bsz, ch), F32),
        ],
        compiler_params=_cparams(("arbitrary", "arbitrary")),
        name="rglru_bidir_scan",
    )(x_tm, x_tm, x_tm, h0, conv_w, conv_b.reshape(1, ch), wa_dense, vec(ba), wx_dense, vec(bx),
      vec(lam))


def _hgrn2_kernel(q_ref, z_ref, v_ref, lb_ref, s0_ref, o_ref, sfin_ref, *states):
    d = pl.program_id(0)
    i = pl.program_id(2)
    tT = q_ref.shape[0]
    nh, dv, dk = states[0].shape
    c = HG_CHUNK
    nch = tT // c
    group = len(states) - 1

    @pl.when(i == 0)
    def _():
        states[0][...] = s0_ref[...]

    row = lax.broadcasted_iota(jnp.int32, (c, c), 0)
    col = lax.broadcasted_iota(jnp.int32, (c, c), 1)
    fwd = d == 0
    keep = (col - row) * jnp.where(fwd, 1, -1) <= 0
    tri = jnp.where(keep, 1.0, 0.0).astype(BF16)
    tri2 = jnp.concatenate([tri, tri], axis=1)
    nt_dims = (((1,), (1,)), ((), ()))
    tn_dims = (((0,), (0,)), ((), ()))

    def gates(z, lb):
        e = jnp.exp(-jnp.abs(z))
        s_big = 1.0 / (1.0 + e)
        sig = jnp.where(z >= 0, s_big, e * s_big)
        f = lb + (1.0 - lb) * sig
        return jnp.maximum(f, TINY), 1.0 - f

    heads = [(slice(h * dk, (h + 1) * dk), slice(h * dv, (h + 1) * dv)) for h in range(nh)]

    def chunk_base(j):
        return pl.multiple_of(jnp.where(fwd, j, nch - 1 - j) * c, c)

    def stage_a(rows):
        out = []
        for sl, _ in heads:
            f, kk = gates(z_ref[rows, sl].astype(F32), lb_ref[:, sl])
            g = jnp.log(f)
            g1 = g.astype(BF16)
            g2 = (g - g1.astype(F32)).astype(BF16)
            gc = jnp.dot(tri2, jnp.concatenate([g1, g2], axis=0), preferred_element_type=F32)
            out.append((kk, gc))
        return out

    def stage_b(rows, from_a, src, dst):
        out = []
        span = jnp.zeros((1, dk), F32)
        for h, ((sl, sv), (kk, gc)) in enumerate(zip(heads, from_a)):
            ref_row = gc[c // 2:c // 2 + 1, :]
            tot = jnp.where(fwd, gc[c - 1:c, :], gc[0:1, :])
            span = jnp.maximum(span, jnp.maximum(jnp.abs(gc[0:1, :] - ref_row),
                                                 jnp.abs(gc[c - 1:c, :] - ref_row)))
            e_q = jnp.exp(gc - ref_row)
            qp = q_ref[rows, sl].astype(F32) * e_q
            kp = kk * (1.0 / e_q)
            qpp = (qp * jnp.exp(ref_row)).astype(BF16)
            kpp = (kp * jnp.exp(tot - ref_row)).astype(BF16)
            sc = lax.dot_general(qp.astype(BF16), kp.astype(BF16), nt_dims, preferred_element_type=F32)
            s_t = src[h]
            inter = lax.dot_general(qpp, s_t.astype(BF16), nt_dims, preferred_element_type=F32)
            upd = lax.dot_general(v_ref[rows, sv].astype(BF16), kpp, tn_dims, preferred_element_type=F32)
            dst[h] = s_t * jnp.exp(tot) + upd
            out.append((sc, inter))
        return out, span

    def stage_c(rows, from_b):
        for (_, sv), (sc, inter) in zip(heads, from_b):
            sc = jnp.where(keep, sc, 0.0).astype(BF16)
            o_ref[rows, sv] = inter + jnp.dot(sc, v_ref[rows, sv].astype(BF16), preferred_element_type=F32)

    def exact_chunk(base, src, dst):
        sub = lax.broadcasted_iota(jnp.int32, (ROW_GROUP, 1), 0)
        dst[...] = src[...]

        def step(u, carry):
            t = base + jnp.where(fwd, u, c - 1 - u)
            grp = pl.ds(pl.multiple_of((t // ROW_GROUP) * ROW_GROUP, ROW_GROUP), ROW_GROUP)
            live = sub == t % ROW_GROUP
            for h, (sl, sv) in enumerate(heads):
                f_g, k_g = gates(z_ref[grp, sl].astype(F32), lb_ref[:, sl])
                f = jnp.sum(jnp.where(live, f_g, 0.0), axis=0, keepdims=True)
                v_g = jnp.where(live, v_ref[grp, sv].astype(F32), 0.0)
                s_new = dst[h] * f + lax.dot_general(v_g, k_g, tn_dims, preferred_element_type=F32)
                dst[h] = s_new
                o_g = lax.dot_general(q_ref[grp, sl].astype(F32), s_new, nt_dims,
                                      preferred_element_type=F32)
                o_ref[grp, sv] = jnp.where(live, o_g, o_ref[grp, sv])
            return carry

        lax.fori_loop(0, c, step, 0)

    def chunk_group(jj, carry):
        bases = [chunk_base(group * jj + g) for g in range(group)]
        rows = [pl.ds(b, c) for b in bases]
        span = jnp.zeros((1, dk), F32)
        from_a = stage_a(rows[0])
        for g in range(group):
            from_b, span_g = stage_b(rows[g], from_a, states[g], states[g + 1])
            span = jnp.maximum(span, span_g)
            if g + 1 < group:
                from_a = stage_a(rows[g + 1])
            stage_c(rows[g], from_b)

        @pl.when(jnp.max(span) > HG_SAFE_SPAN)
        def _():
            for g in range(group):
                exact_chunk(bases[g], states[g], states[g + 1])

        states[0][...] = states[group][...]
        return carry

    lax.fori_loop(0, nch // group, chunk_group, 0)
    sfin_ref[...] = states[0][...]


def _hgrn2(proj, q_blk, z_blk, v_blk, lb, s0, tT):
    bsz, t, _ = proj.shape
    _, _, nh, dv, dk = s0.shape
    w = nh * dk
    tT = min(tT, t)
    n = t // tT
    nch = tT // HG_CHUNK
    group = max(g for g in (1, 2, 4, HG_GROUP) if nch % g == 0)

    def tile(d, i):
        return jnp.where(d == 0, i, n - 1 - i)

    return pl.pallas_call(
        _hgrn2_kernel,
        out_shape=(jax.ShapeDtypeStruct((2, bsz, t, nh * dv), F32),
                   jax.ShapeDtypeStruct((2, bsz, nh, dv, dk), F32)),
        grid=(2, bsz, n),
        in_specs=[
            pl.BlockSpec((None, tT, w), lambda d, b, i: (b, tile(d, i), q_blk)),
            pl.BlockSpec((None, tT, w), lambda d, b, i: (b, tile(d, i), z_blk + d)),
            pl.BlockSpec((None, tT, w), lambda d, b, i: (b, tile(d, i), v_blk)),
            pl.BlockSpec((None, 1, w), lambda d, b, i: (d, 0, 0)),
            pl.BlockSpec((None, None, nh, dv, dk), lambda d, b, i: (d, b, 0, 0, 0)),
        ],
        out_specs=(
            pl.BlockSpec((None, None, tT, nh * dv), lambda d, b, i: (d, b, tile(d, i), 0)),
            pl.BlockSpec((None, None, nh, dv, dk), lambda d, b, i: (d, b, 0, 0, 0)),
        ),
        scratch_shapes=[pltpu.VMEM((nh, dv, dk), F32)] * (group + 1),
        compiler_params=_cparams(("arbitrary", "arbitrary", "arbitrary")),
        name="hgrn2_bidir_chunk_scan",
    )(proj, proj, proj, lb.reshape(2, 1, w), s0)


HY_MAX_RADIX = 8
HY_MIN_SUBLEN = 128
HY_CHUNK = 8


def _hyena_radix(length):
    return max(1, min(HY_MAX_RADIX, length // HY_MIN_SUBLEN))


def _hyena_tables(length):
    radix = _hyena_radix(length)
    n = length // radix
    k2 = 2 * jnp.arange(n, dtype=jnp.int32) + 1
    ph = (k2[:, None] * jnp.arange(n, dtype=jnp.int32)[None, :]) % (4 * n)
    psi = ph.astype(F32) * (2.0 * math.pi / (4 * n))
    fc, fs = jnp.cos(psi), jnp.sin(psi)
    fwd = jnp.concatenate([fc, fs], axis=0).astype(BF16)
    inv = jnp.concatenate([fc.T, -fs.T], axis=1).astype(BF16)
    phr = (jnp.arange(radix, dtype=jnp.int32)[:, None] * k2[None, :]) % (4 * length)
    phi = phr.astype(F32) * (2.0 * math.pi / (4 * length))
    rep = lambda a: jnp.broadcast_to(a[:, :, None], (radix, n, LANES))
    return fwd, inv, rep(jnp.cos(phi)), rep(jnp.sin(phi))


def _cmul_const(x, c, s):
    re, im = x
    if abs(s) < 1e-12:
        return (re, im) if c > 0 else (-re, -im)
    if abs(c) < 1e-12:
        return (-im, re) if s > 0 else (im, -re)
    return (re * c - im * s, re * s + im * c)


def _fft_list(xs, sign):
    r = len(xs)
    if r == 1:
        return xs
    ev = _fft_list(xs[0::2], sign)
    od = _fft_list(xs[1::2], sign)
    out = [None] * r
    for j in range(r // 2):
        ang = sign * 2.0 * math.pi * j / r
        t = _cmul_const(od[j], math.cos(ang), math.sin(ang))
        out[j] = (ev[j][0] + t[0], ev[j][1] + t[1])
        out[j + r // 2] = (ev[j][0] - t[0], ev[j][1] - t[1])
    return out


def _hy_sub_transforms(slab, fwd_ref, z_ref, radix, n):
    n_slabs = slab.shape[0]
    for r in range(radix):
        parts = [slab[s, pl.ds(r, n, stride=radix), :] if radix > 1 else slab[s]
                 for s in range(n_slabs)]
        xr = parts[0] if n_slabs == 1 else jnp.concatenate(parts, axis=1)
        z_ref[r] = jnp.dot(fwd_ref[...], xr.astype(BF16), preferred_element_type=F32)


def _hy_load_twiddles(twr_ref, twi_ref, rows_c, radix):
    return [(twr_ref[r, rows_c, :], twi_ref[r, rows_c, :]) for r in range(1, radix)]


def _hy_twiddled_inputs(z_ref, tw, rows_c, rows_s, lanes, radix):
    xs = []
    for r in range(radix):
        c = z_ref[r, rows_c, lanes]
        s = z_ref[r, rows_s, lanes]
        if r == 0:
            xs.append((c, -s))
        else:
            wr, wi = tw[r - 1]
            xs.append((c * wr - s * wi, -(c * wi + s * wr)))
    return xs


def _filter_kernel(emb_ref, w1_ref, b1_ref, w2_ref, b2_ref, w3_ref, fr_ref, dl_ref,
                   a_ref, b_ref, ss_ref, *, length):
    i = pl.program_id(0)
    tl = emb_ref.shape[0]
    cw = dl_ref.shape[1]
    fr = fr_ref[...]
    hi = lax.Precision.HIGHEST
    hdn = jnp.sin(fr * (jnp.dot(emb_ref[...], w1_ref[...], preferred_element_type=F32,
                                precision=hi) + b1_ref[...]))
    hdn = jnp.sin(fr * (jnp.dot(hdn, w2_ref[...], preferred_element_type=F32,
                                precision=hi) + b2_ref[...]))
    h = jnp.dot(hdn.astype(BF16), w3_ref[...].astype(BF16), preferred_element_type=F32)
    pos = (lax.broadcasted_iota(jnp.int32, (tl, 1), 0) + i * tl)
    t = pos.astype(F32) / float(length)
    win = jnp.exp(-t * dl_ref[...])
    not_first = pos > 0

    @pl.when(i == 0)
    def _():
        ss_ref[...] = jnp.zeros_like(ss_ref)

    for o in range(HY_ORDER):
        hpos = h[:, (2 * o) * cw:(2 * o + 1) * cw] * win
        hneg = jnp.where(not_first, h[:, (2 * o + 1) * cw:(2 * o + 2) * cw] * win, 0.0)
        a_ref[:, o * cw:(o + 1) * cw] = hpos + hneg
        b_ref[:, o * cw:(o + 1) * cw] = hneg - hpos
        ss_ref[:, o * cw:(o + 1) * cw] += jnp.sum(hpos * hpos + hneg * hneg, axis=0, keepdims=True)


def _spectrum_kernel(a_ref, b_ref, ss_ref, fwd_ref, twr_ref, twi_ref, hr_ref, hi_ref, slab, zab,
                     *, scale):
    radix, n, _ = hr_ref.shape
    slab[0] = a_ref[...]
    slab[1] = b_ref[...]
    _hy_sub_transforms(slab, fwd_ref, zab, radix, n)
    norm = lax.rsqrt(ss_ref[...] + TINY) * scale
    lanes_a, lanes_b = slice(0, LANES), slice(LANES, 2 * LANES)

    def step(i, carry):
        base = pl.multiple_of(i * HY_CHUNK, HY_CHUNK)
        rows_c = pl.ds(base, HY_CHUNK)
        rows_s = pl.ds(base + n, HY_CHUNK)
        tw = _hy_load_twiddles(twr_ref, twi_ref, rows_c, radix)
        fa = _fft_list(_hy_twiddled_inputs(zab, tw, rows_c, rows_s, lanes_a, radix), -1)
        fb = _fft_list(_hy_twiddled_inputs(zab, tw, rows_c, rows_s, lanes_b, radix), -1)
        for j in range(radix):
            hr_ref[j, rows_c, :] = fa[j][0] * norm
            hi_ref[j, rows_c, :] = -fb[j][1] * norm
        return carry

    lax.fori_loop(0, n // HY_CHUNK, step, 0)


def _hyena_spectra(length, tables, w1, b1, w2, b2, w3, freq, width, tl):
    f32 = F32
    pos = jnp.arange(length, dtype=f32)
    t = pos / length
    bands = jnp.linspace(1e-4, HY_BANDS - 1, HY_BANDS, dtype=f32)
    ang = (2.0 * math.pi / length) * pos[:, None] * bands[None, :]
    emb = jnp.concatenate([t[:, None], jnp.cos(ang), jnp.sin(ang)], axis=-1)
    deltas = jnp.abs(jnp.linspace(math.log(HY_TARGET) / HY_SLOW_DECAY,
                                  math.log(HY_TARGET) / HY_FAST_DECAY, width, dtype=f32))
    n_emb = -(-emb.shape[1] // LANES) * LANES
    w1 = jnp.pad(w1.astype(f32), ((0, n_emb - emb.shape[1]), (0, 0)))
    emb = jnp.pad(emb, ((0, 0), (0, n_emb - emb.shape[1])))
    hid = w1.shape[1]
    cols = HY_ORDER * width
    tl = min(tl, length)
    full = lambda *s: pl.BlockSpec(s, lambda i: (0,) * len(s))
    a_un, b_un, ss = pl.pallas_call(
        functools.partial(_filter_kernel, length=length),
        out_shape=(jax.ShapeDtypeStruct((length, cols), f32),
                   jax.ShapeDtypeStruct((length, cols), f32),
                   jax.ShapeDtypeStruct((1, cols), f32)),
        grid=(length // tl,),
        in_specs=[
            pl.BlockSpec((tl, n_emb), lambda i: (i, 0)),
            full(n_emb, hid), full(1, hid), full(hid, hid), full(1, hid),
            full(hid, 2 * cols), full(1, hid), full(1, width),
        ],
        out_specs=(pl.BlockSpec((tl, cols), lambda i: (i, 0)),
                   pl.BlockSpec((tl, cols), lambda i: (i, 0)),
                   pl.BlockSpec((1, cols), lambda i: (0, 0))),
        compiler_params=_cparams(("arbitrary",)),
        name="hyena_filter_mlp",
    )(emb, w1, b1.reshape(1, hid), w2, b2.reshape(1, hid), w3, freq.reshape(1, hid),
      deltas.reshape(1, width))
    fwd, _, twr, twi = tables
    radix, n, _ = twr.shape
    cb = LANES
    return pl.pallas_call(
        functools.partial(_spectrum_kernel, scale=1.0 / length),
        out_shape=(jax.ShapeDtypeStruct((radix, n, cols), f32),
                   jax.ShapeDtypeStruct((radix, n, cols), f32)),
        grid=(cols // cb,),
        in_specs=[
            pl.BlockSpec((length, cb), lambda j: (0, j)),
            pl.BlockSpec((length, cb), lambda j: (0, j)),
            pl.BlockSpec((1, cb), lambda j: (0, j)),
            pl.BlockSpec((2 * n, n), lambda j: (0, 0)),
            pl.BlockSpec((radix, n, LANES), lambda j: (0, 0, 0)),
            pl.BlockSpec((radix, n, LANES), lambda j: (0, 0, 0)),
        ],
        out_specs=(pl.BlockSpec((radix, n, cb), lambda j: (0, 0, j)),
                   pl.BlockSpec((radix, n, cb), lambda j: (0, 0, j))),
        scratch_shapes=[pltpu.VMEM((2, length, LANES), f32),
                        pltpu.VMEM((radix, 2 * n, 2 * LANES), f32)],
        compiler_params=_cparams(("arbitrary",)),
        name="hyena_filter_spectrum",
    )(a_un, b_un, ss, fwd, twr, twi)


def _short_conv(x, w_ref, b_ref):
    length = x.shape[0]
    rows = lax.broadcasted_iota(jnp.int32, (length, 1), 0)
    prev = jnp.where(rows > 0, pltpu.roll(x, 1, axis=0), 0.0)
    nxt = jnp.where(rows < length - 1, pltpu.roll(x, length - 1, axis=0), 0.0)
    return w_ref[0:1, :] * prev + w_ref[1:2, :] * x + w_ref[2:3, :] * nxt + b_ref[...]


def _hyena_conv_kernel(*refs, conv_a, has_gate, m_blk, g_blk):
    it = iter(refs)
    a_ref = next(it)
    aw_ref = ab_ref = None
    if conv_a:
        aw_ref, ab_ref = next(it), next(it)
    m_hbm, mw_ref, mb_ref = next(it), next(it), next(it)
    g_hbm = next(it) if has_gate else None
    skip_ref, fwd_ref, inv_ref, twr_ref, twi_ref, hr_ref, hi_ref = (next(it) for _ in range(7))
    o_ref = next(it)
    slab, z_s = next(it), next(it)
    a_keep = next(it) if conv_a else None
    m_ref = next(it)
    g_ref = next(it) if has_gate else None
    sems = next(it)
    radix, n, cb = hr_ref.shape
    tiles = [slice(t * LANES, (t + 1) * LANES) for t in range(cb // LANES)]

    def side_copies():
        c, b = pl.program_id(0), pl.program_id(1)
        cps = [pltpu.make_async_copy(m_hbm.at[b, :, pl.ds((m_blk + c) * cb, cb)], m_ref, sems.at[0])]
        if has_gate:
            cps.append(pltpu.make_async_copy(g_hbm.at[b, :, pl.ds((g_blk + c) * cb, cb)], g_ref,
                                             sems.at[1]))
        return cps

    for cp in side_copies():
        cp.start()

    def a_value():
        a = a_ref[...].astype(F32)
        return _short_conv(a, aw_ref, ab_ref) if conv_a else a

    a = a_value()
    if conv_a:
        a_keep[...] = a
    for s, lanes in enumerate(tiles):
        slab[s] = a[:, lanes]
    _hy_sub_transforms(slab, fwd_ref, z_s, radix, n)

    def step(i, carry):
        base = pl.multiple_of(i * HY_CHUNK, HY_CHUNK)
        rows_c = pl.ds(base, HY_CHUNK)
        rows_s = pl.ds(base + n, HY_CHUNK)
        tw = _hy_load_twiddles(twr_ref, twi_ref, rows_c, radix)
        for lanes in tiles:
            zf = _fft_list(_hy_twiddled_inputs(z_s, tw, rows_c, rows_s, lanes, radix), -1)
            ys = []
            for j in range(radix):
                hr = hr_ref[j, rows_c, lanes]
                hi = hi_ref[j, rows_c, lanes]
                zr, zi = zf[j]
                ys.append((zr * hr - zi * hi, zr * hi + zi * hr))
            vs = _fft_list(ys, +1)
            for r in range(radix):
                ur, ui = vs[r]
                if r > 0:
                    wr, wi = tw[r - 1]
                    ur, ui = wr * ur - wi * ui, wr * ui + wi * ur
                z_s[r, rows_c, lanes] = ur
                z_s[r, rows_s, lanes] = ui
        return carry

    lax.fori_loop(0, n // HY_CHUNK, step, 0)

    for r in range(radix):
        yr = jnp.dot(inv_ref[...], z_s[r].astype(BF16), preferred_element_type=F32)
        for s, lanes in enumerate(tiles):
            if radix > 1:
                slab[s, pl.ds(r, n, stride=radix), :] = yr[:, lanes]
            else:
                slab[s] = yr[:, lanes]

    for cp in side_copies():
        cp.wait()
    a = a_keep[...] if conv_a else a_value()
    m = _short_conv(m_ref[...].astype(F32), mw_ref, mb_ref)
    for s, lanes in enumerate(tiles):
        y = m[:, lanes] * (slab[s] + skip_ref[:, lanes] * a[:, lanes])
        if has_gate:
            y = y * _silu(g_ref[:, lanes].astype(F32))
        o_ref[:, lanes] = y.astype(o_ref.dtype)


def _hyena_conv(a_src, a_blk, a_conv, m_src, m_blk, m_conv, g_src, g_blk, skip, tables, spectra,
                order, width, cb, out_dtype):
    bsz, length, _ = m_src.shape
    fwd, inv, twr, twi = tables
    hr, hi = spectra
    radix, n, _ = twr.shape
    nc = width // cb
    conv_a = a_conv is not None
    has_gate = g_src is not None
    once = dict(pipeline_mode=pl.Buffered(1))
    inputs, specs = [], []

    def add(arr, spec):
        inputs.append(arr)
        specs.append(spec)

    add(a_src, pl.BlockSpec((None, length, cb), lambda c, b: (b, 0, a_blk + c)))
    if conv_a:
        add(a_conv[0], pl.BlockSpec((HY_CONV, cb), lambda c, b: (0, a_conv[2] + c)))
        add(a_conv[1], pl.BlockSpec((1, cb), lambda c, b: (0, a_conv[2] + c)))
    add(m_src, pl.BlockSpec(memory_space=pl.ANY))
    add(m_conv[0], pl.BlockSpec((HY_CONV, cb), lambda c, b: (0, m_conv[2] + c)))
    add(m_conv[1], pl.BlockSpec((1, cb), lambda c, b: (0, m_conv[2] + c)))
    if has_gate:
        add(g_src, pl.BlockSpec(memory_space=pl.ANY))
    add(skip, pl.BlockSpec((None, 1, cb), lambda c, b: (order, 0, c)))
    add(fwd, pl.BlockSpec((2 * n, n), lambda c, b: (0, 0), **once))
    add(inv, pl.BlockSpec((n, 2 * n), lambda c, b: (0, 0), **once))
    add(twr, pl.BlockSpec((radix, n, LANES), lambda c, b: (0, 0, 0), **once))
    add(twi, pl.BlockSpec((radix, n, LANES), lambda c, b: (0, 0, 0), **once))
    add(hr, pl.BlockSpec((radix, n, cb), lambda c, b: (0, 0, order * nc + c), **once))
    add(hi, pl.BlockSpec((radix, n, cb), lambda c, b: (0, 0, order * nc + c), **once))
    side_bufs = [pltpu.VMEM((length, cb), m_src.dtype)] * (2 if has_gate else 1)
    return pl.pallas_call(
        functools.partial(_hyena_conv_kernel, conv_a=conv_a, has_gate=has_gate, m_blk=m_blk,
                          g_blk=g_blk),
        out_shape=jax.ShapeDtypeStruct((bsz, length, width), out_dtype),
        grid=(nc, bsz),
        in_specs=specs,
        out_specs=pl.BlockSpec((None, length, cb), lambda c, b: (b, 0, c)),
        scratch_shapes=[pltpu.VMEM((cb // LANES, length, LANES), F32),
                        pltpu.VMEM((radix, 2 * n, cb), F32)]
                       + ([pltpu.VMEM((length, cb), F32)] if conv_a else []) + side_bufs
                       + [pltpu.SemaphoreType.DMA((2,))],
        compiler_params=_cparams(("arbitrary", "arbitrary")),
        name="hyena_long_conv_o%d" % order,
    )(*inputs)


def _outproj_kernel(x_ref, ya_ref, ga_ref, yb_ref, of_ref, ob_ref, gc_ref, hgn_ref, w_ref, mod_ref,
                    fg_ref, o_ref, *, final, n_heads):
    wa = ya_ref.shape[1]
    wb = yb_ref.shape[1]
    wc = of_ref.shape[1]
    dv = wc // n_heads
    ya = (ya_ref[...].astype(F32) * _silu(ga_ref[...].astype(F32))).astype(BF16)
    acc = jnp.dot(ya, w_ref[0:wa, :], preferred_element_type=F32)
    acc = acc + jnp.dot(yb_ref[...].astype(BF16), w_ref[wa:wa + wb, :], preferred_element_type=F32)
    oc = of_ref[...] + ob_ref[...]
    gate_c = _silu(gc_ref[...].astype(F32)) * hgn_ref[...]
    for h in range(n_heads):
        sl = slice(h * dv, (h + 1) * dv)
        och = oc[:, sl]
        ms = jnp.mean(och * och, axis=-1, keepdims=True)
        ych = (och * lax.rsqrt(ms + EPS) * gate_c[:, sl]).astype(BF16)
        acc = acc + jnp.dot(ych, w_ref[wa + wb + h * dv:wa + wb + (h + 1) * dv, :],
                            preferred_element_type=F32)
    xn = x_ref[...] + mod_ref[2:3, :] * acc
    if final:
        ms = jnp.mean(xn * xn, axis=-1, keepdims=True)
        xn = xn * lax.rsqrt(ms + EPS) * fg_ref[...]
    o_ref[...] = xn


def _out_projection(x, ya, proj, ga_blk, yb, o_dirs, gc_blk, hg_norm, w_out_bf16, mod, final_g, final,
                    tm):
    bsz, t, d = x.shape
    wa = ya.shape[2]
    wb = yb.shape[2]
    wc = o_dirs.shape[3]
    tm = min(tm, t)
    return pl.pallas_call(
        functools.partial(_outproj_kernel, final=final, n_heads=HG_HEADS),
        out_shape=jax.ShapeDtypeStruct((bsz, t, d), F32),
        grid=(bsz, t // tm),
        in_specs=[
            pl.BlockSpec((None, tm, d), lambda b, i: (b, i, 0)),
            pl.BlockSpec((None, tm, wa), lambda b, i: (b, i, 0)),
            pl.BlockSpec((None, tm, wa), lambda b, i: (b, i, ga_blk)),
            pl.BlockSpec((None, tm, wb), lambda b, i: (b, i, 0)),
            pl.BlockSpec((None, None, tm, wc), lambda b, i: (0, b, i, 0)),
            pl.BlockSpec((None, None, tm, wc), lambda b, i: (1, b, i, 0)),
            pl.BlockSpec((None, tm, wc), lambda b, i: (b, i, gc_blk)),
            pl.BlockSpec((1, wc), lambda b, i: (0, 0)),
            pl.BlockSpec((wa + wb + wc, d), lambda b, i: (0, 0)),
            pl.BlockSpec((None, 3, d), lambda b, i: (b, 0, 0)),
            pl.BlockSpec((1, d), lambda b, i: (0, 0)),
        ],
        out_specs=pl.BlockSpec((None, tm, d), lambda b, i: (b, i, 0)),
        compiler_params=_cparams(("arbitrary", "arbitrary")),
        name="gate_outproj_residual",
    )(x, ya, proj, yb, o_dirs, o_dirs, proj, hg_norm.reshape(1, wc), w_out_bf16, mod,
      final_g.reshape(1, d))


def _block_diag(w):
    two, nh, n, _ = w.shape
    eye = jnp.eye(nh, dtype=w.dtype)
    dense = jnp.einsum("dhij,hg->dhigj", w, eye).reshape(two, nh * n, nh * n)
    return dense.astype(BF16)


def _mixers(x, mod, norm_g, lp, init, grid_rows, with_output):
    bsz, t, d = x.shape
    wl = lp["lru_w"]
    wh = lp["hy_w"]
    wg = lp["hg_w"]
    if grid_rows is None:
        proj = _in_projection(x.reshape(1, bsz * t, d), mod[:1], norm_g, lp["w_in"], *INPROJ_CTX_TILE)
        proj = proj.reshape(bsz, t, -1)
    else:
        proj = _in_projection(x, mod, norm_g, lp["w_in"], *INPROJ_TILE)
    xa = proj[:, :, :wl].astype(F32)
    if grid_rows is not None:
        x_tm = xa.reshape(bsz, grid_rows, GRID_W, wl).transpose(2, 1, 0, 3).reshape(t, bsz, wl)
    else:
        x_tm = xa.transpose(1, 0, 2)
    h_dirs, h_fin = _rglru(x_tm, init[0], lp["lru_conv_w"], lp["lru_conv_b"], lp["wa_dense"],
                           lp["lru_ba"], lp["wx_dense"], lp["lru_bx"], lp["lru_lam"], tp=LRU_STEPS)
    q_off = 2 * wl + 4 * wh
    o_dirs, s_fin = _hgrn2(proj, q_off // wg, q_off // wg + 1, q_off // wg + 3, lp["hg_lb"], init[1],
                           tT=HG_ROWS)
    states = (h_fin, s_fin)
    if not with_output:
        return None, states
    ya_tm = h_dirs[0] + h_dirs[1]
    if grid_rows is not None:
        ya = ya_tm.reshape(GRID_W, grid_rows, bsz, wl).transpose(2, 1, 0, 3).reshape(bsz, t, wl)
    else:
        ya = ya_tm.transpose(1, 0, 2)
    ya = ya.astype(PROJ_DTYPE)
    tables = lp["dft"][t]
    spectra = _hyena_spectra(t, tables, lp["hy_w1"], lp["hy_b1"], lp["hy_w2"],
                             lp["hy_b2"], lp["hy_w3"], lp["hy_freq"], wh, tl=HY_FILTER_ROWS)
    cb = min(wh, HY_CHANNELS)
    ub_blk = (2 * wl) // cb
    nb = wh // cb
    cw, cbias = lp["hy_conv_w"], lp["hy_conv_b"].reshape(1, -1)
    z = _hyena_conv(proj, ub_blk, (cw, cbias, 0), proj, ub_blk + nb, (cw, cbias, nb), None, 0,
                    lp["hy_skip"], tables, spectra, 0, wh, cb, F32)
    yb = _hyena_conv(z, 0, None, proj, ub_blk + 2 * nb, (cw, cbias, 2 * nb), proj, ub_blk + 3 * nb,
                     lp["hy_skip"], tables, spectra, 1, wh, cb, BF16)
    return (ya, yb, o_dirs, proj), states


def kernel(x, c, ctx, c_ctx, norm_g, w_mod, b_mod, w_in, w_out, lru_conv_w, lru_conv_b, lru_wa, lru_ba,
           lru_wx, lru_bx, lru_lam, hy_conv_w, hy_conv_b, hy_w1, hy_b1, hy_w2, hy_b2, hy_w3, hy_freq,
           hy_skip, hg_lb_logits, hg_norm, final_g):
    bsz, seq, d = x.shape
    depth = w_mod.shape[0]
    ctx_len = ctx.shape[1]
    rows = seq // GRID_W
    wl = lru_lam.shape[-1]
    wh = hy_skip.shape[-1]
    wg = hg_norm.shape[-1]
    dk = wg // HG_HEADS

    p = jax.nn.softmax(hg_lb_logits.astype(F32), axis=0)
    lower_bounds = jnp.cumsum(p, axis=0) - p[0]

    n_rows = -(-(bsz + 1) // SUBLANES) * SUBLANES
    cvec = jnp.zeros((n_rows, d), F32).at[:bsz].set(c).at[bsz].set(c_ctx)
    mod_all = _modulation(cvec, w_mod, b_mod)

    dft = {length: _hyena_tables(length) for length in (seq, ctx_len)}

    zero_states = (jnp.zeros((2, bsz, wl), F32), jnp.zeros((2, bsz, HG_HEADS, dk, dk), F32))
    xc = ctx
    for layer in range(depth):
        last = layer == depth - 1
        mod = mod_all[layer, :bsz].reshape(bsz, 3, d)
        mod_c = jnp.broadcast_to(mod_all[layer, bsz].reshape(1, 3, d), (bsz, 3, d))
        lp = dict(
            w_in=w_in[layer].astype(BF16), lru_w=wl, hy_w=wh, hg_w=wg,
            lru_conv_w=lru_conv_w[layer], lru_conv_b=lru_conv_b[layer],
            wa_dense=_block_diag(lru_wa[layer]), wx_dense=_block_diag(lru_wx[layer]),
            lru_ba=lru_ba[layer], lru_bx=lru_bx[layer], lru_lam=lru_lam[layer],
            hy_conv_w=hy_conv_w[layer], hy_conv_b=hy_conv_b[layer], hy_w1=hy_w1[layer],
            hy_b1=hy_b1[layer], hy_w2=hy_w2[layer], hy_b2=hy_b2[layer], hy_w3=hy_w3[layer],
            hy_freq=hy_freq[layer], hy_skip=hy_skip[layer].reshape(HY_ORDER, 1, wh),
            hg_lb=lower_bounds[layer], dft=dft)
        w_out_l = w_out[layer].astype(BF16)
        ga_blk = 1
        gc_blk = (2 * wl + 4 * wh + 4 * wg) // wg
        pieces_c, ctx_states = _mixers(xc, mod_c, norm_g[layer], lp, zero_states, None, not last)
        pieces, _ = _mixers(x, mod, norm_g[layer], lp, ctx_states, rows, True)
        ya, yb, o_dirs, proj = pieces
        x = _out_projection(x, ya, proj, ga_blk, yb, o_dirs, gc_blk, hg_norm[layer], w_out_l, mod,
                            final_g, last, tm=OUTPROJ_ROWS)
        if not last:
            ya, yb, o_dirs, proj = pieces_c
            xc = _out_projection(xc, ya, proj, ga_blk, yb, o_dirs, gc_blk, hg_norm[layer], w_out_l,
                                 mod_c, final_g, False, tm=OUTPROJ_ROWS)
    return x
```

```python
import functools
import math

import jax
import jax.numpy as jnp
from jax import lax
from jax.experimental import pallas as pl
from jax.experimental.pallas import tpu as pltpu

GRID_W = 64
LRU_HEADS = 8
LRU_CONV = 4
LRU_C = 8.0
HY_ORDER = 2
HY_CONV = 3
HY_BANDS = 16
HY_FAST_DECAY = 0.3
HY_SLOW_DECAY = 1.5
HY_TARGET = 1e-2
HG_HEADS = 8
HG_CHUNK = 64
HG_GROUP = 8
HG_SAFE_SPAN = 60.0
EPS = 1e-6
TINY = 1e-12

LANES = 128
SUBLANES = 8
ROW_GROUP = 16
VMEM_LIMIT_BYTES = 56 * 1024 * 1024

INPROJ_TILE = (1024, 4096)
INPROJ_CTX_TILE = (2048, 2048)
LRU_STEPS = 128
HG_ROWS = 512
HY_FILTER_ROWS = 1024
HY_CHANNELS = 256
OUTPROJ_ROWS = 512

F32 = jnp.float32
BF16 = jnp.bfloat16
PROJ_DTYPE = BF16


def _cparams(semantics):
    return pltpu.CompilerParams(dimension_semantics=semantics, vmem_limit_bytes=VMEM_LIMIT_BYTES)


def _sigmoid(x):
    return 1.0 / (1.0 + jnp.exp(-x))


def _sigmoid_tanh(x):
    return 0.5 * jnp.tanh(0.5 * x) + 0.5


def _silu(x):
    return x * _sigmoid(x)


def _mod_kernel(c_ref, w_ref, b_ref, o_ref):
    s = _silu(c_ref[...])
    o_ref[...] = jnp.dot(s, w_ref[...], preferred_element_type=F32,
                         precision=lax.Precision.HIGHEST) + b_ref[...]


def _modulation(cvec, w_mod, b_mod):
    depth, d, d3 = w_mod.shape
    r = cvec.shape[0]
    return pl.pallas_call(
        _mod_kernel,
        out_shape=jax.ShapeDtypeStruct((depth, r, d3), F32),
        grid=(depth, d3 // d),
        in_specs=[
            pl.BlockSpec((r, d), lambda l, j: (0, 0)),
            pl.BlockSpec((None, d, d), lambda l, j: (l, 0, j)),
            pl.BlockSpec((None, 1, d), lambda l, j: (l, 0, j)),
        ],
        out_specs=pl.BlockSpec((None, r, d), lambda l, j: (l, 0, j)),
        compiler_params=_cparams(("arbitrary", "arbitrary")),
        name="adaln_modulation",
    )(cvec, w_mod, b_mod.reshape(depth, 1, d3))


def _inproj_kernel(x_ref, mod_ref, g_ref, w_ref, o_ref, h_s):
    @pl.when(pl.program_id(2) == 0)
    def _():
        x = x_ref[...]
        ms = jnp.mean(x * x, axis=-1, keepdims=True)
        y = x * lax.rsqrt(ms + EPS) * g_ref[...]
        h = y * (1.0 + mod_ref[1:2, :]) + mod_ref[0:1, :]
        h_s[...] = h.astype(BF16)

    o_ref[...] = jnp.dot(h_s[...], w_ref[...], preferred_element_type=F32).astype(o_ref.dtype)


def _in_projection(x, mod, gain, w_bf16, tm, tn):
    bsz, t, d = x.shape
    d_in = w_bf16.shape[1]
    tm = min(tm, t)
    tn = min(tn, d_in)
    return pl.pallas_call(
        _inproj_kernel,
        out_shape=jax.ShapeDtypeStruct((bsz, t, d_in), PROJ_DTYPE),
        grid=(bsz, t // tm, d_in // tn),
        in_specs=[
            pl.BlockSpec((None, tm, d), lambda b, i, j: (b, i, 0)),
            pl.BlockSpec((None, 3, d), lambda b, i, j: (b, 0, 0)),
            pl.BlockSpec((1, d), lambda b, i, j: (0, 0)),
            pl.BlockSpec((d, tn), lambda b, i, j: (0, j)),
        ],
        out_specs=pl.BlockSpec((None, tm, tn), lambda b, i, j: (b, i, j)),
        scratch_shapes=[pltpu.VMEM((tm, d), BF16)],
        compiler_params=_cparams(("arbitrary", "arbitrary", "arbitrary")),
        name="rmsnorm_adaln_inproj",
    )(x, mod, gain.reshape(1, d), w_bf16)


def _lru_kernel(xprev_ref, x_ref, xnext_ref, h0_ref, cw_ref, cb_ref, wa_ref, ba_ref, wx_ref,
                bx_ref, lam_ref, h_ref, hfin_ref, xpad, a_s, b_s, hstate):
    d = pl.program_id(0)
    i = pl.program_id(1)
    n = pl.num_programs(1)
    idx = jnp.where(d == 0, i, n - 1 - i)
    tp, bsz, ch = x_ref.shape

    @pl.when(i == 0)
    def _():
        hstate[...] = h0_ref[...]

    xpad[0:2] = jnp.where(idx > 0, xprev_ref[...], 0.0)
    xpad[2:2 + tp] = x_ref[...]
    xpad[2 + tp:3 + tp] = jnp.where(idx < n - 1, xnext_ref[...], 0.0)
    u = cb_ref[...] + cw_ref[0:1, :] * xpad[0:tp]
    for k in range(1, LRU_CONV):
        u = u + cw_ref[k:k + 1, :] * xpad[k:k + tp]
    u2 = u.reshape(tp * bsz, ch)
    ub = u2.astype(BF16)
    r = _sigmoid_tanh(jnp.dot(ub, wa_ref[...], preferred_element_type=F32) + ba_ref[...])
    gi = _sigmoid_tanh(jnp.dot(ub, wx_ref[...], preferred_element_type=F32) + bx_ref[...])
    nlam = -lam_ref[...]
    softplus = jnp.maximum(nlam, 0.0) + jnp.log(1.0 + jnp.exp(-jnp.abs(nlam)))
    a = jnp.exp(r * ((-LRU_C) * softplus))
    var = jnp.maximum(1.0 - a * a, TINY)
    bb = (var * lax.rsqrt(var)) * (gi * u2)
    a_s[...] = a.reshape(tp, bsz, ch)
    b_s[...] = bb.reshape(tp, bsz, ch)

    def body(t, h):
        tt = jnp.where(d == 0, t, tp - 1 - t)
        h = a_s[tt] * h + b_s[tt]
        h_ref[tt] = h
        return h

    h = lax.fori_loop(0, tp, body, hstate[...], unroll=8)
    hstate[...] = h
    hfin_ref[...] = h


def _rglru(x_tm, h0, conv_w, conv_b, wa_dense, ba, wx_dense, bx, lam, tp):
    p, bsz, ch = x_tm.shape
    tp = min(tp, p)
    n = p // tp

    def tile(d, i):
        return jnp.where(d == 0, i, n - 1 - i)

    vec = lambda a: a.reshape(2, 1, ch)
    return pl.pallas_call(
        _lru_kernel,
        out_shape=(jax.ShapeDtypeStruct((2, p, bsz, ch), F32),
                   jax.ShapeDtypeStruct((2, bsz, ch), F32)),
        grid=(2, n),
        in_specs=[
            pl.BlockSpec((2, bsz, ch), lambda d, i: (jnp.maximum(tile(d, i) * (tp // 2) - 1, 0), 0, 0)),
            pl.BlockSpec((tp, bsz, ch), lambda d, i: (tile(d, i), 0, 0)),
            pl.BlockSpec((1, bsz, ch), lambda d, i: (jnp.minimum((tile(d, i) + 1) * tp, p - 1), 0, 0)),
            pl.BlockSpec((None, bsz, ch), lambda d, i: (d, 0, 0)),
            pl.BlockSpec((LRU_CONV, ch), lambda d, i: (0, 0)),
            pl.BlockSpec((1, ch), lambda d, i: (0, 0)),
            pl.BlockSpec((None, ch, ch), lambda d, i: (d, 0, 0)),
            pl.BlockSpec((None, 1, ch), lambda d, i: (d, 0, 0)),
            pl.BlockSpec((None, ch, ch), lambda d, i: (d, 0, 0)),
            pl.BlockSpec((None, 1, ch), lambda d, i: (d, 0, 0)),
            pl.BlockSpec((None, 1, ch), lambda d, i: (d, 0, 0)),
        ],
        out_specs=(
            pl.BlockSpec((None, tp, bsz, ch), lambda d, i: (d, tile(d, i), 0, 0)),
            pl.BlockSpec((None, bsz, ch), lambda d, i: (d, 0, 0)),
        ),
        scratch_shapes=[
            pltpu.VMEM((tp + LRU_CONV - 1, bsz, ch), F32),
            pltpu.VMEM((tp, bsz, ch), F32),
            pltpu.VMEM((tp, bsz, ch), F32),
            pltpu.VMEM((bsz, ch), F32),
        ],
        compiler_params=_cparams(("arbitrary", "arbitrary")),
        name="rglru_bidir_scan",
    )(x_tm, x_tm, x_tm, h0, conv_w, conv_b.reshape(1, ch), wa_dense, vec(ba), wx_dense, vec(bx),
      vec(lam))


def _hgrn2_kernel(q_ref, z_ref, v_ref, lb_ref, s0_ref, o_ref, sfin_ref, *states):
    d = pl.program_id(0)
    i = pl.program_id(2)
    tT = q_ref.shape[0]
    nh, dv, dk = states[0].shape
    c = HG_CHUNK
    nch = tT // c
    group = len(states) - 1

    @pl.when(i == 0)
    def _():
        states[0][...] = s0_ref[...]

    row = lax.broadcasted_iota(jnp.int32, (c, c), 0)
    col = lax.broadcasted_iota(jnp.int32, (c, c), 1)
    fwd = d == 0
    keep = (col - row) * jnp.where(fwd, 1, -1) <= 0
    tri = jnp.where(keep, 1.0, 0.0).astype(BF16)
    tri2 = jnp.concatenate([tri, tri], axis=1)
    nt_dims = (((1,), (1,)), ((), ()))
    tn_dims = (((0,), (0,)), ((), ()))

    def gates(z, lb):
        e = jnp.exp(-jnp.abs(z))
        s_big = 1.0 / (1.0 + e)
        sig = jnp.where(z >= 0, s_big, e * s_big)
        f = lb + (1.0 - lb) * sig
        return jnp.maximum(f, TINY), 1.0 - f

    heads = [(slice(h * dk, (h + 1) * dk), slice(h * dv, (h + 1) * dv)) for h in range(nh)]

    def chunk_base(j):
        return pl.multiple_of(jnp.where(fwd, j, nch - 1 - j) * c, c)

    def stage_a(rows):
        out = []
        for sl, _ in heads:
            f, kk = gates(z_ref[rows, sl].astype(F32), lb_ref[:, sl])
            g = jnp.log(f)
            g1 = g.astype(BF16)
            g2 = (g - g1.astype(F32)).astype(BF16)
            gc = jnp.dot(tri2, jnp.concatenate([g1, g2], axis=0), preferred_element_type=F32)
            out.append((kk, gc))
        return out

    def stage_b(rows, from_a, src, dst):
        out = []
        span = jnp.zeros((1, dk), F32)
        for h, ((sl, sv), (kk, gc)) in enumerate(zip(heads, from_a)):
            ref_row = gc[c // 2:c // 2 + 1, :]
            tot = jnp.where(fwd, gc[c - 1:c, :], gc[0:1, :])
            span = jnp.maximum(span, jnp.maximum(jnp.abs(gc[0:1, :] - ref_row),
                                                 jnp.abs(gc[c - 1:c, :] - ref_row)))
            e_q = jnp.exp(gc - ref_row)
            qp = q_ref[rows, sl].astype(F32) * e_q
            kp = kk * (1.0 / e_q)
            qpp = (qp * jnp.exp(ref_row)).astype(BF16)
            kpp = (kp * jnp.exp(tot - ref_row)).astype(BF16)
            sc = lax.dot_general(qp.astype(BF16), kp.astype(BF16), nt_dims, preferred_element_type=F32)
            s_t = src[h]
            inter = lax.dot_general(qpp, s_t.astype(BF16), nt_dims, preferred_element_type=F32)
            upd = lax.dot_general(v_ref[rows, sv].astype(BF16), kpp, tn_dims, preferred_element_type=F32)
            dst[h] = s_t * jnp.exp(tot) + upd
            out.append((sc, inter))
        return out, span

    def stage_c(rows, from_b):
        for (_, sv), (sc, inter) in zip(heads, from_b):
            sc = jnp.where(keep, sc, 0.0).astype(BF16)
            o = inter + jnp.dot(sc, v_ref[rows, sv].astype(BF16), preferred_element_type=F32)
            o_ref[rows, sv] = o.astype(o_ref.dtype)

    def exact_chunk(base, src, dst):
        sub = lax.broadcasted_iota(jnp.int32, (ROW_GROUP, 1), 0)
        dst[...] = src[...]

        def step(u, carry):
            t = base + jnp.where(fwd, u, c - 1 - u)
            grp = pl.ds(pl.multiple_of((t // ROW_GROUP) * ROW_GROUP, ROW_GROUP), ROW_GROUP)
            live = sub == t % ROW_GROUP
            for h, (sl, sv) in enumerate(heads):
                f_g, k_g = gates(z_ref[grp, sl].astype(F32), lb_ref[:, sl])
                f = jnp.sum(jnp.where(live, f_g, 0.0), axis=0, keepdims=True)
                v_g = jnp.where(live, v_ref[grp, sv].astype(F32), 0.0)
                s_new = dst[h] * f + lax.dot_general(v_g, k_g, tn_dims, preferred_element_type=F32)
                dst[h] = s_new
                o_g = lax.dot_general(q_ref[grp, sl].astype(F32), s_new, nt_dims,
                                      preferred_element_type=F32)
                o_ref[grp, sv] = jnp.where(live, o_g, o_ref[grp, sv].astype(F32)).astype(o_ref.dtype)
            return carry

        lax.fori_loop(0, c, step, 0)

    def chunk_group(jj, carry):
        bases = [chunk_base(group * jj + g) for g in range(group)]
        rows = [pl.ds(b, c) for b in bases]
        span = jnp.zeros((1, dk), F32)
        from_a = stage_a(rows[0])
        for g in range(group):
            from_b, span_g = stage_b(rows[g], from_a, states[g], states[g + 1])
            span = jnp.maximum(span, span_g)
            if g + 1 < group:
                from_a = stage_a(rows[g + 1])
            stage_c(rows[g], from_b)

        @pl.when(jnp.max(span) > HG_SAFE_SPAN)
        def _():
            for g in range(group):
                exact_chunk(bases[g], states[g], states[g + 1])

        states[0][...] = states[group][...]
        return carry

    lax.fori_loop(0, nch // group, chunk_group, 0)
    sfin_ref[...] = states[0][...]


def _hgrn2(proj, q_blk, z_blk, v_blk, lb, s0, tT):
    bsz, t, _ = proj.shape
    _, _, nh, dv, dk = s0.shape
    w = nh * dk
    tT = min(tT, t)
    n = t // tT
    nch = tT // HG_CHUNK
    group = max(g for g in (1, 2, 4, HG_GROUP) if nch % g == 0)

    def tile(d, i):
        return jnp.where(d == 0, i, n - 1 - i)

    return pl.pallas_call(
        _hgrn2_kernel,
        out_shape=(jax.ShapeDtypeStruct((2, bsz, t, nh * dv), PROJ_DTYPE),
                   jax.ShapeDtypeStruct((2, bsz, nh, dv, dk), F32)),
        grid=(2, bsz, n),
        in_specs=[
            pl.BlockSpec((None, tT, w), lambda d, b, i: (b, tile(d, i), q_blk)),
            pl.BlockSpec((None, tT, w), lambda d, b, i: (b, tile(d, i), z_blk + d)),
            pl.BlockSpec((None, tT, w), lambda d, b, i: (b, tile(d, i), v_blk)),
            pl.BlockSpec((None, 1, w), lambda d, b, i: (d, 0, 0)),
            pl.BlockSpec((None, None, nh, dv, dk), lambda d, b, i: (d, b, 0, 0, 0)),
        ],
        out_specs=(
            pl.BlockSpec((None, None, tT, nh * dv), lambda d, b, i: (d, b, tile(d, i), 0)),
            pl.BlockSpec((None, None, nh, dv, dk), lambda d, b, i: (d, b, 0, 0, 0)),
        ),
        scratch_shapes=[pltpu.VMEM((nh, dv, dk), F32)] * (group + 1),
        compiler_params=_cparams(("arbitrary", "arbitrary", "arbitrary")),
        name="hgrn2_bidir_chunk_scan",
    )(proj, proj, proj, lb.reshape(2, 1, w), s0)


HY_MAX_RADIX = 8
HY_MIN_SUBLEN = 128
HY_CHUNK = 8


def _hyena_radix(length):
    return max(1, min(HY_MAX_RADIX, length // HY_MIN_SUBLEN))


def _hyena_tables(length):
    radix = _hyena_radix(length)
    n = length // radix
    k2 = 2 * jnp.arange(n, dtype=jnp.int32) + 1
    ph = (k2[:, None] * jnp.arange(n, dtype=jnp.int32)[None, :]) % (4 * n)
    psi = ph.astype(F32) * (2.0 * math.pi / (4 * n))
    fc, fs = jnp.cos(psi), jnp.sin(psi)
    fwd = jnp.concatenate([fc, fs], axis=0).astype(BF16)
    inv = jnp.concatenate([fc.T, -fs.T], axis=1).astype(BF16)
    phr = (jnp.arange(radix, dtype=jnp.int32)[:, None] * k2[None, :]) % (4 * length)
    phi = phr.astype(F32) * (2.0 * math.pi / (4 * length))
    rep = lambda a: jnp.broadcast_to(a[:, :, None], (radix, n, LANES))
    return fwd, inv, rep(jnp.cos(phi)), rep(jnp.sin(phi))


def _cmul_const(x, c, s):
    re, im = x
    if abs(s) < 1e-12:
        return (re, im) if c > 0 else (-re, -im)
    if abs(c) < 1e-12:
        return (-im, re) if s > 0 else (im, -re)
    return (re * c - im * s, re * s + im * c)


def _fft_list(xs, sign):
    r = len(xs)
    if r == 1:
        return xs
    ev = _fft_list(xs[0::2], sign)
    od = _fft_list(xs[1::2], sign)
    out = [None] * r
    for j in range(r // 2):
        ang = sign * 2.0 * math.pi * j / r
        t = _cmul_const(od[j], math.cos(ang), math.sin(ang))
        out[j] = (ev[j][0] + t[0], ev[j][1] + t[1])
        out[j + r // 2] = (ev[j][0] - t[0], ev[j][1] - t[1])
    return out


def _hy_sub_transforms(slab, fwd_ref, z_ref, radix, n):
    n_slabs = slab.shape[0]
    for r in range(radix):
        parts = [slab[s, pl.ds(r, n, stride=radix), :] if radix > 1 else slab[s]
                 for s in range(n_slabs)]
        xr = parts[0] if n_slabs == 1 else jnp.concatenate(parts, axis=1)
        z_ref[r] = jnp.dot(fwd_ref[...], xr.astype(BF16), preferred_element_type=F32)


def _hy_load_twiddles(twr_ref, twi_ref, rows_c, radix):
    return [(twr_ref[r, rows_c, :], twi_ref[r, rows_c, :]) for r in range(1, radix)]


def _hy_twiddled_inputs(z_ref, tw, rows_c, rows_s, lanes, radix):
    xs = []
    for r in range(radix):
        c = z_ref[r, rows_c, lanes]
        s = z_ref[r, rows_s, lanes]
        if r == 0:
            xs.append((c, -s))
        else:
            wr, wi = tw[r - 1]
            xs.append((c * wr - s * wi, -(c * wi + s * wr)))
    return xs


def _filter_kernel(emb_ref, w1_ref, b1_ref, w2_ref, b2_ref, w3_ref, fr_ref, dl_ref,
                   a_ref, b_ref, ss_ref, *, length):
    i = pl.program_id(0)
    tl = emb_ref.shape[0]
    cw = dl_ref.shape[1]
    fr = fr_ref[...]
    hi = lax.Precision.HIGHEST
    hdn = jnp.sin(fr * (jnp.dot(emb_ref[...], w1_ref[...], preferred_element_type=F32,
                                precision=hi) + b1_ref[...]))
    hdn = jnp.sin(fr * (jnp.dot(hdn, w2_ref[...], preferred_element_type=F32,
                                precision=hi) + b2_ref[...]))
    h = jnp.dot(hdn.astype(BF16), w3_ref[...].astype(BF16), preferred_element_type=F32)
    pos = (lax.broadcasted_iota(jnp.int32, (tl, 1), 0) + i * tl)
    t = pos.astype(F32) / float(length)
    win = jnp.exp(-t * dl_ref[...])
    not_first = pos > 0

    @pl.when(i == 0)
    def _():
        ss_ref[...] = jnp.zeros_like(ss_ref)

    for o in range(HY_ORDER):
        hpos = h[:, (2 * o) * cw:(2 * o + 1) * cw] * win
        hneg = jnp.where(not_first, h[:, (2 * o + 1) * cw:(2 * o + 2) * cw] * win, 0.0)
        a_ref[:, o * cw:(o + 1) * cw] = hpos + hneg
        b_ref[:, o * cw:(o + 1) * cw] = hneg - hpos
        ss_ref[:, o * cw:(o + 1) * cw] += jnp.sum(hpos * hpos + hneg * hneg, axis=0, keepdims=True)


def _spectrum_kernel(a_ref, b_ref, ss_ref, fwd_ref, twr_ref, twi_ref, hr_ref, hi_ref, slab, zab,
                     *, scale):
    radix, n, _ = hr_ref.shape
    slab[0] = a_ref[...]
    slab[1] = b_ref[...]
    _hy_sub_transforms(slab, fwd_ref, zab, radix, n)
    norm = lax.rsqrt(ss_ref[...] + TINY) * scale
    lanes_a, lanes_b = slice(0, LANES), slice(LANES, 2 * LANES)

    def step(i, carry):
        base = pl.multiple_of(i * HY_CHUNK, HY_CHUNK)
        rows_c = pl.ds(base, HY_CHUNK)
        rows_s = pl.ds(base + n, HY_CHUNK)
        tw = _hy_load_twiddles(twr_ref, twi_ref, rows_c, radix)
        fa = _fft_list(_hy_twiddled_inputs(zab, tw, rows_c, rows_s, lanes_a, radix), -1)
        fb = _fft_list(_hy_twiddled_inputs(zab, tw, rows_c, rows_s, lanes_b, radix), -1)
        for j in range(radix):
            hr_ref[j, rows_c, :] = fa[j][0] * norm
            hi_ref[j, rows_c, :] = -fb[j][1] * norm
        return carry

    lax.fori_loop(0, n // HY_CHUNK, step, 0)


def _hyena_spectra(length, tables, w1, b1, w2, b2, w3, freq, width, tl):
    f32 = F32
    pos = jnp.arange(length, dtype=f32)
    t = pos / length
    bands = jnp.linspace(1e-4, HY_BANDS - 1, HY_BANDS, dtype=f32)
    ang = (2.0 * math.pi / length) * pos[:, None] * bands[None, :]
    emb = jnp.concatenate([t[:, None], jnp.cos(ang), jnp.sin(ang)], axis=-1)
    deltas = jnp.abs(jnp.linspace(math.log(HY_TARGET) / HY_SLOW_DECAY,
                                  math.log(HY_TARGET) / HY_FAST_DECAY, width, dtype=f32))
    n_emb = -(-emb.shape[1] // LANES) * LANES
    w1 = jnp.pad(w1.astype(f32), ((0, n_emb - emb.shape[1]), (0, 0)))
    emb = jnp.pad(emb, ((0, 0), (0, n_emb - emb.shape[1])))
    hid = w1.shape[1]
    cols = HY_ORDER * width
    tl = min(tl, length)
    full = lambda *s: pl.BlockSpec(s, lambda i: (0,) * len(s))
    a_un, b_un, ss = pl.pallas_call(
        functools.partial(_filter_kernel, length=length),
        out_shape=(jax.ShapeDtypeStruct((length, cols), f32),
                   jax.ShapeDtypeStruct((length, cols), f32),
                   jax.ShapeDtypeStruct((1, cols), f32)),
        grid=(length // tl,),
        in_specs=[
            pl.BlockSpec((tl, n_emb), lambda i: (i, 0)),
            full(n_emb, hid), full(1, hid), full(hid, hid), full(1, hid),
            full(hid, 2 * cols), full(1, hid), full(1, width),
        ],
        out_specs=(pl.BlockSpec((tl, cols), lambda i: (i, 0)),
                   pl.BlockSpec((tl, cols), lambda i: (i, 0)),
                   pl.BlockSpec((1, cols), lambda i: (0, 0))),
        compiler_params=_cparams(("arbitrary",)),
        name="hyena_filter_mlp",
    )(emb, w1, b1.reshape(1, hid), w2, b2.reshape(1, hid), w3, freq.reshape(1, hid),
      deltas.reshape(1, width))
    fwd, _, twr, twi = tables
    radix, n, _ = twr.shape
    cb = LANES
    return pl.pallas_call(
        functools.partial(_spectrum_kernel, scale=1.0 / length),
        out_shape=(jax.ShapeDtypeStruct((radix, n, cols), f32),
                   jax.ShapeDtypeStruct((radix, n, cols), f32)),
        grid=(cols // cb,),
        in_specs=[
            pl.BlockSpec((length, cb), lambda j: (0, j)),
            pl.BlockSpec((length, cb), lambda j: (0, j)),
            pl.BlockSpec((1, cb), lambda j: (0, j)),
            pl.BlockSpec((2 * n, n), lambda j: (0, 0)),
            pl.BlockSpec((radix, n, LANES), lambda j: (0, 0, 0)),
            pl.BlockSpec((radix, n, LANES), lambda j: (0, 0, 0)),
        ],
        out_specs=(pl.BlockSpec((radix, n, cb), lambda j: (0, 0, j)),
                   pl.BlockSpec((radix, n, cb), lambda j: (0, 0, j))),
        scratch_shapes=[pltpu.VMEM((2, length, LANES), f32),
                        pltpu.VMEM((radix, 2 * n, 2 * LANES), f32)],
        compiler_params=_cparams(("arbitrary",)),
        name="hyena_filter_spectrum",
    )(a_un, b_un, ss, fwd, twr, twi)


def _short_conv(x, w_ref, b_ref):
    length = x.shape[0]
    rows = lax.broadcasted_iota(jnp.int32, (length, 1), 0)
    prev = jnp.where(rows > 0, pltpu.roll(x, 1, axis=0), 0.0)
    nxt = jnp.where(rows < length - 1, pltpu.roll(x, length - 1, axis=0), 0.0)
    return w_ref[0:1, :] * prev + w_ref[1:2, :] * x + w_ref[2:3, :] * nxt + b_ref[...]


def _hyena_conv_kernel(*refs, conv_a, has_gate, m_blk, g_blk):
    it = iter(refs)
    a_ref = next(it)
    aw_ref = ab_ref = None
    if conv_a:
        aw_ref, ab_ref = next(it), next(it)
    m_hbm, mw_ref, mb_ref = next(it), next(it), next(it)
    g_hbm = next(it) if has_gate else None
    skip_ref, fwd_ref, inv_ref, twr_ref, twi_ref, hr_ref, hi_ref = (next(it) for _ in range(7))
    o_ref = next(it)
    slab, z_s = next(it), next(it)
    a_keep = next(it) if conv_a else None
    m_ref = next(it)
    g_ref = next(it) if has_gate else None
    sems = next(it)
    radix, n, cb = hr_ref.shape
    tiles = [slice(t * LANES, (t + 1) * LANES) for t in range(cb // LANES)]

    def side_copies():
        c, b = pl.program_id(0), pl.program_id(1)
        cps = [pltpu.make_async_copy(m_hbm.at[b, :, pl.ds((m_blk + c) * cb, cb)], m_ref, sems.at[0])]
        if has_gate:
            cps.append(pltpu.make_async_copy(g_hbm.at[b, :, pl.ds((g_blk + c) * cb, cb)], g_ref,
                                             sems.at[1]))
        return cps

    for cp in side_copies():
        cp.start()

    def a_value():
        a = a_ref[...].astype(F32)
        return _short_conv(a, aw_ref, ab_ref) if conv_a else a

    a = a_value()
    if conv_a:
        a_keep[...] = a
    for s, lanes in enumerate(tiles):
        slab[s] = a[:, lanes]
    _hy_sub_transforms(slab, fwd_ref, z_s, radix, n)

    def step(i, carry):
        base = pl.multiple_of(i * HY_CHUNK, HY_CHUNK)
        rows_c = pl.ds(base, HY_CHUNK)
        rows_s = pl.ds(base + n, HY_CHUNK)
        tw = _hy_load_twiddles(twr_ref, twi_ref, rows_c, radix)
        for lanes in tiles:
            zf = _fft_list(_hy_twiddled_inputs(z_s, tw, rows_c, rows_s, lanes, radix), -1)
            ys = []
            for j in range(radix):
                hr = hr_ref[j, rows_c, lanes]
                hi = hi_ref[j, rows_c, lanes]
                zr, zi = zf[j]
                ys.append((zr * hr - zi * hi, zr * hi + zi * hr))
            vs = _fft_list(ys, +1)
            for r in range(radix):
                ur, ui = vs[r]
                if r > 0:
                    wr, wi = tw[r - 1]
                    ur, ui = wr * ur - wi * ui, wr * ui + wi * ur
                z_s[r, rows_c, lanes] = ur
                z_s[r, rows_s, lanes] = ui
        return carry

    lax.fori_loop(0, n // HY_CHUNK, step, 0)

    for r in range(radix):
        yr = jnp.dot(inv_ref[...], z_s[r].astype(BF16), preferred_element_type=F32)
        for s, lanes in enumerate(tiles):
            if radix > 1:
                slab[s, pl.ds(r, n, stride=radix), :] = yr[:, lanes]
            else:
                slab[s] = yr[:, lanes]

    for cp in side_copies():
        cp.wait()
    a = a_keep[...] if conv_a else a_value()
    m = _short_conv(m_ref[...].astype(F32), mw_ref, mb_ref)
    for s, lanes in enumerate(tiles):
        y = m[:, lanes] * (slab[s] + skip_ref[:, lanes] * a[:, lanes])
        if has_gate:
            y = y * _silu(g_ref[:, lanes].astype(F32))
        o_ref[:, lanes] = y.astype(o_ref.dtype)


def _hyena_conv(a_src, a_blk, a_conv, m_src, m_blk, m_conv, g_src, g_blk, skip, tables, spectra,
                order, width, cb, out_dtype):
    bsz, length, _ = m_src.shape
    fwd, inv, twr, twi = tables
    hr, hi = spectra
    radix, n, _ = twr.shape
    nc = width // cb
    conv_a = a_conv is not None
    has_gate = g_src is not None
    once = dict(pipeline_mode=pl.Buffered(1))
    inputs, specs = [], []

    def add(arr, spec):
        inputs.append(arr)
        specs.append(spec)

    add(a_src, pl.BlockSpec((None, length, cb), lambda c, b: (b, 0, a_blk + c)))
    if conv_a:
        add(a_conv[0], pl.BlockSpec((HY_CONV, cb), lambda c, b: (0, a_conv[2] + c)))
        add(a_conv[1], pl.BlockSpec((1, cb), lambda c, b: (0, a_conv[2] + c)))
    add(m_src, pl.BlockSpec(memory_space=pl.ANY))
    add(m_conv[0], pl.BlockSpec((HY_CONV, cb), lambda c, b: (0, m_conv[2] + c)))
    add(m_conv[1], pl.BlockSpec((1, cb), lambda c, b: (0, m_conv[2] + c)))
    if has_gate:
        add(g_src, pl.BlockSpec(memory_space=pl.ANY))
    add(skip, pl.BlockSpec((None, 1, cb), lambda c, b: (order, 0, c)))
    add(fwd, pl.BlockSpec((2 * n, n), lambda c, b: (0, 0), **once))
    add(inv, pl.BlockSpec((n, 2 * n), lambda c, b: (0, 0), **once))
    add(twr, pl.BlockSpec((radix, n, LANES), lambda c, b: (0, 0, 0), **once))
    add(twi, pl.BlockSpec((radix, n, LANES), lambda c, b: (0, 0, 0), **once))
    add(hr, pl.BlockSpec((radix, n, cb), lambda c, b: (0, 0, order * nc + c), **once))
    add(hi, pl.BlockSpec((radix, n, cb), lambda c, b: (0, 0, order * nc + c), **once))
    side_bufs = [pltpu.VMEM((length, cb), m_src.dtype)] * (2 if has_gate else 1)
    return pl.pallas_call(
        functools.partial(_hyena_conv_kernel, conv_a=conv_a, has_gate=has_gate, m_blk=m_blk,
                          g_blk=g_blk),
        out_shape=jax.ShapeDtypeStruct((bsz, length, width), out_dtype),
        grid=(nc, bsz),
        in_specs=specs,
        out_specs=pl.BlockSpec((None, length, cb), lambda c, b: (b, 0, c)),
        scratch_shapes=[pltpu.VMEM((cb // LANES, length, LANES), F32),
                        pltpu.VMEM((radix, 2 * n, cb), F32)]
                       + ([pltpu.VMEM((length, cb), F32)] if conv_a else []) + side_bufs
                       + [pltpu.SemaphoreType.DMA((2,))],
        compiler_params=_cparams(("arbitrary", "arbitrary")),
        name="hyena_long_conv_o%d" % order,
    )(*inputs)


def _outproj_kernel(x_ref, ya_ref, ga_ref, yb_ref, of_ref, ob_ref, gc_ref, hgn_ref, w_ref, mod_ref,
                    fg_ref, o_ref, *, final, n_heads):
    wa = ya_ref.shape[1]
    wb = yb_ref.shape[1]
    wc = of_ref.shape[1]
    dv = wc // n_heads
    ya = (ya_ref[...].astype(F32) * _silu(ga_ref[...].astype(F32))).astype(BF16)
    acc = jnp.dot(ya, w_ref[0:wa, :], preferred_element_type=F32)
    acc = acc + jnp.dot(yb_ref[...].astype(BF16), w_ref[wa:wa + wb, :], preferred_element_type=F32)
    oc = of_ref[...].astype(F32) + ob_ref[...].astype(F32)
    gate_c = _silu(gc_ref[...].astype(F32)) * hgn_ref[...]
    for h in range(n_heads):
        sl = slice(h * dv, (h + 1) * dv)
        och = oc[:, sl]
        ms = jnp.mean(och * och, axis=-1, keepdims=True)
        ych = (och * lax.rsqrt(ms + EPS) * gate_c[:, sl]).astype(BF16)
        acc = acc + jnp.dot(ych, w_ref[wa + wb + h * dv:wa + wb + (h + 1) * dv, :],
                            preferred_element_type=F32)
    xn = x_ref[...] + mod_ref[2:3, :] * acc
    if final:
        ms = jnp.mean(xn * xn, axis=-1, keepdims=True)
        xn = xn * lax.rsqrt(ms + EPS) * fg_ref[...]
    o_ref[...] = xn


def _out_projection(x, ya, proj, ga_blk, yb, o_dirs, gc_blk, hg_norm, w_out_bf16, mod, final_g, final,
                    tm):
    bsz, t, d = x.shape
    wa = ya.shape[2]
    wb = yb.shape[2]
    wc = o_dirs.shape[3]
    tm = min(tm, t)
    return pl.pallas_call(
        functools.partial(_outproj_kernel, final=final, n_heads=HG_HEADS),
        out_shape=jax.ShapeDtypeStruct((bsz, t, d), F32),
        grid=(bsz, t // tm),
        in_specs=[
            pl.BlockSpec((None, tm, d), lambda b, i: (b, i, 0)),
            pl.BlockSpec((None, tm, wa), lambda b, i: (b, i, 0)),
            pl.BlockSpec((None, tm, wa), lambda b, i: (b, i, ga_blk)),
            pl.BlockSpec((None, tm, wb), lambda b, i: (b, i, 0)),
            pl.BlockSpec((None, None, tm, wc), lambda b, i: (0, b, i, 0)),
            pl.BlockSpec((None, None, tm, wc), lambda b, i: (1, b, i, 0)),
            pl.BlockSpec((None, tm, wc), lambda b, i: (b, i, gc_blk)),
            pl.BlockSpec((1, wc), lambda b, i: (0, 0)),
            pl.BlockSpec((wa + wb + wc, d), lambda b, i: (0, 0)),
            pl.BlockSpec((None, 3, d), lambda b, i: (b, 0, 0)),
            pl.BlockSpec((1, d), lambda b, i: (0, 0)),
        ],
        out_specs=pl.BlockSpec((None, tm, d), lambda b, i: (b, i, 0)),
        compiler_params=_cparams(("arbitrary", "arbitrary")),
        name="gate_outproj_residual",
    )(x, ya, proj, yb, o_dirs, o_dirs, proj, hg_norm.reshape(1, wc), w_out_bf16, mod,
      final_g.reshape(1, d))


def _block_diag(w):
    two, nh, n, _ = w.shape
    eye = jnp.eye(nh, dtype=w.dtype)
    dense = jnp.einsum("dhij,hg->dhigj", w, eye).reshape(two, nh * n, nh * n)
    return dense.astype(BF16)


def _mixers(x, mod, norm_g, lp, init, grid_rows, with_output):
    bsz, t, d = x.shape
    wl = lp["lru_w"]
    wh = lp["hy_w"]
    wg = lp["hg_w"]
    if grid_rows is None:
        proj = _in_projection(x.reshape(1, bsz * t, d), mod[:1], norm_g, lp["w_in"], *INPROJ_CTX_TILE)
        proj = proj.reshape(bsz, t, -1)
    else:
        proj = _in_projection(x, mod, norm_g, lp["w_in"], *INPROJ_TILE)
    xa = proj[:, :, :wl].astype(F32)
    if grid_rows is not None:
        x_tm = xa.reshape(bsz, grid_rows, GRID_W, wl).transpose(2, 1, 0, 3).reshape(t, bsz, wl)
    else:
        x_tm = xa.transpose(1, 0, 2)
    h_dirs, h_fin = _rglru(x_tm, init[0], lp["lru_conv_w"], lp["lru_conv_b"], lp["wa_dense"],
                           lp["lru_ba"], lp["wx_dense"], lp["lru_bx"], lp["lru_lam"], tp=LRU_STEPS)
    q_off = 2 * wl + 4 * wh
    o_dirs, s_fin = _hgrn2(proj, q_off // wg, q_off // wg + 1, q_off // wg + 3, lp["hg_lb"], init[1],
                           tT=HG_ROWS)
    states = (h_fin, s_fin)
    if not with_output:
        return None, states
    ya_tm = h_dirs[0] + h_dirs[1]
    if grid_rows is not None:
        ya = ya_tm.reshape(GRID_W, grid_rows, bsz, wl).transpose(2, 1, 0, 3).reshape(bsz, t, wl)
    else:
        ya = ya_tm.transpose(1, 0, 2)
    ya = ya.astype(PROJ_DTYPE)
    tables = lp["dft"][t]
    spectra = _hyena_spectra(t, tables, lp["hy_w1"], lp["hy_b1"], lp["hy_w2"],
                             lp["hy_b2"], lp["hy_w3"], lp["hy_freq"], wh, tl=HY_FILTER_ROWS)
    cb = min(wh, HY_CHANNELS)
    ub_blk = (2 * wl) // cb
    nb = wh // cb
    cw, cbias = lp["hy_conv_w"], lp["hy_conv_b"].reshape(1, -1)
    z = _hyena_conv(proj, ub_blk, (cw, cbias, 0), proj, ub_blk + nb, (cw, cbias, nb), None, 0,
                    lp["hy_skip"], tables, spectra, 0, wh, cb, F32)
    yb = _hyena_conv(z, 0, None, proj, ub_blk + 2 * nb, (cw, cbias, 2 * nb), proj, ub_blk + 3 * nb,
                     lp["hy_skip"], tables, spectra, 1, wh, cb, BF16)
    return (ya, yb, o_dirs, proj), states


def kernel(x, c, ctx, c_ctx, norm_g, w_mod, b_mod, w_in, w_out, lru_conv_w, lru_conv_b, lru_wa, lru_ba,
           lru_wx, lru_bx, lru_lam, hy_conv_w, hy_conv_b, hy_w1, hy_b1, hy_w2, hy_b2, hy_w3, hy_freq,
           hy_skip, hg_lb_logits, hg_norm, final_g):
    bsz, seq, d = x.shape
    depth = w_mod.shape[0]
    ctx_len = ctx.shape[1]
    rows = seq // GRID_W
    wl = lru_lam.shape[-1]
    wh = hy_skip.shape[-1]
    wg = hg_norm.shape[-1]
    dk = wg // HG_HEADS

    p = jax.nn.softmax(hg_lb_logits.astype(F32), axis=0)
    lower_bounds = jnp.cumsum(p, axis=0) - p[0]

    n_rows = -(-(bsz + 1) // SUBLANES) * SUBLANES
    cvec = jnp.zeros((n_rows, d), F32).at[:bsz].set(c).at[bsz].set(c_ctx)
    mod_all = _modulation(cvec, w_mod, b_mod)

    dft = {length: _hyena_tables(length) for length in (seq, ctx_len)}

    zero_states = (jnp.zeros((2, bsz, wl), F32), jnp.zeros((2, bsz, HG_HEADS, dk, dk), F32))
    xc = ctx
    for layer in range(depth):
        last = layer == depth - 1
        mod = mod_all[layer, :bsz].reshape(bsz, 3, d)
        mod_c = jnp.broadcast_to(mod_all[layer, bsz].reshape(1, 3, d), (bsz, 3, d))
        lp = dict(
            w_in=w_in[layer].astype(BF16), lru_w=wl, hy_w=wh, hg_w=wg,
            lru_conv_w=lru_conv_w[layer], lru_conv_b=lru_conv_b[layer],
            wa_dense=_block_diag(lru_wa[layer]), wx_dense=_block_diag(lru_wx[layer]),
            lru_ba=lru_ba[layer], lru_bx=lru_bx[layer], lru_lam=lru_lam[layer],
            hy_conv_w=hy_conv_w[layer], hy_conv_b=hy_conv_b[layer], hy_w1=hy_w1[layer],
            hy_b1=hy_b1[layer], hy_w2=hy_w2[layer], hy_b2=hy_b2[layer], hy_w3=hy_w3[layer],
            hy_freq=hy_freq[layer], hy_skip=hy_skip[layer].reshape(HY_ORDER, 1, wh),
            hg_lb=lower_bounds[layer], dft=dft)
        w_out_l = w_out[layer].astype(BF16)
        ga_blk = 1
        gc_blk = (2 * wl + 4 * wh + 4 * wg) // wg
        pieces_c, ctx_states = _mixers(xc, mod_c, norm_g[layer], lp, zero_states, None, not last)
        pieces, _ = _mixers(x, mod, norm_g[layer], lp, ctx_states, rows, True)
        ya, yb, o_dirs, proj = pieces
        x = _out_projection(x, ya, proj, ga_blk, yb, o_dirs, gc_blk, hg_norm[layer], w_out_l, mod,
                            final_g, last, tm=OUTPROJ_ROWS)
        if not last:
            ya, yb, o_dirs, proj = pieces_c
            xc = _out_projection(xc, ya, proj, ga_blk, yb, o_dirs, gc_blk, hg_norm[layer], w_out_l,
                                 mod_c, final_g, False, tm=OUTPROJ_ROWS)
    return x
```
